```python
import math
import jax, jax.numpy as jnp
from jax import lax
import numpy as np

D_MODEL = 1024
BATCH = 8
SEQ = 4096
DEPTH = 4

N_MIXERS = 2
EPS = 1e-6
NEG_INF = -1e30

GMLP_CHUNK = 128
GMLP_WIDTH = 3 * D_MODEL
GMLP_GROUPS = 8
GMLP_GROUP_WIDTH = GMLP_WIDTH // GMLP_GROUPS

DIL_PATTERNS = ((128, 1), (512, 4), (2048, 16))
N_DIL_GROUPS = len(DIL_PATTERNS)
HEADS_PER_GROUP = 8
HEAD_DIM = 64
ATTN_WIDTH = HEADS_PER_GROUP * HEAD_DIM
QKV_WIDTH = 3 * N_DIL_GROUPS * ATTN_WIDTH
QUERY_BLOCK = 128

N_EXPERTS = 16
CAPACITY_FACTOR = 2
D_EXPERT = 2 * D_MODEL

kernel_name = "hybrid_gmlp_dilated_attn_ec_moe_encoder"


def rmsnorm(x, gain):
    xf = x.astype(jnp.float32)
    y = xf * lax.rsqrt(jnp.mean(xf * xf, axis=-1, keepdims=True) + EPS)
    return (y * gain.astype(jnp.float32)).astype(x.dtype)


def head_rmsnorm(t, gain):
    tf = t.astype(jnp.float32)
    y = tf * lax.rsqrt(jnp.mean(tf * tf, axis=-1, keepdims=True) + EPS)
    return y * gain.astype(jnp.float32)[:, None, :]


def alibi_slopes(n_heads):
    return jnp.exp2(-8.0 * jnp.arange(1, n_heads + 1, dtype=jnp.float32) / n_heads)


def gmlp_mixer(h, w_in, b_in, v_gain, w_s, b_s, w_out, b_out):
    B, S, _ = h.shape
    z = jax.nn.gelu(h @ w_in + b_in)
    u, v = z[..., :GMLP_WIDTH], z[..., GMLP_WIDTH:]
    v = rmsnorm(v, v_gain)
    n_chunks = S // GMLP_CHUNK
    v = v.reshape(B, n_chunks, GMLP_CHUNK, GMLP_GROUPS, GMLP_GROUP_WIDTH)
    v = jnp.einsum('gts,bcsge->bctge', w_s, v) + b_s.T[None, None, :, :, None]
    return (u * v.reshape(B, S, GMLP_WIDTH)) @ w_out + b_out


def dilated_band_attention(q, k, v, dil, half, slopes):
    B, S, H, Dh = q.shape
    L = S // dil
    N = B * dil

    def to_sub(t):
        return t.reshape(B, L, dil, H, Dh).transpose(0, 2, 1, 3, 4).reshape(N, L, H, Dh)

    qs, ks, vs = to_sub(q), to_sub(k), to_sub(v.astype(jnp.float32))
    nb = -(-L // QUERY_BLOCK)
    Lp = nb * QUERY_BLOCK
    KB = QUERY_BLOCK + 2 * half
    qs = jnp.pad(qs, ((0, 0), (0, Lp - L), (0, 0), (0, 0)))
    kpad = ((0, 0), (half, Lp - L + half), (0, 0), (0, 0))
    ks, vs = jnp.pad(ks, kpad), jnp.pad(vs, kpad)
    kidx = jnp.arange(nb)[:, None] * QUERY_BLOCK + jnp.arange(KB)[None, :]
    kb, vb = ks[:, kidx], vs[:, kidx]
    qb = qs.reshape(N, nb, QUERY_BLOCK, H, Dh)
    s = jnp.einsum('nbqhd,nbkhd->nbhqk', qb, kb) / math.sqrt(Dh)
    rel = jnp.arange(KB)[None, :] - half - jnp.arange(QUERY_BLOCK)[:, None]
    kpos = kidx - half
    valid = (jnp.abs(rel)[None] <= half) & ((kpos >= 0) & (kpos < L))[:, None, :]
    bias = -slopes[:, None, None] * (dil * jnp.abs(rel)).astype(jnp.float32)[None]
    s = jnp.where(valid[None, :, None], s + bias[None, None], NEG_INF)
    lse = jax.nn.logsumexp(s, axis=-1)
    p = jnp.exp(s - lse[..., None])
    o = jnp.einsum('nbhqk,nbkhd->nbqhd', p, vb).reshape(N, Lp, H, Dh)[:, :L]
    lse = lse.transpose(0, 1, 3, 2).reshape(N, Lp, H)[:, :L]
    o = o.reshape(B, dil, L, H, Dh).transpose(0, 2, 1, 3, 4).reshape(B, S, H, Dh)
    lse = lse.reshape(B, dil, L, H).transpose(0, 2, 1, 3).reshape(B, S, H)
    return o, lse


def dilated_mixer(h, w_qkv, q_gain, k_gain, w_o):
    B, S, _ = h.shape
    qkv = (h @ w_qkv).reshape(B, S, 3, N_DIL_GROUPS, HEADS_PER_GROUP, HEAD_DIM)
    q = head_rmsnorm(qkv[:, :, 0], q_gain)
    k = head_rmsnorm(qkv[:, :, 1], k_gain)
    v = qkv[:, :, 2]
    slopes = alibi_slopes(HEADS_PER_GROUP)
    outs, lses = [], []
    for g, (window, dil) in enumerate(DIL_PATTERNS):
        o, l = dilated_band_attention(q[:, :, g], k[:, :, g], v[:, :, g], dil, window // (2 * dil), slopes)
        outs.append(o)
        lses.append(l)
    wts = jax.nn.softmax(jnp.stack(lses, 0), axis=0)
    o = jnp.einsum('gbsh,gbshd->bshd', wts, jnp.stack(outs, 0))
    return o.reshape(B, S, ATTN_WIDTH).astype(h.dtype) @ w_o


def ec_moe(h, w_router, b_router, w1, w3, w2):
    B, S, D = h.shape
    cap = CAPACITY_FACTOR * S // N_EXPERTS
    logits = jnp.einsum('bsd,de->bse', h, w_router).astype(jnp.float32) + b_router.astype(jnp.float32)
    aff = jax.nn.softmax(logits, axis=-1)
    gates, idx = lax.top_k(jnp.swapaxes(aff, 1, 2), cap)
    xin = jax.vmap(lambda hb, ib: hb[ib])(h, idx)
    hid = jax.nn.silu(jnp.einsum('becd,edf->becf', xin, w1)) * jnp.einsum('becd,edf->becf', xin, w3)
    y = jnp.einsum('becf,efd->becd', hid, w2) * gates[..., None].astype(h.dtype)
    return jax.vmap(lambda ib, yb: jax.ops.segment_sum(yb.reshape(-1, D), ib.reshape(-1), num_segments=S))(idx, y)


def setup_inputs(seed: int = 0) -> dict:
    key = jax.random.key(seed)
    ks = jax.random.split(key, 20)
    f32 = jnp.float32
    n_a = (DEPTH + 1) // N_MIXERS
    n_b = DEPTH // N_MIXERS

    def nrm(k, shape, scale):
        return jax.random.normal(k, shape, f32) * scale

    return {
        "x": nrm(ks[0], (BATCH, SEQ, D_MODEL), 1.0),
        "mix_norm": 1.0 + nrm(ks[1], (DEPTH, D_MODEL), 0.02),
        "ffn_norm": 1.0 + nrm(ks[2], (DEPTH, D_MODEL), 0.02),
        "gm_w_in": nrm(ks[3], (n_a, D_MODEL, 2 * GMLP_WIDTH), D_MODEL ** -0.5),
        "gm_b_in": nrm(ks[4], (n_a, 2 * GMLP_WIDTH), 0.02),
        "gm_v_norm": 1.0 + nrm(ks[5], (n_a, GMLP_WIDTH), 0.02),
        "gm_w_s": nrm(ks[6], (n_a, GMLP_GROUPS, GMLP_CHUNK, GMLP_CHUNK), GMLP_CHUNK ** -0.5),
        "gm_b_s": 1.0 + nrm(ks[7], (n_a, GMLP_GROUPS, GMLP_CHUNK), 0.02),
        "gm_w_out": nrm(ks[8], (n_a, GMLP_WIDTH, D_MODEL), GMLP_WIDTH ** -0.5),
        "gm_b_out": nrm(ks[9], (n_a, D_MODEL), 0.02),
        "at_w_qkv": nrm(ks[10], (n_b, D_MODEL, QKV_WIDTH), D_MODEL ** -0.5),
        "at_q_norm": 1.0 + nrm(ks[11], (n_b, N_DIL_GROUPS, HEAD_DIM), 0.02),
        "at_k_norm": 1.0 + nrm(ks[12], (n_b, N_DIL_GROUPS, HEAD_DIM), 0.02),
        "at_w_o": nrm(ks[13], (n_b, ATTN_WIDTH, D_MODEL), ATTN_WIDTH ** -0.5),
        "moe_w_router": nrm(ks[14], (DEPTH, D_MODEL, N_EXPERTS), D_MODEL ** -0.5),
        "moe_b_router": nrm(ks[15], (DEPTH, N_EXPERTS), 0.01),
        "moe_w1": nrm(ks[16], (DEPTH, N_EXPERTS, D_MODEL, D_EXPERT), D_MODEL ** -0.5),
        "moe_w3": nrm(ks[17], (DEPTH, N_EXPERTS, D_MODEL, D_EXPERT), D_MODEL ** -0.5),
        "moe_w2": nrm(ks[18], (DEPTH, N_EXPERTS, D_EXPERT, D_MODEL), D_EXPERT ** -0.5),
    }


def reference(x, mix_norm, ffn_norm, gm_w_in, gm_b_in, gm_v_norm, gm_w_s, gm_b_s, gm_w_out, gm_b_out,
              at_w_qkv, at_q_norm, at_k_norm, at_w_o,
              moe_w_router, moe_b_router, moe_w1, moe_w3, moe_w2):
    for i in range(DEPTH):
        j = i // N_MIXERS
        h = rmsnorm(x, mix_norm[i])
        if i % N_MIXERS == 0:
            x = x + gmlp_mixer(h, gm_w_in[j], gm_b_in[j], gm_v_norm[j], gm_w_s[j], gm_b_s[j],
                               gm_w_out[j], gm_b_out[j])
        else:
            x = x + dilated_mixer(h, at_w_qkv[j], at_q_norm[j], at_k_norm[j], at_w_o[j])
        h = rmsnorm(x, ffn_norm[i])
        x = x + ec_moe(h, moe_w_router[i], moe_b_router[i], moe_w1[i], moe_w3[i], moe_w2[i])
    return x
```

```python
import functools

import jax
import jax.numpy as jnp
from jax import lax
from jax.experimental import pallas as pl
from jax.experimental.pallas import tpu as pltpu

F32 = jnp.float32
BF16 = jnp.bfloat16
I32 = jnp.int32

EPS = 1e-6
NEG_INF = -1e30

LANES = 128
MXU_DIM = 256
VMEM_LIMIT_BYTES = 56 * 1024 * 1024

GMLP_CHUNK = 128
GMLP_GROUPS = 8
DIL_PATTERNS = ((128, 1), (512, 4), (2048, 16))
N_GROUPS = len(DIL_PATTERNS)
HEADS = 8
HEAD_DIM = 64
ATTN_WIDTH = HEADS * HEAD_DIM
QUERY_BLOCK = 128
HALF_WINDOW = 64
N_EXPERTS = 16
CAPACITY_FACTOR = 2

GMLP_ROWS = 256
QKV_ROWS = 256
ROUTER_ROWS = 512
TOKEN_BLOCK = MXU_DIM
GATHER_WINDOW = 128
SCATTER_WINDOW = MXU_DIM
LO_STRIDE = 32


def _params(*semantics):
    return pltpu.CompilerParams(dimension_semantics=semantics, vmem_limit_bytes=VMEM_LIMIT_BYTES)


def _const_spec(shape):
    zeros = (0,) * len(shape)
    return pl.BlockSpec(shape, lambda *_: zeros)


def _rms(x, gain):
    return x * lax.rsqrt(jnp.mean(x * x, axis=-1, keepdims=True) + EPS) * gain


def _dot(a, b):
    return jnp.dot(a, b, preferred_element_type=F32)


def _gmlp_body(x_ref, g_ref, win_ref, bin_ref, vg_ref, ws_ref, bst_ref, wout_ref, bout_ref, o_ref,
               v_s, uv_s):
    x = x_ref[...]
    rows, width = v_s.shape
    gw = width // GMLP_GROUPS
    h = _rms(x, g_ref[...]).astype(BF16)
    zv = jax.nn.gelu(_dot(h, win_ref[:, width:]) + bin_ref[:, width:])
    r = lax.rsqrt(jnp.mean(zv * zv, axis=-1, keepdims=True) + EPS)
    v_s[...] = (zv * r * vg_ref[...]).astype(BF16)
    pair = 2 * gw
    for p in range(GMLP_GROUPS // 2):
        zu = jax.nn.gelu(_dot(h, win_ref[:, p * pair:(p + 1) * pair]) + bin_ref[:, p * pair:(p + 1) * pair])
        for gl in range(2):
            g = 2 * p + gl
            cols = slice(g * gw, (g + 1) * gw)
            bias = bst_ref[:, g:g + 1]
            for c in range(rows // GMLP_CHUNK):
                rs = slice(c * GMLP_CHUNK, (c + 1) * GMLP_CHUNK)
                sv = _dot(ws_ref[g], v_s[rs, cols]) + bias
                uv_s[rs, cols] = (zu[rs, gl * gw:(gl + 1) * gw] * sv).astype(BF16)
    o_ref[...] = x + _dot(uv_s[...], wout_ref[...]) + bout_ref[...]


def _gmlp_layer(x2, gain, w_in, b_in, v_gain, w_s, b_st, w_out, b_out):
    n, d = x2.shape
    width = w_out.shape[0]
    rows = GMLP_ROWS
    return pl.pallas_call(
        _gmlp_body,
        out_shape=jax.ShapeDtypeStruct((n, d), F32),
        grid=(n // rows,),
        in_specs=[
            pl.BlockSpec((rows, d), lambda i: (i, 0)),
            _const_spec(gain.shape), _const_spec(w_in.shape), _const_spec(b_in.shape),
            _const_spec(v_gain.shape), _const_spec(w_s.shape), _const_spec(b_st.shape),
            _const_spec(w_out.shape), _const_spec(b_out.shape),
        ],
        out_specs=pl.BlockSpec((rows, d), lambda i: (i, 0)),
        scratch_shapes=[pltpu.VMEM((rows, width), BF16), pltpu.VMEM((rows, width), BF16)],
        compiler_params=_params("parallel"),
        name="gmlp_layer",
    )(x2, gain, w_in, b_in, v_gain, w_s, b_st, w_out, b_out)


def _qkv_body(x_ref, g_ref, w_ref, qg_ref, kg_ref, bd_ref, o_ref):
    h = _rms(x_ref[...], g_ref[...]).astype(BF16)
    qk_width = N_GROUPS * ATTN_WIDTH
    bd = bd_ref[...]
    for part, gain_ref in ((0, qg_ref), (1, kg_ref)):
        for c in range(qk_width // MXU_DIM):
            cols = slice(part * qk_width + c * MXU_DIM, part * qk_width + (c + 1) * MXU_DIM)
            t = _dot(h, w_ref[:, cols])
            t2 = t * t
            hi = t2.astype(BF16)
            lo = (t2 - hi.astype(F32)).astype(BF16)
            ss = _dot(hi, bd) + _dot(lo, bd)
            tn = t * lax.rsqrt(ss * (1.0 / HEAD_DIM) + EPS) * gain_ref[:, c * MXU_DIM:(c + 1) * MXU_DIM]
            o_ref[:, cols] = tn.astype(BF16)
    o_ref[:, 2 * qk_width:] = _dot(h, w_ref[:, 2 * qk_width:]).astype(BF16)


def _qkv_proj(x2, gain, w_qkv, q_gain, k_gain, head_ones):
    n, d = x2.shape
    width = w_qkv.shape[1]
    rows = QKV_ROWS
    return pl.pallas_call(
        _qkv_body,
        out_shape=jax.ShapeDtypeStruct((n, width), BF16),
        grid=(n // rows,),
        in_specs=[
            pl.BlockSpec((rows, d), lambda i: (i, 0)),
            _const_spec(gain.shape), _const_spec(w_qkv.shape), _const_spec(q_gain.shape),
            _const_spec(k_gain.shape), _const_spec(head_ones.shape),
        ],
        out_specs=pl.BlockSpec((rows, width), lambda i: (i, 0)),
        compiler_params=_params("parallel"),
        name="qkv_proj",
    )(x2, gain, w_qkv, q_gain, k_gain, head_ones)


def _attn_body(has_prev, is_last, *refs):
    q_ref, kp_ref, kc_ref, kn_ref, vp_ref, vc_ref, vn_ref, bias_ref = refs[:8]
    rest = refs[8:]
    if has_prev:
        oprev_ref, lprev_ref = rest[:2]
        rest = rest[2:]
    if is_last:
        x_ref, wo_ref, out_ref = rest
    else:
        o_ref, l_ref = rest

    hw = HALF_WINDOW
    q = q_ref[0]
    k = jnp.concatenate([kp_ref[0, QUERY_BLOCK - hw:, :], kc_ref[0], kn_ref[0, :hw, :]], axis=0)
    v = jnp.concatenate([vp_ref[0, QUERY_BLOCK - hw:, :], vc_ref[0], vn_ref[0, :hw, :]], axis=0)
    heads_per_tile = MXU_DIM // HEAD_DIM
    head_of_lane = lax.broadcasted_iota(I32, (QUERY_BLOCK, MXU_DIM), 1) // HEAD_DIM
    o_parts, l_parts = [], []
    for half in range(ATTN_WIDTH // MXU_DIM):
        cs = slice(half * MXU_DIM, (half + 1) * MXU_DIM)
        qh, kh, vh = q[:, cs], k[:, cs], v[:, cs]
        qs = jnp.concatenate(
            [jnp.where(head_of_lane == hh, qh, jnp.zeros_like(qh)) for hh in range(heads_per_tile)], axis=0)
        s = lax.dot_general(qs, kh, (((1,), (1,)), ((), ())), preferred_element_type=F32)
        s = s + bias_ref[0, half]
        m = jnp.max(s, axis=-1, keepdims=True)
        p = jnp.exp(s - m)
        l = jnp.sum(p, axis=-1, keepdims=True)
        pv = _dot(p.astype(BF16), vh) / l
        lse = m + jnp.log(l)
        o_half = jnp.zeros((QUERY_BLOCK, MXU_DIM), F32)
        l_half = jnp.zeros((QUERY_BLOCK, MXU_DIM), F32)
        for hh in range(heads_per_tile):
            rs = slice(hh * QUERY_BLOCK, (hh + 1) * QUERY_BLOCK)
            sel = head_of_lane == hh
            o_half = jnp.where(sel, pv[rs], o_half)
            l_half = jnp.where(sel, lse[rs], l_half)
        o_parts.append(o_half)
        l_parts.append(l_half)
    o = jnp.concatenate(o_parts, axis=1)
    lse = jnp.concatenate(l_parts, axis=1)
    if has_prev:
        lprev = lprev_ref[0]
        mx = jnp.maximum(lprev, lse)
        e_prev = jnp.exp(lprev - mx)
        e_cur = jnp.exp(lse - mx)
        den = e_prev + e_cur
        o = (oprev_ref[0] * e_prev + o * e_cur) / den
        lse = mx + jnp.log(den)
    if is_last:
        out_ref[0] = x_ref[0] + _dot(o.astype(BF16), wo_ref[...])
    else:
        o_ref[0] = o
        l_ref[0] = lse


def _attn_group(g, qkv, bias, prev, last):
    bsz, seq, width = qkv.shape
    dil = DIL_PATTERNS[g][1]
    sub_len = seq // dil
    nb = sub_len // QUERY_BLOCK
    col_blocks = width // ATTN_WIDTH
    qb = QUERY_BLOCK

    def col(which):
        return which * N_GROUPS + g

    def spec(which, shift):
        def index(b, r, i):
            return (b, jnp.clip(i + shift, 0, nb - 1), r * col_blocks + col(which))
        return pl.BlockSpec((1, qb, ATTN_WIDTH), index)

    def bias_index(b, r, i):
        return (jnp.where(i == 0, 0, jnp.where(i == nb - 1, 2, 1)), 0, 0, 0)

    def strided(w):
        return pl.BlockSpec((1, qb, w), lambda b, r, i: (b, i, r))

    qkv_v = qkv.reshape(bsz, sub_len, dil * width)
    args = [qkv_v] * 7 + [bias]
    in_specs = [spec(0, 0), spec(1, -1), spec(1, 0), spec(1, 1), spec(2, -1), spec(2, 0), spec(2, 1),
                pl.BlockSpec((1,) + bias.shape[1:], bias_index)]
    if prev is not None:
        args += [a.reshape(bsz, sub_len, dil * ATTN_WIDTH) for a in prev]
        in_specs += [strided(ATTN_WIDTH), strided(ATTN_WIDTH)]
    if last is not None:
        x, w_o = last
        d = x.shape[-1]
        args += [x.reshape(bsz, sub_len, dil * d), w_o]
        in_specs += [strided(d), _const_spec(w_o.shape)]
        out_shape = jax.ShapeDtypeStruct((bsz, sub_len, dil * d), F32)
        out_specs = strided(d)
    else:
        out_shape = [jax.ShapeDtypeStruct((bsz, sub_len, dil * ATTN_WIDTH), F32)] * 2
        out_specs = [strided(ATTN_WIDTH), strided(ATTN_WIDTH)]
    out = pl.pallas_call(
        functools.partial(_attn_body, prev is not None, last is not None),
        out_shape=out_shape,
        grid=(bsz, dil, nb),
        in_specs=in_specs,
        out_specs=out_specs,
        compiler_params=_params("parallel", "parallel", "arbitrary"),
        name=f"dilated_attn_g{g}",
    )(*args)
    if last is not None:
        return out.reshape(bsz, seq, -1)
    return tuple(a.reshape(bsz, seq, ATTN_WIDTH) for a in out)


def _attn_bias(dil):
    slopes = jnp.exp2(-8.0 * jnp.arange(1, HEADS + 1, dtype=F32) / HEADS)
    kb = QUERY_BLOCK + 2 * HALF_WINDOW
    rel = jnp.arange(kb)[None, :] - HALF_WINDOW - jnp.arange(QUERY_BLOCK)[:, None]
    band = jnp.abs(rel) <= HALF_WINDOW
    alibi = -slopes[:, None, None] * (dil * jnp.abs(rel)).astype(F32)[None]
    j = jnp.arange(kb)
    edge = jnp.stack([j >= HALF_WINDOW, j >= 0, j < kb - HALF_WINDOW])
    ok = band[None, None] & edge[:, None, None, :]
    tab = jnp.where(ok, alibi[None], NEG_INF)
    per_tile = MXU_DIM // HEAD_DIM
    return tab.reshape(3, HEADS // per_tile, per_tile * QUERY_BLOCK, kb)


def _router_body(x_ref, g_ref, wr_ref, br_ref, hn_ref, aff_ref):
    hn = _rms(x_ref[...], g_ref[...])
    hn_ref[...] = hn.astype(BF16)
    logits = jnp.dot(hn, wr_ref[...], preferred_element_type=F32, precision=lax.Precision.HIGHEST) + br_ref[...]
    m = jnp.max(logits, axis=-1, keepdims=True)
    e = jnp.exp(logits - m)
    aff_ref[...] = e / jnp.sum(e, axis=-1, keepdims=True)


def _router(x2, gain, w_r, b_r):
    n, d = x2.shape
    rows = ROUTER_ROWS
    return pl.pallas_call(
        _router_body,
        out_shape=[jax.ShapeDtypeStruct((n, d), BF16), jax.ShapeDtypeStruct((n, LANES), F32)],
        grid=(n // rows,),
        in_specs=[pl.BlockSpec((rows, d), lambda i: (i, 0)), _const_spec(gain.shape),
                  _const_spec(w_r.shape), _const_spec(b_r.shape)],
        out_specs=[pl.BlockSpec((rows, d), lambda i: (i, 0)), pl.BlockSpec((rows, LANES), lambda i: (i, 0))],
        compiler_params=_params("parallel"),
        name="moe_router",
    )(x2, gain, w_r, b_r)


def _select_body(cap, aff_ref, tri_ref, post_ref, gatet_ref, pos_ref, lo_ref):
    seq = aff_ref.shape[1]
    aff_t = aff_ref[0].T[:N_EXPERTS]
    key = pltpu.bitcast(aff_t, I32)
    thr = jnp.zeros((N_EXPERTS, 1), I32)
    for bit in range(30, -1, -1):
        cand = thr | (1 << bit)
        cnt = jnp.sum(jnp.where(key >= cand, 1.0, 0.0), axis=1, keepdims=True)
        thr = jnp.where(cnt >= cap, cand, thr)
    gt = key > thr
    eq = key == thr
    need = cap - jnp.sum(jnp.where(gt, 1.0, 0.0), axis=1, keepdims=True)
    tri = tri_ref[...]
    nblk = seq // TOKEN_BLOCK

    def prefix(mask_f, want_offsets):
        off = jnp.zeros((N_EXPERTS, 1), F32)
        parts, offs = [], []
        for jb in range(nblk):
            blk = mask_f[:, jb * TOKEN_BLOCK:(jb + 1) * TOKEN_BLOCK]
            offs.append(off)
            parts.append(_dot(blk.astype(BF16), tri) + off)
            off = off + jnp.sum(blk, axis=1, keepdims=True)
        offs.append(off)
        return jnp.concatenate(parts, axis=1), (offs if want_offsets else None)

    eq_rank, _ = prefix(jnp.where(eq, 1.0, 0.0), False)
    sel = gt | (eq & (eq_rank < need))
    rank, offs = prefix(jnp.where(sel, 1.0, 0.0), True)
    post = jnp.where(sel, rank, -1.0)
    post_ref[0] = post.astype(I32)
    gatet_ref[0] = jnp.where(sel, aff_t, 0.0)
    padded = jnp.concatenate([post, jnp.full((LANES - N_EXPERTS, seq), -1.0, F32)], axis=0)
    pos_ref[0] = padded.T.astype(I32)
    lane = lax.broadcasted_iota(I32, (N_EXPERTS, LANES), 1)
    lo = jnp.zeros((N_EXPERTS, LANES), F32)
    for jb, off in enumerate(offs):
        lo = jnp.where(lane == jb, off, lo)
    lo_ref[0] = lo.astype(I32)


def _select(aff, tri, cap):
    bsz, seq, _ = aff.shape
    return pl.pallas_call(
        functools.partial(_select_body, cap),
        out_shape=[jax.ShapeDtypeStruct((bsz, N_EXPERTS, seq), I32),
                   jax.ShapeDtypeStruct((bsz, N_EXPERTS, seq), F32),
                   jax.ShapeDtypeStruct((bsz, seq, LANES), I32),
                   jax.ShapeDtypeStruct((bsz, N_EXPERTS, LANES), I32)],
        grid=(bsz,),
        in_specs=[pl.BlockSpec((1, seq, LANES), lambda b: (b, 0, 0)), _const_spec(tri.shape)],
        out_specs=[pl.BlockSpec((1, N_EXPERTS, seq), lambda b: (b, 0, 0)),
                   pl.BlockSpec((1, N_EXPERTS, seq), lambda b: (b, 0, 0)),
                   pl.BlockSpec((1, seq, LANES), lambda b: (b, 0, 0)),
                   pl.BlockSpec((1, N_EXPERTS, LANES), lambda b: (b, 0, 0))],
        compiler_params=_params("parallel"),
        name="moe_select",
    )(aff, tri)


def _gather_body(cap, lo_ref, hn_ref, post_ref, gatet_ref, xin_ref, gate_ref, xin_s, gate_s):
    b = pl.program_id(0)
    e = pl.program_id(1)
    base = (b * N_EXPERTS + e) * LO_STRIDE
    seq = hn_ref.shape[1]
    xin_s[...] = jnp.zeros_like(xin_s)
    gate_s[...] = jnp.zeros_like(gate_s)
    slot_iota = lax.broadcasted_iota(I32, (GATHER_WINDOW, TOKEN_BLOCK), 0)
    for j in range(seq // TOKEN_BLOCK):
        ts = slice(j * TOKEN_BLOCK, (j + 1) * TOKEN_BLOCK)
        lo = lo_ref[base + j]
        hi = lo_ref[base + j + 1]
        start = lax.shift_left(lax.shift_right_logical(lo, 3), 3)
        nwin = jnp.where(hi > lo, lax.shift_right_logical(hi - start + (GATHER_WINDOW - 1), 7), 0)
        pos_row = post_ref[0, 0, :, ts]
        gate_row = gatet_ref[0, 0, :, ts]

        def window(w, carry, ts=ts, start=start, pos_row=pos_row, gate_row=gate_row):
            s0 = pl.multiple_of(start + w * GATHER_WINDOW, 8)
            hit = pos_row == (slot_iota + s0)
            onehot = jnp.where(hit, 1.0, 0.0).astype(BF16)
            xin_s[pl.ds(s0, GATHER_WINDOW), :] += _dot(onehot, hn_ref[0, ts, :])
            gsum = jnp.sum(jnp.where(hit, gate_row, 0.0), axis=1, keepdims=True)
            gate_s[pl.ds(s0, GATHER_WINDOW), :] += jnp.broadcast_to(gsum, (GATHER_WINDOW, LANES))
            return carry

        lax.fori_loop(0, nwin, window, 0)
    xin_ref[0, 0] = xin_s[:cap, :].astype(BF16)
    gate_ref[0, 0] = gate_s[:cap, :]


def _gather(lo_flat, hn, post, gatet, cap):
    bsz, seq, d = hn.shape
    pad_rows = cap + TOKEN_BLOCK
    post4 = post.reshape(bsz, N_EXPERTS, 1, seq)
    gatet4 = gatet.reshape(bsz, N_EXPERTS, 1, seq)
    grid_spec = pltpu.PrefetchScalarGridSpec(
        num_scalar_prefetch=1,
        grid=(bsz, N_EXPERTS),
        in_specs=[pl.BlockSpec((1, seq, d), lambda b, e, lo: (b, 0, 0)),
                  pl.BlockSpec((1, 1, 1, seq), lambda b, e, lo: (b, e, 0, 0)),
                  pl.BlockSpec((1, 1, 1, seq), lambda b, e, lo: (b, e, 0, 0))],
        out_specs=[pl.BlockSpec((1, 1, cap, d), lambda b, e, lo: (b, e, 0, 0)),
                   pl.BlockSpec((1, 1, cap, LANES), lambda b, e, lo: (b, e, 0, 0))],
        scratch_shapes=[pltpu.VMEM((pad_rows, d), F32), pltpu.VMEM((pad_rows, LANES), F32)],
    )
    return pl.pallas_call(
        functools.partial(_gather_body, cap),
        out_shape=[jax.ShapeDtypeStruct((bsz, N_EXPERTS, cap, d), BF16),
                   jax.ShapeDtypeStruct((bsz, N_EXPERTS, cap, LANES), F32)],
        grid_spec=grid_spec,
        compiler_params=_params("parallel", "arbitrary"),
        name="moe_gather",
    )(lo_flat, hn, post4, gatet4)


def _expert_body(xin_ref, gate_ref, w1_ref, w3_ref, w2_ref, y_ref):
    xin = xin_ref[0, 0]
    f = w1_ref.shape[2]
    chunk = 2 * MXU_DIM
    y = jnp.zeros(y_ref.shape[2:], F32)
    for c in range(f // chunk):
        fs = slice(c * chunk, (c + 1) * chunk)
        hid = jax.nn.silu(_dot(xin, w1_ref[0, :, fs])) * _dot(xin, w3_ref[0, :, fs])
        y = y + _dot(hid.astype(BF16), w2_ref[0, fs, :])
    y_ref[0, 0] = (y * gate_ref[0, 0, :, 0:1]).astype(BF16)


def _experts(xin, gate, w1, w3, w2):
    bsz, ne, cap, d = xin.shape
    f = w1.shape[2]
    return pl.pallas_call(
        _expert_body,
        out_shape=jax.ShapeDtypeStruct((bsz, ne, cap, d), BF16),
        grid=(ne, bsz),
        in_specs=[pl.BlockSpec((1, 1, cap, d), lambda e, b: (b, e, 0, 0)),
                  pl.BlockSpec((1, 1, cap, LANES), lambda e, b: (b, e, 0, 0)),
                  pl.BlockSpec((1, d, f), lambda e, b: (e, 0, 0)),
                  pl.BlockSpec((1, d, f), lambda e, b: (e, 0, 0)),
                  pl.BlockSpec((1, f, d), lambda e, b: (e, 0, 0))],
        out_specs=pl.BlockSpec((1, 1, cap, d), lambda e, b: (b, e, 0, 0)),
        compiler_params=_params("parallel", "arbitrary"),
        name="moe_experts",
    )(xin, gate, w1, w3, w2)


def _combine_body(lo_ref, x_ref, y_ref, pos_ref, o_ref, acc_s):
    b = pl.program_id(0)
    j = pl.program_id(1)
    cap = y_ref.shape[2]
    acc_s[...] = x_ref[0]
    pos = pos_ref[0]
    lane = lax.broadcasted_iota(I32, (TOKEN_BLOCK, SCATTER_WINDOW), 1)
    for e in range(N_EXPERTS):
        base = (b * N_EXPERTS + e) * LO_STRIDE
        lo = lo_ref[base + j]
        hi = lo_ref[base + j + 1]
        start = lax.shift_left(lax.shift_right_logical(lo, 4), 4)
        nwin = jnp.where(hi > lo, lax.shift_right_logical(hi - start + (SCATTER_WINDOW - 1), 8), 0)
        pcol = pos[:, e:e + 1]

        def window(w, carry, e=e, start=start, pcol=pcol):
            s0 = start + w * SCATTER_WINDOW
            s0c = pl.multiple_of(jnp.minimum(s0, cap - SCATTER_WINDOW), 16)
            hit = ((pcol - s0c) == lane) & (pcol >= s0)
            onehot = jnp.where(hit, 1.0, 0.0).astype(BF16)
            acc_s[...] += _dot(onehot, y_ref[0, e, pl.ds(s0c, SCATTER_WINDOW), :])
            return carry

        lax.fori_loop(0, nwin, window, 0)
    o_ref[0] = acc_s[...]


def _combine(lo_flat, x, y, pos):
    bsz, seq, d = x.shape
    _, ne, cap, _ = y.shape
    grid_spec = pltpu.PrefetchScalarGridSpec(
        num_scalar_prefetch=1,
        grid=(bsz, seq // TOKEN_BLOCK),
        in_specs=[pl.BlockSpec((1, TOKEN_BLOCK, d), lambda b, j, lo: (b, j, 0)),
                  pl.BlockSpec((1, ne, cap, d), lambda b, j, lo: (b, 0, 0, 0)),
                  pl.BlockSpec((1, TOKEN_BLOCK, LANES), lambda b, j, lo: (b, j, 0))],
        out_specs=pl.BlockSpec((1, TOKEN_BLOCK, d), lambda b, j, lo: (b, j, 0)),
        scratch_shapes=[pltpu.VMEM((TOKEN_BLOCK, d), F32)],
    )
    return pl.pallas_call(
        _combine_body,
        out_shape=jax.ShapeDtypeStruct((bsz, seq, d), F32),
        grid_spec=grid_spec,
        compiler_params=_params("parallel", "arbitrary"),
        name="moe_combine",
    )(lo_flat, x, y, pos)


def _ec_moe(x, gain, w_r, b_r, w1, w3, w2, tri):
    bsz, seq, d = x.shape
    cap = CAPACITY_FACTOR * seq // N_EXPERTS
    hn, aff = _router(x.reshape(bsz * seq, d), gain, w_r, b_r)
    hn = hn.reshape(bsz, seq, d)
    post, gatet, pos, lo = _select(aff.reshape(bsz, seq, LANES), tri, cap)
    lo_flat = lo[:, :, :LO_STRIDE].reshape(-1)
    xin, gate = _gather(lo_flat, hn, post, gatet, cap)
    y = _experts(xin, gate, w1, w3, w2)
    return _combine(lo_flat, x, y, pos)


def kernel(x, mix_norm, ffn_norm, gm_w_in, gm_b_in, gm_v_norm, gm_w_s, gm_b_s, gm_w_out, gm_b_out,
           at_w_qkv, at_q_norm, at_k_norm, at_w_o, moe_w_router, moe_b_router, moe_w1, moe_w3, moe_w2):
    bsz, seq, d = x.shape
    depth = mix_norm.shape[0]
    n_mixers = 2

    lane_head = jnp.arange(MXU_DIM) // HEAD_DIM
    head_ones = (lane_head[:, None] == lane_head[None, :]).astype(BF16)
    tri = (jnp.arange(TOKEN_BLOCK)[:, None] < jnp.arange(TOKEN_BLOCK)[None, :]).astype(BF16)
    biases = [_attn_bias(dil) for _, dil in DIL_PATTERNS]
    pad_e = LANES - N_EXPERTS

    for i in range(depth):
        j = i // n_mixers
        gain = mix_norm[i][None, :]
        if i % n_mixers == 0:
            x = _gmlp_layer(
                x.reshape(bsz * seq, d), gain, gm_w_in[j].astype(BF16), gm_b_in[j][None, :],
                gm_v_norm[j][None, :], gm_w_s[j].astype(BF16), gm_b_s[j].T, gm_w_out[j].astype(BF16),
                gm_b_out[j][None, :]).reshape(bsz, seq, d)
        else:
            q_gain = jnp.tile(at_q_norm[j][:, None, :], (1, HEADS, 1)).reshape(1, -1) * (HEAD_DIM ** -0.5)
            k_gain = jnp.tile(at_k_norm[j][:, None, :], (1, HEADS, 1)).reshape(1, -1)
            qkv = _qkv_proj(x.reshape(bsz * seq, d), gain, at_w_qkv[j].astype(BF16), q_gain, k_gain, head_ones)
            qkv = qkv.reshape(bsz, seq, -1)
            state = None
            for g in range(N_GROUPS):
                last = (x, at_w_o[j].astype(BF16)) if g == N_GROUPS - 1 else None
                state = _attn_group(g, qkv, biases[g], state, last)
            x = state
        w_r = jnp.pad(moe_w_router[i], ((0, 0), (0, pad_e)))
        b_r = jnp.pad(moe_b_router[i], (0, pad_e), constant_values=NEG_INF)[None, :]
        x = _ec_moe(x, ffn_norm[i][None, :], w_r, b_r, moe_w1[i].astype(BF16), moe_w3[i].astype(BF16),
                    moe_w2[i].astype(BF16), tri)
    return x
```

```python
import functools

import jax
import jax.numpy as jnp
from jax import lax
from jax.experimental import pallas as pl
from jax.experimental.pallas import tpu as pltpu

F32 = jnp.float32
BF16 = jnp.bfloat16
I32 = jnp.int32

EPS = 1e-6
NEG_INF = -1e30

LANES = 128
MXU_DIM = 256
VMEM_LIMIT_BYTES = 56 * 1024 * 1024

GMLP_CHUNK = 128
GMLP_GROUPS = 8
DIL_PATTERNS = ((128, 1), (512, 4), (2048, 16))
N_GROUPS = len(DIL_PATTERNS)
HEADS = 8
HEAD_DIM = 64
ATTN_WIDTH = HEADS * HEAD_DIM
GROUP_WIDTH = 3 * ATTN_WIDTH
QUERY_BLOCK = 128
HALF_WINDOW = 64
N_EXPERTS = 16
CAPACITY_FACTOR = 2

GMLP_ROWS = 256
QKV_ROWS = 512
MERGE_ROWS = 256
ROUTER_ROWS = 512
TOKEN_BLOCK = MXU_DIM
GATHER_WINDOW = 128
SCATTER_WINDOW = MXU_DIM
LO_STRIDE = 32


def _params(*semantics):
    return pltpu.CompilerParams(dimension_semantics=semantics, vmem_limit_bytes=VMEM_LIMIT_BYTES)


def _const_spec(shape):
    zeros = (0,) * len(shape)
    return pl.BlockSpec(shape, lambda *_: zeros)


def _rms(x, gain):
    return x * lax.rsqrt(jnp.mean(x * x, axis=-1, keepdims=True) + EPS) * gain


def _dot(a, b):
    return jnp.dot(a, b, preferred_element_type=F32)


def _floor_to(x, multiple):
    shift = multiple.bit_length() - 1
    return lax.shift_left(lax.shift_right_logical(x, shift), shift)


def _ceil_div(x, divisor):
    return lax.shift_right_logical(x + (divisor - 1), divisor.bit_length() - 1)


def _gmlp_body(x_ref, g_ref, win_ref, bin_ref, vg_ref, ws_ref, bst_ref, wout_ref, bout_ref, o_ref,
               v_s, uv_s):
    x = x_ref[...]
    rows, width = v_s.shape
    gw = width // GMLP_GROUPS
    h = _rms(x, g_ref[...]).astype(BF16)
    zv = jax.nn.gelu(_dot(h, win_ref[:, width:]) + bin_ref[:, width:])
    r = lax.rsqrt(jnp.mean(zv * zv, axis=-1, keepdims=True) + EPS)
    v_s[...] = (zv * r * vg_ref[...]).astype(BF16)
    pair = 2 * gw
    for p in range(GMLP_GROUPS // 2):
        zu = jax.nn.gelu(_dot(h, win_ref[:, p * pair:(p + 1) * pair]) + bin_ref[:, p * pair:(p + 1) * pair])
        for gl in range(2):
            g = 2 * p + gl
            cols = slice(g * gw, (g + 1) * gw)
            bias = bst_ref[:, g:g + 1]
            for c in range(rows // GMLP_CHUNK):
                rs = slice(c * GMLP_CHUNK, (c + 1) * GMLP_CHUNK)
                sv = _dot(ws_ref[g], v_s[rs, cols]) + bias
                uv_s[rs, cols] = (zu[rs, gl * gw:(gl + 1) * gw] * sv).astype(BF16)
    o_ref[...] = x + _dot(uv_s[...], wout_ref[...]) + bout_ref[...]


def _gmlp_layer(x2, gain, w_in, b_in, v_gain, w_s, b_st, w_out, b_out):
    n, d = x2.shape
    width = w_out.shape[0]
    rows = GMLP_ROWS
    return pl.pallas_call(
        _gmlp_body,
        out_shape=jax.ShapeDtypeStruct((n, d), F32),
        grid=(n // rows,),
        in_specs=[
            pl.BlockSpec((rows, d), lambda i: (i, 0)),
            _const_spec(gain.shape), _const_spec(w_in.shape), _const_spec(b_in.shape),
            _const_spec(v_gain.shape), _const_spec(w_s.shape), _const_spec(b_st.shape),
            _const_spec(w_out.shape), _const_spec(b_out.shape),
        ],
        out_specs=pl.BlockSpec((rows, d), lambda i: (i, 0)),
        scratch_shapes=[pltpu.VMEM((rows, width), BF16), pltpu.VMEM((rows, width), BF16)],
        compiler_params=_params("parallel"),
        name="gmlp_layer",
    )(x2, gain, w_in, b_in, v_gain, w_s, b_st, w_out, b_out)


def _qkv_body(x_ref, g_ref, w_ref, gain_ref, bd_ref, *rest):
    out_refs, t_s = rest[:N_GROUPS], rest[N_GROUPS]
    rows = t_s.shape[1]
    h = _rms(x_ref[...], g_ref[...]).astype(BF16)
    bd = bd_ref[...]

    def put(col0, t):
        for c in range(t.shape[1] // LANES):
            t_s[col0 // LANES + c] = t[:, c * LANES:(c + 1) * LANES]

    for g in range(N_GROUPS):
        for c in range(2 * ATTN_WIDTH // MXU_DIM):
            col0 = g * GROUP_WIDTH + c * MXU_DIM
            cols = slice(col0, col0 + MXU_DIM)
            t = _dot(h, w_ref[:, cols])
            t2 = t * t
            hi = t2.astype(BF16)
            lo = (t2 - hi.astype(F32)).astype(BF16)
            ss = _dot(hi, bd) + _dot(lo, bd)
            put(col0, t * lax.rsqrt(ss * (1.0 / HEAD_DIM) + EPS) * gain_ref[:, cols])
        col0 = g * GROUP_WIDTH + 2 * ATTN_WIDTH
        put(col0, _dot(h, w_ref[:, col0:col0 + ATTN_WIDTH]))
    for g, (_, dil) in enumerate(DIL_PATTERNS):
        for r in range(dil):
            rs = slice(None) if dil == 1 else pl.ds(r, rows // dil, stride=dil)
            for c in range(GROUP_WIDTH // LANES):
                out_refs[g][0, r, :, c * LANES:(c + 1) * LANES] = (
                    t_s[g * (GROUP_WIDTH // LANES) + c, rs, :].astype(BF16))


def _qkv_proj(x, gain, w_qkv, qk_gain, head_ones):
    bsz, seq, d = x.shape
    width = w_qkv.shape[1]
    rows = QKV_ROWS
    out_shape, out_specs = [], []
    for _, dil in DIL_PATTERNS:
        out_shape.append(jax.ShapeDtypeStruct((bsz, dil, seq // dil, GROUP_WIDTH), BF16))
        out_specs.append(pl.BlockSpec((1, dil, rows // dil, GROUP_WIDTH), lambda b, i: (b, 0, i, 0)))
    return pl.pallas_call(
        _qkv_body,
        out_shape=out_shape,
        grid=(bsz, seq // rows),
        in_specs=[
            pl.BlockSpec((None, rows, d), lambda b, i: (b, i, 0)),
            _const_spec(gain.shape), _const_spec(w_qkv.shape), _const_spec(qk_gain.shape),
            _const_spec(head_ones.shape),
        ],
        out_specs=out_specs,
        scratch_shapes=[pltpu.VMEM((width // LANES, rows, LANES), F32)],
        compiler_params=_params("parallel", "parallel"),
        name="qkv_proj",
    )(x, gain, w_qkv, qk_gain, head_ones)


def _attn_body(q_ref, kp_ref, kc_ref, kn_ref, vp_ref, vc_ref, vn_ref, bias_ref, o_ref, l_ref):
    hw = HALF_WINDOW
    q = q_ref[0, 0]
    k = jnp.concatenate([kp_ref[0, 0, QUERY_BLOCK - hw:, :], kc_ref[0, 0], kn_ref[0, 0, :hw, :]], axis=0)
    v = jnp.concatenate([vp_ref[0, 0, QUERY_BLOCK - hw:, :], vc_ref[0, 0], vn_ref[0, 0, :hw, :]], axis=0)
    heads_per_tile = MXU_DIM // HEAD_DIM
    head_of_lane = lax.broadcasted_iota(I32, (QUERY_BLOCK, MXU_DIM), 1) // HEAD_DIM
    for half in range(ATTN_WIDTH // MXU_DIM):
        cs = slice(half * MXU_DIM, (half + 1) * MXU_DIM)
        qh, kh, vh = q[:, cs], k[:, cs], v[:, cs]
        qs = jnp.concatenate(
            [jnp.where(head_of_lane == hh, qh, jnp.zeros_like(qh)) for hh in range(heads_per_tile)], axis=0)
        s = lax.dot_general(qs, kh, (((1,), (1,)), ((), ())), preferred_element_type=F32)
        s = s + bias_ref[0, half]
        m = jnp.max(s, axis=-1, keepdims=True)
        p = jnp.exp(s - m)
        l = jnp.sum(p, axis=-1, keepdims=True)
        pv = _dot(p.astype(BF16), vh) / l
        lse = m + jnp.log(l)
        o_half = jnp.zeros((QUERY_BLOCK, MXU_DIM), F32)
        l_half = jnp.zeros((QUERY_BLOCK, MXU_DIM), F32)
        for hh in range(heads_per_tile):
            rs = slice(hh * QUERY_BLOCK, (hh + 1) * QUERY_BLOCK)
            sel = head_of_lane == hh
            o_half = jnp.where(sel, pv[rs], o_half)
            l_half = jnp.where(sel, lse[rs], l_half)
        o_ref[0, 0, :, cs] = o_half.astype(BF16)
        l_ref[0, 0, :, cs] = l_half


def _attn_group(g, qkv_g, bias):
    bsz, dil, sub_len, _ = qkv_g.shape
    nb = sub_len // QUERY_BLOCK
    qb = QUERY_BLOCK

    def spec(which, shift):
        def index(b, r, i):
            return (b, r, jnp.clip(i + shift, 0, nb - 1), which)
        return pl.BlockSpec((1, 1, qb, ATTN_WIDTH), index)

    def bias_index(b, r, i):
        return (jnp.where(i == 0, 0, jnp.where(i == nb - 1, 2, 1)), 0, 0, 0)

    out_spec = pl.BlockSpec((1, 1, qb, ATTN_WIDTH), lambda b, r, i: (b, r, i, 0))
    return pl.pallas_call(
        _attn_body,
        out_shape=[jax.ShapeDtypeStruct((bsz, dil, sub_len, ATTN_WIDTH), BF16),
                   jax.ShapeDtypeStruct((bsz, dil, sub_len, ATTN_WIDTH), F32)],
        grid=(bsz, dil, nb),
        in_specs=[spec(0, 0), spec(1, -1), spec(1, 0), spec(1, 1), spec(2, -1), spec(2, 0), spec(2, 1),
                  pl.BlockSpec((1,) + bias.shape[1:], bias_index)],
        out_specs=[out_spec, out_spec],
        compiler_params=_params("parallel", "parallel", "arbitrary"),
        name=f"dilated_attn_g{g}",
    )(*([qkv_g] * 7), bias)


def _attn_bias(dil):
    slopes = jnp.exp2(-8.0 * jnp.arange(1, HEADS + 1, dtype=F32) / HEADS)
    kb = QUERY_BLOCK + 2 * HALF_WINDOW
    rel = jnp.arange(kb)[None, :] - HALF_WINDOW - jnp.arange(QUERY_BLOCK)[:, None]
    band = jnp.abs(rel) <= HALF_WINDOW
    alibi = -slopes[:, None, None] * (dil * jnp.abs(rel)).astype(F32)[None]
    j = jnp.arange(kb)
    edge = jnp.stack([j >= HALF_WINDOW, j >= 0, j < kb - HALF_WINDOW])
    ok = band[None, None] & edge[:, None, None, :]
    tab = jnp.where(ok, alibi[None], NEG_INF)
    per_tile = MXU_DIM // HEAD_DIM
    return tab.reshape(3, HEADS // per_tile, per_tile * QUERY_BLOCK, kb)


def _merge_body(*refs):
    o_refs, l_refs = refs[:N_GROUPS], refs[N_GROUPS:2 * N_GROUPS]
    x_ref, wo_ref, out_ref, o_s, l_s = refs[2 * N_GROUPS:]
    rows = x_ref.shape[0]
    tiles = ATTN_WIDTH // LANES
    for g, (_, dil) in enumerate(DIL_PATTERNS):
        for r in range(dil):
            rs = slice(None) if dil == 1 else pl.ds(r, rows // dil, stride=dil)
            for c in range(tiles):
                cs = slice(c * LANES, (c + 1) * LANES)
                o_s[g * tiles + c, rs, :] = o_refs[g][0, r, :, cs].astype(F32)
                l_s[g * tiles + c, rs, :] = l_refs[g][0, r, :, cs]
    merged = []
    for c in range(tiles):
        lses = [l_s[g * tiles + c] for g in range(N_GROUPS)]
        mx = functools.reduce(jnp.maximum, lses)
        num = jnp.zeros_like(mx)
        den = jnp.zeros_like(mx)
        for g in range(N_GROUPS):
            e = jnp.exp(lses[g] - mx)
            num = num + e * o_s[g * tiles + c]
            den = den + e
        merged.append((num / den).astype(BF16))
    out_ref[...] = x_ref[...] + _dot(jnp.concatenate(merged, axis=1), wo_ref[...])


def _attn_merge(outs, x, w_o):
    bsz, seq, d = x.shape
    rows = MERGE_ROWS
    args, in_specs = [], []
    for which in range(2):
        for g, (_, dil) in enumerate(DIL_PATTERNS):
            args.append(outs[g][which])
            in_specs.append(pl.BlockSpec((1, dil, rows // dil, ATTN_WIDTH), lambda b, i: (b, 0, i, 0)))
    args += [x, w_o]
    in_specs += [pl.BlockSpec((None, rows, d), lambda b, i: (b, i, 0)), _const_spec(w_o.shape)]
    return pl.pallas_call(
        _merge_body,
        out_shape=jax.ShapeDtypeStruct((bsz, seq, d), F32),
        grid=(bsz, seq // rows),
        in_specs=in_specs,
        out_specs=pl.BlockSpec((None, rows, d), lambda b, i: (b, i, 0)),
        scratch_shapes=[pltpu.VMEM((N_GROUPS * ATTN_WIDTH // LANES, rows, LANES), F32),
                        pltpu.VMEM((N_GROUPS * ATTN_WIDTH // LANES, rows, LANES), F32)],
        compiler_params=_params("parallel", "parallel"),
        name="attn_merge",
    )(*args)


def _router_body(x_ref, g_ref, wr_ref, br_ref, hn_ref, aff_ref):
    hn = _rms(x_ref[...], g_ref[...])
    hn_ref[...] = hn.astype(BF16)
    logits = jnp.dot(hn, wr_ref[...], preferred_element_type=F32, precision=lax.Precision.HIGHEST) + br_ref[...]
    m = jnp.max(logits, axis=-1, keepdims=True)
    e = jnp.exp(logits - m)
    aff_ref[...] = e / jnp.sum(e, axis=-1, keepdims=True)


def _router(x2, gain, w_r, b_r):
    n, d = x2.shape
    rows = ROUTER_ROWS
    return pl.pallas_call(
        _router_body,
        out_shape=[jax.ShapeDtypeStruct((n, d), BF16), jax.ShapeDtypeStruct((n, LANES), F32)],
        grid=(n // rows,),
        in_specs=[pl.BlockSpec((rows, d), lambda i: (i, 0)), _const_spec(gain.shape),
                  _const_spec(w_r.shape), _const_spec(b_r.shape)],
        out_specs=[pl.BlockSpec((rows, d), lambda i: (i, 0)), pl.BlockSpec((rows, LANES), lambda i: (i, 0))],
        compiler_params=_params("parallel"),
        name="moe_router",
    )(x2, gain, w_r, b_r)


def _select_body(cap, aff_ref, tri_ref, post_ref, gatet_ref, pos_ref, lo_ref):
    seq = aff_ref.shape[1]
    aff_t = aff_ref[0].T[:N_EXPERTS]
    key = pltpu.bitcast(aff_t, I32)
    thr = jnp.zeros((N_EXPERTS, 1), I32)
    for bit in range(30, -1, -1):
        cand = thr | (1 << bit)
        cnt = jnp.sum(jnp.where(key >= cand, 1.0, 0.0), axis=1, keepdims=True)
        thr = jnp.where(cnt >= cap, cand, thr)
    gt = key > thr
    eq = key == thr
    need = cap - jnp.sum(jnp.where(gt, 1.0, 0.0), axis=1, keepdims=True)
    tri = tri_ref[...]
    nblk = seq // TOKEN_BLOCK

    def prefix(mask_f, want_offsets):
        off = jnp.zeros((N_EXPERTS, 1), F32)
        parts, offs = [], []
        for jb in range(nblk):
            blk = mask_f[:, jb * TOKEN_BLOCK:(jb + 1) * TOKEN_BLOCK]
            offs.append(off)
            parts.append(_dot(blk.astype(BF16), tri) + off)
            off = off + jnp.sum(blk, axis=1, keepdims=True)
        offs.append(off)
        return jnp.concatenate(parts, axis=1), (offs if want_offsets else None)

    eq_rank, _ = prefix(jnp.where(eq, 1.0, 0.0), False)
    sel = gt | (eq & (eq_rank < need))
    rank, offs = prefix(jnp.where(sel, 1.0, 0.0), True)
    post = jnp.where(sel, rank, -1.0)
    post_ref[0] = post.astype(I32)
    gatet_ref[0] = jnp.where(sel, aff_t, 0.0)
    padded = jnp.concatenate([post, jnp.full((LANES - N_EXPERTS, seq), -1.0, F32)], axis=0)
    pos_ref[0] = padded.T.astype(I32)
    lane = lax.broadcasted_iota(I32, (N_EXPERTS, LANES), 1)
    lo = jnp.zeros((N_EXPERTS, LANES), F32)
    for jb, off in enumerate(offs):
        lo = jnp.where(lane == jb, off, lo)
    lo_ref[0] = lo.astype(I32)


def _select(aff, tri, cap):
    bsz, seq, _ = aff.shape
    return pl.pallas_call(
        functools.partial(_select_body, cap),
        out_shape=[jax.ShapeDtypeStruct((bsz, N_EXPERTS, seq), I32),
                   jax.ShapeDtypeStruct((bsz, N_EXPERTS, seq), F32),
                   jax.ShapeDtypeStruct((bsz, seq, LANES), I32),
                   jax.ShapeDtypeStruct((bsz, N_EXPERTS, LANES), I32)],
        grid=(bsz,),
        in_specs=[pl.BlockSpec((1, seq, LANES), lambda b: (b, 0, 0)), _const_spec(tri.shape)],
        out_specs=[pl.BlockSpec((1, N_EXPERTS, seq), lambda b: (b, 0, 0)),
                   pl.BlockSpec((1, N_EXPERTS, seq), lambda b: (b, 0, 0)),
                   pl.BlockSpec((1, seq, LANES), lambda b: (b, 0, 0)),
                   pl.BlockSpec((1, N_EXPERTS, LANES), lambda b: (b, 0, 0))],
        compiler_params=_params("parallel"),
        name="moe_select",
    )(aff, tri)


def _gather_body(cap, lo_ref, hn_ref, post_ref, gatet_ref, xin_ref, gate_ref, xin_s, gate_s):
    b = pl.program_id(0)
    e = pl.program_id(1)
    base = (b * N_EXPERTS + e) * LO_STRIDE
    seq = hn_ref.shape[1]
    xin_s[...] = jnp.zeros_like(xin_s)
    gate_s[...] = jnp.zeros_like(gate_s)
    slot_iota = lax.broadcasted_iota(I32, (GATHER_WINDOW, TOKEN_BLOCK), 0)
    for j in range(seq // TOKEN_BLOCK):
        ts = slice(j * TOKEN_BLOCK, (j + 1) * TOKEN_BLOCK)
        start = _floor_to(lo_ref[base + j], 8)
        nwin = _ceil_div(lo_ref[base + j + 1] - start, GATHER_WINDOW)
        pos_row = post_ref[0, 0, :, ts]
        gate_row = gatet_ref[0, 0, :, ts]

        def window(w, carry, ts=ts, start=start, pos_row=pos_row, gate_row=gate_row):
            s0 = pl.multiple_of(start + w * GATHER_WINDOW, 8)
            hit = pos_row == (slot_iota + s0)
            onehot = jnp.where(hit, 1.0, 0.0).astype(BF16)
            xin_s[pl.ds(s0, GATHER_WINDOW), :] += _dot(onehot, hn_ref[0, ts, :])
            gsum = jnp.sum(jnp.where(hit, gate_row, 0.0), axis=1, keepdims=True)
            gate_s[pl.ds(s0, GATHER_WINDOW), :] += jnp.broadcast_to(gsum, (GATHER_WINDOW, LANES))
            return carry

        window(0, 0)
        lax.fori_loop(1, nwin, window, 0)
    xin_ref[0, 0] = xin_s[:cap, :].astype(BF16)
    gate_ref[0, 0] = gate_s[:cap, :]


def _gather(lo_flat, hn, post, gatet, cap):
    bsz, seq, d = hn.shape
    pad_rows = cap + GATHER_WINDOW
    post4 = post.reshape(bsz, N_EXPERTS, 1, seq)
    gatet4 = gatet.reshape(bsz, N_EXPERTS, 1, seq)
    grid_spec = pltpu.PrefetchScalarGridSpec(
        num_scalar_prefetch=1,
        grid=(bsz, N_EXPERTS),
        in_specs=[pl.BlockSpec((1, seq, d), lambda b, e, lo: (b, 0, 0)),
                  pl.BlockSpec((1, 1, 1, seq), lambda b, e, lo: (b, e, 0, 0)),
                  pl.BlockSpec((1, 1, 1, seq), lambda b, e, lo: (b, e, 0, 0))],
        out_specs=[pl.BlockSpec((1, 1, cap, d), lambda b, e, lo: (b, e, 0, 0)),
                   pl.BlockSpec((1, 1, cap, LANES), lambda b, e, lo: (b, e, 0, 0))],
        scratch_shapes=[pltpu.VMEM((pad_rows, d), F32), pltpu.VMEM((pad_rows, LANES), F32)],
    )
    return pl.pallas_call(
        functools.partial(_gather_body, cap),
        out_shape=[jax.ShapeDtypeStruct((bsz, N_EXPERTS, cap, d), BF16),
                   jax.ShapeDtypeStruct((bsz, N_EXPERTS, cap, LANES), F32)],
        grid_spec=grid_spec,
        compiler_params=_params("parallel", "arbitrary"),
        name="moe_gather",
    )(lo_flat, hn, post4, gatet4)


def _expert_body(xin_ref, gate_ref, w1_ref, w3_ref, w2_ref, y_ref):
    xin = xin_ref[0, 0]
    f = w1_ref.shape[3]
    chunk = 2 * MXU_DIM
    y = jnp.zeros(y_ref.shape[2:], F32)
    for c in range(f // chunk):
        fs = slice(c * chunk, (c + 1) * chunk)
        hid = jax.nn.silu(_dot(xin, w1_ref[0, 0, :, fs])) * _dot(xin, w3_ref[0, 0, :, fs])
        y = y + _dot(hid.astype(BF16), w2_ref[0, 0, fs, :])
    y_ref[0, 0] = (y * gate_ref[0, 0, :, 0:1]).astype(BF16)


def _experts(layer, xin, gate, w1, w3, w2):
    bsz, ne, cap, d = xin.shape
    f = w1.shape[3]
    return pl.pallas_call(
        _expert_body,
        out_shape=jax.ShapeDtypeStruct((bsz, ne, cap, d), BF16),
        grid=(ne, bsz),
        in_specs=[pl.BlockSpec((1, 1, cap, d), lambda e, b: (b, e, 0, 0)),
                  pl.BlockSpec((1, 1, cap, LANES), lambda e, b: (b, e, 0, 0)),
                  pl.BlockSpec((1, 1, d, f), lambda e, b: (layer, e, 0, 0)),
                  pl.BlockSpec((1, 1, d, f), lambda e, b: (layer, e, 0, 0)),
                  pl.BlockSpec((1, 1, f, d), lambda e, b: (layer, e, 0, 0))],
        out_specs=pl.BlockSpec((1, 1, cap, d), lambda e, b: (b, e, 0, 0)),
        compiler_params=_params("parallel", "arbitrary"),
        name="moe_experts",
    )(xin, gate, w1, w3, w2)


def _combine_body(lo_ref, x_ref, y_ref, pos_ref, o_ref, acc_s):
    b = pl.program_id(0)
    j = pl.program_id(1)
    cap = y_ref.shape[2]
    pos = pos_ref[0]
    lane = lax.broadcasted_iota(I32, (TOKEN_BLOCK, SCATTER_WINDOW), 1)

    def contribution(e, pcol, start, w):
        s0 = start + w * SCATTER_WINDOW
        s0c = pl.multiple_of(jnp.minimum(s0, cap - SCATTER_WINDOW), 16)
        hit = ((pcol - s0c) == lane) & (pcol >= s0)
        onehot = jnp.where(hit, 1.0, 0.0).astype(BF16)
        return _dot(onehot, y_ref[0, e, pl.ds(s0c, SCATTER_WINDOW), :])

    acc = x_ref[0]
    extra = []
    for e in range(N_EXPERTS):
        base = (b * N_EXPERTS + e) * LO_STRIDE
        start = _floor_to(lo_ref[base + j], 16)
        nwin = _ceil_div(lo_ref[base + j + 1] - start, SCATTER_WINDOW)
        pcol = pos[:, e:e + 1]
        acc = acc + contribution(e, pcol, start, 0)
        extra.append((e, pcol, start, nwin))
    acc_s[...] = acc
    for e, pcol, start, nwin in extra:
        def window(w, carry, e=e, pcol=pcol, start=start):
            acc_s[...] += contribution(e, pcol, start, w)
            return carry

        lax.fori_loop(1, nwin, window, 0)
    o_ref[0] = acc_s[...]


def _combine(lo_flat, x, y, pos):
    bsz, seq, d = x.shape
    _, ne, cap, _ = y.shape
    grid_spec = pltpu.PrefetchScalarGridSpec(
        num_scalar_prefetch=1,
        grid=(bsz, seq // TOKEN_BLOCK),
        in_specs=[pl.BlockSpec((1, TOKEN_BLOCK, d), lambda b, j, lo: (b, j, 0)),
                  pl.BlockSpec((1, ne, cap, d), lambda b, j, lo: (b, 0, 0, 0)),
                  pl.BlockSpec((1, TOKEN_BLOCK, LANES), lambda b, j, lo: (b, j, 0))],
        out_specs=pl.BlockSpec((1, TOKEN_BLOCK, d), lambda b, j, lo: (b, j, 0)),
        scratch_shapes=[pltpu.VMEM((TOKEN_BLOCK, d), F32)],
    )
    return pl.pallas_call(
        _combine_body,
        out_shape=jax.ShapeDtypeStruct((bsz, seq, d), F32),
        grid_spec=grid_spec,
        compiler_params=_params("parallel", "arbitrary"),
        name="moe_combine",
    )(lo_flat, x, y, pos)


def _ec_moe(layer, x, gain, w_r, b_r, w1, w3, w2, tri):
    bsz, seq, d = x.shape
    cap = CAPACITY_FACTOR * seq // N_EXPERTS
    hn, aff = _router(x.reshape(bsz * seq, d), gain, w_r, b_r)
    hn = hn.reshape(bsz, seq, d)
    post, gatet, pos, lo = _select(aff.reshape(bsz, seq, LANES), tri, cap)
    lo_flat = lo[:, :, :LO_STRIDE].reshape(-1)
    xin, gate = _gather(lo_flat, hn, post, gatet, cap)
    y = _experts(layer, xin, gate, w1, w3, w2)
    return _combine(lo_flat, x, y, pos)


def _group_major(a):
    lead = a.shape[:-1]
    a = a.reshape(lead + (3, N_GROUPS, ATTN_WIDTH))
    return jnp.swapaxes(a, -3, -2).reshape(lead + (3 * N_GROUPS * ATTN_WIDTH,))


def kernel(x, mix_norm, ffn_norm, gm_w_in, gm_b_in, gm_v_norm, gm_w_s, gm_b_s, gm_w_out, gm_b_out,
           at_w_qkv, at_q_norm, at_k_norm, at_w_o, moe_w_router, moe_b_router, moe_w1, moe_w3, moe_w2):
    bsz, seq, d = x.shape
    depth = mix_norm.shape[0]
    n_mixers = 2

    lane_head = jnp.arange(MXU_DIM) // HEAD_DIM
    head_ones = (lane_head[:, None] == lane_head[None, :]).astype(BF16)
    tri = (jnp.arange(TOKEN_BLOCK)[:, None] < jnp.arange(TOKEN_BLOCK)[None, :]).astype(BF16)
    biases = [_attn_bias(dil) for _, dil in DIL_PATTERNS]
    pad_e = LANES - N_EXPERTS
    w1, w3, w2 = moe_w1.astype(BF16), moe_w3.astype(BF16), moe_w2.astype(BF16)

    for i in range(depth):
        j = i // n_mixers
        gain = mix_norm[i][None, :]
        if i % n_mixers == 0:
            x = _gmlp_layer(
                x.reshape(bsz * seq, d), gain, gm_w_in[j].astype(BF16), gm_b_in[j][None, :],
                gm_v_norm[j][None, :], gm_w_s[j].astype(BF16), gm_b_s[j].T, gm_w_out[j].astype(BF16),
                gm_b_out[j][None, :]).reshape(bsz, seq, d)
        else:
            def per_head(g):
                return jnp.tile(g[:, None, :], (1, HEADS, 1)).reshape(N_GROUPS, ATTN_WIDTH)
            qk_gain = jnp.stack([per_head(at_q_norm[j]) * (HEAD_DIM ** -0.5), per_head(at_k_norm[j]),
                                 jnp.ones((N_GROUPS, ATTN_WIDTH), F32)]).reshape(1, -1)
            qkvs = _qkv_proj(x, gain, _group_major(at_w_qkv[j]).astype(BF16), _group_major(qk_gain), head_ones)
            outs = [_attn_group(g, qkvs[g], biases[g]) for g in range(N_GROUPS)]
            x = _attn_merge(outs, x, at_w_o[j].astype(BF16))
        w_r = jnp.pad(moe_w_router[i], ((0, 0), (0, pad_e)))
        b_r = jnp.pad(moe_b_router[i], (0, pad_e), constant_values=NEG_INF)[None, :]
        x = _ec_moe(i, x, ffn_norm[i][None, :], w_r, b_r, w1, w3, w2, tri)
    return x
```

```python
import functools

import jax
import jax.numpy as jnp
from jax import lax
from jax.experimental import pallas as pl
from jax.experimental.pallas import tpu as pltpu

F32 = jnp.float32
BF16 = jnp.bfloat16
I32 = jnp.int32

EPS = 1e-6
NEG_INF = -1e30

LANES = 128
MXU_DIM = 256
VMEM_LIMIT_BYTES = 56 * 1024 * 1024

GMLP_CHUNK = 128
GMLP_GROUPS = 8
DIL_PATTERNS = ((128, 1), (512, 4), (2048, 16))
N_GROUPS = len(DIL_PATTERNS)
HEADS = 8
HEAD_DIM = 64
ATTN_WIDTH = HEADS * HEAD_DIM
GROUP_WIDTH = 3 * ATTN_WIDTH
QUERY_BLOCK = 128
HALF_WINDOW = 64
N_EXPERTS = 16
CAPACITY_FACTOR = 2

GMLP_ROWS = 256
QKV_ROWS = 512
MERGE_ROWS = 256
ROUTER_ROWS = 512
TOKEN_BLOCK = MXU_DIM
PACK = 4
FIRST_WINDOW = MXU_DIM // PACK
GATHER_WINDOW = 128
SCATTER_WINDOW = MXU_DIM
LO_STRIDE = 32


def _params(*semantics):
    return pltpu.CompilerParams(dimension_semantics=semantics, vmem_limit_bytes=VMEM_LIMIT_BYTES)


def _const_spec(shape):
    zeros = (0,) * len(shape)
    return pl.BlockSpec(shape, lambda *_: zeros)


def _rms(x, gain):
    return x * lax.rsqrt(jnp.mean(x * x, axis=-1, keepdims=True) + EPS) * gain


def _dot(a, b):
    return jnp.dot(a, b, preferred_element_type=F32)


def _floor_to(x, multiple):
    shift = multiple.bit_length() - 1
    return lax.shift_left(lax.shift_right_logical(x, shift), shift)


def _ceil_div(x, divisor):
    return lax.shift_right_logical(x + (divisor - 1), divisor.bit_length() - 1)


def _gmlp_body(x_ref, g_ref, win_ref, bin_ref, vg_ref, ws_ref, bst_ref, wout_ref, bout_ref, o_ref,
               v_s, uv_s):
    x = x_ref[...]
    rows, width = v_s.shape
    gw = width // GMLP_GROUPS
    h = _rms(x, g_ref[...]).astype(BF16)
    zv = jax.nn.gelu(_dot(h, win_ref[:, width:]) + bin_ref[:, width:])
    r = lax.rsqrt(jnp.mean(zv * zv, axis=-1, keepdims=True) + EPS)
    v_s[...] = (zv * r * vg_ref[...]).astype(BF16)
    pair = 2 * gw
    for p in range(GMLP_GROUPS // 2):
        zu = jax.nn.gelu(_dot(h, win_ref[:, p * pair:(p + 1) * pair]) + bin_ref[:, p * pair:(p + 1) * pair])
        for gl in range(2):
            g = 2 * p + gl
            cols = slice(g * gw, (g + 1) * gw)
            bias = bst_ref[:, g:g + 1]
            for c in range(rows // GMLP_CHUNK):
                rs = slice(c * GMLP_CHUNK, (c + 1) * GMLP_CHUNK)
                sv = _dot(ws_ref[g], v_s[rs, cols]) + bias
                uv_s[rs, cols] = (zu[rs, gl * gw:(gl + 1) * gw] * sv).astype(BF16)
    o_ref[...] = x + _dot(uv_s[...], wout_ref[...]) + bout_ref[...]


def _gmlp_layer(x2, gain, w_in, b_in, v_gain, w_s, b_st, w_out, b_out):
    n, d = x2.shape
    width = w_out.shape[0]
    rows = GMLP_ROWS
    return pl.pallas_call(
        _gmlp_body,
        out_shape=jax.ShapeDtypeStruct((n, d), F32),
        grid=(n // rows,),
        in_specs=[
            pl.BlockSpec((rows, d), lambda i: (i, 0)),
            _const_spec(gain.shape), _const_spec(w_in.shape), _const_spec(b_in.shape),
            _const_spec(v_gain.shape), _const_spec(w_s.shape), _const_spec(b_st.shape),
            _const_spec(w_out.shape), _const_spec(b_out.shape),
        ],
        out_specs=pl.BlockSpec((rows, d), lambda i: (i, 0)),
        scratch_shapes=[pltpu.VMEM((rows, width), BF16), pltpu.VMEM((rows, width), BF16)],
        compiler_params=_params("parallel"),
        name="gmlp_layer",
    )(x2, gain, w_in, b_in, v_gain, w_s, b_st, w_out, b_out)


def _qkv_body(x_ref, g_ref, w_ref, gain_ref, bd_ref, *rest):
    out_refs, t_s = rest[:N_GROUPS], rest[N_GROUPS]
    rows = t_s.shape[1]
    h = _rms(x_ref[...], g_ref[...]).astype(BF16)
    bd = bd_ref[...]

    def put(col0, t):
        for c in range(t.shape[1] // LANES):
            t_s[col0 // LANES + c] = t[:, c * LANES:(c + 1) * LANES]

    for g in range(N_GROUPS):
        tg = _dot(h, w_ref[:, g * GROUP_WIDTH:(g + 1) * GROUP_WIDTH])
        for c in range(2 * ATTN_WIDTH // MXU_DIM):
            col0 = g * GROUP_WIDTH + c * MXU_DIM
            t = tg[:, c * MXU_DIM:(c + 1) * MXU_DIM]
            t2 = t * t
            hi = t2.astype(BF16)
            lo = (t2 - hi.astype(F32)).astype(BF16)
            ss = _dot(hi, bd) + _dot(lo, bd)
            put(col0, t * lax.rsqrt(ss * (1.0 / HEAD_DIM) + EPS) * gain_ref[:, col0:col0 + MXU_DIM])
        put(g * GROUP_WIDTH + 2 * ATTN_WIDTH, tg[:, 2 * ATTN_WIDTH:])
    for g, (_, dil) in enumerate(DIL_PATTERNS):
        for r in range(dil):
            rs = slice(None) if dil == 1 else pl.ds(r, rows // dil, stride=dil)
            for c in range(GROUP_WIDTH // LANES):
                out_refs[g][0, r, :, c * LANES:(c + 1) * LANES] = (
                    t_s[g * (GROUP_WIDTH // LANES) + c, rs, :].astype(BF16))


def _qkv_proj(x, gain, w_qkv, qk_gain, head_ones):
    bsz, seq, d = x.shape
    width = w_qkv.shape[1]
    rows = QKV_ROWS
    out_shape, out_specs = [], []
    for _, dil in DIL_PATTERNS:
        out_shape.append(jax.ShapeDtypeStruct((bsz, dil, seq // dil, GROUP_WIDTH), BF16))
        out_specs.append(pl.BlockSpec((1, dil, rows // dil, GROUP_WIDTH), lambda b, i: (b, 0, i, 0)))
    return pl.pallas_call(
        _qkv_body,
        out_shape=out_shape,
        grid=(bsz, seq // rows),
        in_specs=[
            pl.BlockSpec((None, rows, d), lambda b, i: (b, i, 0)),
            _const_spec(gain.shape), _const_spec(w_qkv.shape), _const_spec(qk_gain.shape),
            _const_spec(head_ones.shape),
        ],
        out_specs=out_specs,
        scratch_shapes=[pltpu.VMEM((width // LANES, rows, LANES), F32)],
        compiler_params=_params("parallel", "parallel"),
        name="qkv_proj",
    )(x, gain, w_qkv, qk_gain, head_ones)


def _attn_body(q_ref, kp_ref, kc_ref, kn_ref, vp_ref, vc_ref, vn_ref, bias_ref, o_ref, l_ref):
    hw = HALF_WINDOW
    q = q_ref[0, 0]
    k = jnp.concatenate([kp_ref[0, 0, QUERY_BLOCK - hw:, :], kc_ref[0, 0], kn_ref[0, 0, :hw, :]], axis=0)
    v = jnp.concatenate([vp_ref[0, 0, QUERY_BLOCK - hw:, :], vc_ref[0, 0], vn_ref[0, 0, :hw, :]], axis=0)
    heads_per_tile = MXU_DIM // HEAD_DIM
    head_of_lane = lax.broadcasted_iota(I32, (QUERY_BLOCK, MXU_DIM), 1) // HEAD_DIM
    for half in range(ATTN_WIDTH // MXU_DIM):
        cs = slice(half * MXU_DIM, (half + 1) * MXU_DIM)
        qh, kh, vh = q[:, cs], k[:, cs], v[:, cs]
        qs = jnp.concatenate(
            [jnp.where(head_of_lane == hh, qh, jnp.zeros_like(qh)) for hh in range(heads_per_tile)], axis=0)
        s = lax.dot_general(qs, kh, (((1,), (1,)), ((), ())), preferred_element_type=F32)
        s = s + bias_ref[0, half]
        m = jnp.max(s, axis=-1, keepdims=True)
        p = jnp.exp(s - m)
        l = jnp.sum(p, axis=-1, keepdims=True)
        pv = _dot(p.astype(BF16), vh) / l
        lse = m + jnp.log(l)
        o_half = jnp.zeros((QUERY_BLOCK, MXU_DIM), F32)
        l_half = jnp.zeros((QUERY_BLOCK, MXU_DIM), F32)
        for hh in range(heads_per_tile):
            rs = slice(hh * QUERY_BLOCK, (hh + 1) * QUERY_BLOCK)
            sel = head_of_lane == hh
            o_half = jnp.where(sel, pv[rs], o_half)
            l_half = jnp.where(sel, lse[rs], l_half)
        o_ref[0, 0, :, cs] = o_half.astype(BF16)
        l_ref[0, 0, :, cs] = l_half


def _attn_group(g, qkv_g, bias):
    bsz, dil, sub_len, _ = qkv_g.shape
    nb = sub_len // QUERY_BLOCK
    qb = QUERY_BLOCK

    def spec(which, shift):
        def index(b, r, i):
            return (b, r, jnp.clip(i + shift, 0, nb - 1), which)
        return pl.BlockSpec((1, 1, qb, ATTN_WIDTH), index)

    def bias_index(b, r, i):
        return (jnp.where(i == 0, 0, jnp.where(i == nb - 1, 2, 1)), 0, 0, 0)

    out_spec = pl.BlockSpec((1, 1, qb, ATTN_WIDTH), lambda b, r, i: (b, r, i, 0))
    return pl.pallas_call(
        _attn_body,
        out_shape=[jax.ShapeDtypeStruct((bsz, dil, sub_len, ATTN_WIDTH), BF16),
                   jax.ShapeDtypeStruct((bsz, dil, sub_len, ATTN_WIDTH), F32)],
        grid=(bsz, dil, nb),
        in_specs=[spec(0, 0), spec(1, -1), spec(1, 0), spec(1, 1), spec(2, -1), spec(2, 0), spec(2, 1),
                  pl.BlockSpec((1,) + bias.shape[1:], bias_index)],
        out_specs=[out_spec, out_spec],
        compiler_params=_params("parallel", "parallel", "arbitrary"),
        name=f"dilated_attn_g{g}",
    )(*([qkv_g] * 7), bias)


def _attn_bias(dil):
    slopes = jnp.exp2(-8.0 * jnp.arange(1, HEADS + 1, dtype=F32) / HEADS)
    kb = QUERY_BLOCK + 2 * HALF_WINDOW
    rel = jnp.arange(kb)[None, :] - HALF_WINDOW - jnp.arange(QUERY_BLOCK)[:, None]
    band = jnp.abs(rel) <= HALF_WINDOW
    alibi = -slopes[:, None, None] * (dil * jnp.abs(rel)).astype(F32)[None]
    j = jnp.arange(kb)
    edge = jnp.stack([j >= HALF_WINDOW, j >= 0, j < kb - HALF_WINDOW])
    ok = band[None, None] & edge[:, None, None, :]
    tab = jnp.where(ok, alibi[None], NEG_INF)
    per_tile = MXU_DIM // HEAD_DIM
    return tab.reshape(3, HEADS // per_tile, per_tile * QUERY_BLOCK, kb)


def _merge_body(*refs):
    o_refs, l_refs = refs[:N_GROUPS], refs[N_GROUPS:2 * N_GROUPS]
    x_ref, wo_ref, out_ref, o_s, l_s = refs[2 * N_GROUPS:]
    rows = x_ref.shape[0]
    tiles = ATTN_WIDTH // LANES
    for g, (_, dil) in enumerate(DIL_PATTERNS):
        for r in range(dil):
            rs = slice(None) if dil == 1 else pl.ds(r, rows // dil, stride=dil)
            for c in range(tiles):
                cs = slice(c * LANES, (c + 1) * LANES)
                o_s[g * tiles + c, rs, :] = o_refs[g][0, r, :, cs].astype(F32)
                l_s[g * tiles + c, rs, :] = l_refs[g][0, r, :, cs]
    merged = []
    for c in range(tiles):
        lses = [l_s[g * tiles + c] for g in range(N_GROUPS)]
        mx = functools.reduce(jnp.maximum, lses)
        num = jnp.zeros_like(mx)
        den = jnp.zeros_like(mx)
        for g in range(N_GROUPS):
            e = jnp.exp(lses[g] - mx)
            num = num + e * o_s[g * tiles + c]
            den = den + e
        merged.append((num / den).astype(BF16))
    out_ref[...] = x_ref[...] + _dot(jnp.concatenate(merged, axis=1), wo_ref[...])


def _attn_merge(outs, x, w_o):
    bsz, seq, d = x.shape
    rows = MERGE_ROWS
    args, in_specs = [], []
    for which in range(2):
        for g, (_, dil) in enumerate(DIL_PATTERNS):
            args.append(outs[g][which])
            in_specs.append(pl.BlockSpec((1, dil, rows // dil, ATTN_WIDTH), lambda b, i: (b, 0, i, 0)))
    args += [x, w_o]
    in_specs += [pl.BlockSpec((None, rows, d), lambda b, i: (b, i, 0)), _const_spec(w_o.shape)]
    return pl.pallas_call(
        _merge_body,
        out_shape=jax.ShapeDtypeStruct((bsz, seq, d), F32),
        grid=(bsz, seq // rows),
        in_specs=in_specs,
        out_specs=pl.BlockSpec((None, rows, d), lambda b, i: (b, i, 0)),
        scratch_shapes=[pltpu.VMEM((N_GROUPS * ATTN_WIDTH // LANES, rows, LANES), F32),
                        pltpu.VMEM((N_GROUPS * ATTN_WIDTH // LANES, rows, LANES), F32)],
        compiler_params=_params("parallel", "parallel"),
        name="attn_merge",
    )(*args)


def _router_body(x_ref, g_ref, wr_ref, br_ref, hn_ref, aff_ref):
    hn = _rms(x_ref[...], g_ref[...])
    hn_ref[...] = hn.astype(BF16)
    logits = jnp.dot(hn, wr_ref[...], preferred_element_type=F32, precision=lax.Precision.HIGHEST) + br_ref[...]
    m = jnp.max(logits, axis=-1, keepdims=True)
    e = jnp.exp(logits - m)
    aff_ref[...] = e / jnp.sum(e, axis=-1, keepdims=True)


def _router(x2, gain, w_r, b_r):
    n, d = x2.shape
    rows = ROUTER_ROWS
    return pl.pallas_call(
        _router_body,
        out_shape=[jax.ShapeDtypeStruct((n, d), BF16), jax.ShapeDtypeStruct((n, LANES), F32)],
        grid=(n // rows,),
        in_specs=[pl.BlockSpec((rows, d), lambda i: (i, 0)), _const_spec(gain.shape),
                  _const_spec(w_r.shape), _const_spec(b_r.shape)],
        out_specs=[pl.BlockSpec((rows, d), lambda i: (i, 0)), pl.BlockSpec((rows, LANES), lambda i: (i, 0))],
        compiler_params=_params("parallel"),
        name="moe_router",
    )(x2, gain, w_r, b_r)


def _select_body(cap, aff_ref, tri_ref, post_ref, gatet_ref, pos_ref, lo_ref):
    seq = aff_ref.shape[1]
    aff_t = aff_ref[0].T[:N_EXPERTS]
    key = pltpu.bitcast(aff_t, I32)
    thr = jnp.zeros((N_EXPERTS, 1), I32)
    for bit in range(30, -1, -1):
        cand = thr | (1 << bit)
        cnt = jnp.sum(jnp.where(key >= cand, 1.0, 0.0), axis=1, keepdims=True)
        thr = jnp.where(cnt >= cap, cand, thr)
    gt = key > thr
    eq = key == thr
    need = cap - jnp.sum(jnp.where(gt, 1.0, 0.0), axis=1, keepdims=True)
    tri = tri_ref[...]
    nblk = seq // TOKEN_BLOCK

    def prefix(mask_f, want_offsets):
        off = jnp.zeros((N_EXPERTS, 1), F32)
        parts, offs = [], []
        for jb in range(nblk):
            blk = mask_f[:, jb * TOKEN_BLOCK:(jb + 1) * TOKEN_BLOCK]
            offs.append(off)
            parts.append(_dot(blk.astype(BF16), tri) + off)
            off = off + jnp.sum(blk, axis=1, keepdims=True)
        offs.append(off)
        return jnp.concatenate(parts, axis=1), (offs if want_offsets else None)

    eq_rank, _ = prefix(jnp.where(eq, 1.0, 0.0), False)
    sel = gt | (eq & (eq_rank < need))
    rank, offs = prefix(jnp.where(sel, 1.0, 0.0), True)
    post = jnp.where(sel, rank, -1.0)
    post_ref[0] = post.astype(I32)
    gatet_ref[0] = jnp.where(sel, aff_t, 0.0)
    padded = jnp.concatenate([post, jnp.full((LANES - N_EXPERTS, seq), -1.0, F32)], axis=0)
    pos_ref[0] = padded.T.astype(I32)
    lane = lax.broadcasted_iota(I32, (N_EXPERTS, LANES), 1)
    lo = jnp.zeros((N_EXPERTS, LANES), F32)
    for jb, off in enumerate(offs):
        lo = jnp.where(lane == jb, off, lo)
    lo_ref[0] = lo.astype(I32)


def _select(aff, tri, cap):
    bsz, seq, _ = aff.shape
    return pl.pallas_call(
        functools.partial(_select_body, cap),
        out_shape=[jax.ShapeDtypeStruct((bsz, N_EXPERTS, seq), I32),
                   jax.ShapeDtypeStruct((bsz, N_EXPERTS, seq), F32),
                   jax.ShapeDtypeStruct((bsz, seq, LANES), I32),
                   jax.ShapeDtypeStruct((bsz, N_EXPERTS, LANES), I32)],
        grid=(bsz,),
        in_specs=[pl.BlockSpec((1, seq, LANES), lambda b: (b, 0, 0)), _const_spec(tri.shape)],
        out_specs=[pl.BlockSpec((1, N_EXPERTS, seq), lambda b: (b, 0, 0)),
                   pl.BlockSpec((1, N_EXPERTS, seq), lambda b: (b, 0, 0)),
                   pl.BlockSpec((1, seq, LANES), lambda b: (b, 0, 0)),
                   pl.BlockSpec((1, N_EXPERTS, LANES), lambda b: (b, 0, 0))],
        compiler_params=_params("parallel"),
        name="moe_select",
    )(aff, tri)


def _gather_body(cap, lo_ref, hn_ref, post_ref, gatet_ref, xin_ref, gate_ref, xin_s, gate_s):
    b = pl.program_id(0)
    eg = pl.program_id(1)
    seq = hn_ref.shape[1]
    nblk = seq // TOKEN_BLOCK
    xin_s[...] = jnp.zeros_like(xin_s)
    gate_s[...] = jnp.zeros_like(gate_s)

    def table(i):
        return (b * N_EXPERTS + eg * PACK + i) * LO_STRIDE

    def accumulate(i, s0, rows, hit, gate_row, contrib):
        xin_s[i, pl.ds(s0, rows), :] += contrib
        gsum = jnp.sum(jnp.where(hit, gate_row, 0.0), axis=1, keepdims=True)
        gate_s[i, pl.ds(s0, rows), :] += jnp.broadcast_to(gsum, (rows, LANES))

    first_iota = lax.broadcasted_iota(I32, (FIRST_WINDOW, TOKEN_BLOCK), 0)
    for j in range(nblk):
        ts = slice(j * TOKEN_BLOCK, (j + 1) * TOKEN_BLOCK)
        starts, hits = [], []
        for i in range(PACK):
            start = pl.multiple_of(_floor_to(lo_ref[table(i) + j], 8), 8)
            starts.append(start)
            hits.append(post_ref[0, i, :, ts] == (first_iota + start))
        onehot = jnp.concatenate([jnp.where(h, 1.0, 0.0).astype(BF16) for h in hits], axis=0)
        contrib = _dot(onehot, hn_ref[0, ts, :])
        for i in range(PACK):
            accumulate(i, starts[i], FIRST_WINDOW, hits[i], gatet_ref[0, i, :, ts],
                       contrib[i * FIRST_WINDOW:(i + 1) * FIRST_WINDOW])

    more_iota = lax.broadcasted_iota(I32, (GATHER_WINDOW, TOKEN_BLOCK), 0)
    for j in range(nblk):
        ts = slice(j * TOKEN_BLOCK, (j + 1) * TOKEN_BLOCK)

        def expert(i, carry, j=j, ts=ts):
            done = _floor_to(lo_ref[table(i) + j], 8) + FIRST_WINDOW
            left = jnp.maximum(lo_ref[table(i) + j + 1] - done, 0)

            def window(w, carry):
                s0 = pl.multiple_of(done + w * GATHER_WINDOW, 8)
                hit = post_ref[0, i, :, ts] == (more_iota + s0)
                contrib = _dot(jnp.where(hit, 1.0, 0.0).astype(BF16), hn_ref[0, ts, :])
                accumulate(i, s0, GATHER_WINDOW, hit, gatet_ref[0, i, :, ts], contrib)
                return carry

            return lax.fori_loop(0, _ceil_div(left, GATHER_WINDOW), window, carry)

        lax.fori_loop(0, PACK, expert, 0)
    for i in range(PACK):
        xin_ref[0, i] = xin_s[i, :cap, :].astype(BF16)
        gate_ref[0, i] = gate_s[i, :cap, :]


def _gather(lo_flat, hn, post, gatet, cap):
    bsz, seq, d = hn.shape
    pad_rows = cap + GATHER_WINDOW
    post4 = post.reshape(bsz, N_EXPERTS, 1, seq)
    gatet4 = gatet.reshape(bsz, N_EXPERTS, 1, seq)
    grid_spec = pltpu.PrefetchScalarGridSpec(
        num_scalar_prefetch=1,
        grid=(bsz, N_EXPERTS // PACK),
        in_specs=[pl.BlockSpec((1, seq, d), lambda b, e, lo: (b, 0, 0)),
                  pl.BlockSpec((1, PACK, 1, seq), lambda b, e, lo: (b, e, 0, 0)),
                  pl.BlockSpec((1, PACK, 1, seq), lambda b, e, lo: (b, e, 0, 0))],
        out_specs=[pl.BlockSpec((1, PACK, cap, d), lambda b, e, lo: (b, e, 0, 0)),
                   pl.BlockSpec((1, PACK, cap, LANES), lambda b, e, lo: (b, e, 0, 0))],
        scratch_shapes=[pltpu.VMEM((PACK, pad_rows, d), F32), pltpu.VMEM((PACK, pad_rows, LANES), F32)],
    )
    return pl.pallas_call(
        functools.partial(_gather_body, cap),
        out_shape=[jax.ShapeDtypeStruct((bsz, N_EXPERTS, cap, d), BF16),
                   jax.ShapeDtypeStruct((bsz, N_EXPERTS, cap, LANES), F32)],
        grid_spec=grid_spec,
        compiler_params=_params("parallel", "arbitrary"),
        name="moe_gather",
    )(lo_flat, hn, post4, gatet4)


def _expert_body(xin_ref, gate_ref, w1_ref, w3_ref, w2_ref, y_ref):
    xin = xin_ref[0, 0]
    f = w1_ref.shape[3]
    chunk = 2 * MXU_DIM
    y = jnp.zeros(y_ref.shape[2:], F32)
    for c in range(f // chunk):
        fs = slice(c * chunk, (c + 1) * chunk)
        hid = jax.nn.silu(_dot(xin, w1_ref[0, 0, :, fs])) * _dot(xin, w3_ref[0, 0, :, fs])
        y = y + _dot(hid.astype(BF16), w2_ref[0, 0, fs, :])
    y_ref[0, 0] = (y * gate_ref[0, 0, :, 0:1]).astype(BF16)


def _experts(layer, xin, gate, w1, w3, w2):
    bsz, ne, cap, d = xin.shape
    f = w1.shape[3]
    return pl.pallas_call(
        _expert_body,
        out_shape=jax.ShapeDtypeStruct((bsz, ne, cap, d), BF16),
        grid=(ne, bsz),
        in_specs=[pl.BlockSpec((1, 1, cap, d), lambda e, b: (b, e, 0, 0)),
                  pl.BlockSpec((1, 1, cap, LANES), lambda e, b: (b, e, 0, 0)),
                  pl.BlockSpec((1, 1, d, f), lambda e, b: (layer, e, 0, 0)),
                  pl.BlockSpec((1, 1, d, f), lambda e, b: (layer, e, 0, 0)),
                  pl.BlockSpec((1, 1, f, d), lambda e, b: (layer, e, 0, 0))],
        out_specs=pl.BlockSpec((1, 1, cap, d), lambda e, b: (b, e, 0, 0)),
        compiler_params=_params("parallel", "arbitrary"),
        name="moe_experts",
    )(xin, gate, w1, w3, w2)


def _combine_body(lo_ref, x_ref, y_ref, pos_ref, o_ref, acc_s):
    b = pl.program_id(0)
    j = pl.program_id(1)
    cap = y_ref.shape[2]
    pos = pos_ref[0]

    def bounds(e):
        base = (b * N_EXPERTS + e) * LO_STRIDE
        return lo_ref[base + j], lo_ref[base + j + 1]

    lane = lax.broadcasted_iota(I32, (TOKEN_BLOCK, PACK * FIRST_WINDOW), 1)
    lane_expert = lane // FIRST_WINDOW
    lane_slot = lane % FIRST_WINDOW
    acc = x_ref[0]
    for q in range(N_EXPERTS // PACK):
        target = jnp.full(lane.shape, -1, I32)
        y_rows = []
        for i in range(PACK):
            e = q * PACK + i
            start = _floor_to(bounds(e)[0], 16)
            s0c = pl.multiple_of(jnp.minimum(start, cap - FIRST_WINDOW), 16)
            pcol = pos[:, e:e + 1]
            rel = jnp.where(pcol >= start, pcol - s0c, -1)
            target = jnp.where(lane_expert == i, rel, target)
            y_rows.append(y_ref[0, e, pl.ds(s0c, FIRST_WINDOW), :])
        onehot = jnp.where(target == lane_slot, 1.0, 0.0).astype(BF16)
        acc = acc + _dot(onehot, jnp.concatenate(y_rows, axis=0))
    acc_s[...] = acc

    more_lane = lax.broadcasted_iota(I32, (TOKEN_BLOCK, SCATTER_WINDOW), 1)
    for e in range(N_EXPERTS):
        lo, hi = bounds(e)
        done = _floor_to(lo, 16) + FIRST_WINDOW
        pcol = pos[:, e:e + 1]

        def window(w, carry, e=e, done=done, pcol=pcol):
            s0 = done + w * SCATTER_WINDOW
            s0c = pl.multiple_of(jnp.minimum(s0, cap - SCATTER_WINDOW), 16)
            hit = ((pcol - s0c) == more_lane) & (pcol >= s0)
            onehot = jnp.where(hit, 1.0, 0.0).astype(BF16)
            acc_s[...] += _dot(onehot, y_ref[0, e, pl.ds(s0c, SCATTER_WINDOW), :])
            return carry

        lax.fori_loop(0, _ceil_div(jnp.maximum(hi - done, 0), SCATTER_WINDOW), window, 0)
    o_ref[0] = acc_s[...]


def _combine(lo_flat, x, y, pos):
    bsz, seq, d = x.shape
    _, ne, cap, _ = y.shape
    grid_spec = pltpu.PrefetchScalarGridSpec(
        num_scalar_prefetch=1,
        grid=(bsz, seq // TOKEN_BLOCK),
        in_specs=[pl.BlockSpec((1, TOKEN_BLOCK, d), lambda b, j, lo: (b, j, 0)),
                  pl.BlockSpec((1, ne, cap, d), lambda b, j, lo: (b, 0, 0, 0)),
                  pl.BlockSpec((1, TOKEN_BLOCK, LANES), lambda b, j, lo: (b, j, 0))],
        out_specs=pl.BlockSpec((1, TOKEN_BLOCK, d), lambda b, j, lo: (b, j, 0)),
        scratch_shapes=[pltpu.VMEM((TOKEN_BLOCK, d), F32)],
    )
    return pl.pallas_call(
        _combine_body,
        out_shape=jax.ShapeDtypeStruct((bsz, seq, d), F32),
        grid_spec=grid_spec,
        compiler_params=_params("parallel", "arbitrary"),
        name="moe_combine",
    )(lo_flat, x, y, pos)


def _ec_moe(layer, x, gain, w_r, b_r, w1, w3, w2, tri):
    bsz, seq, d = x.shape
    cap = CAPACITY_FACTOR * seq // N_EXPERTS
    hn, aff = _router(x.reshape(bsz * seq, d), gain, w_r, b_r)
    hn = hn.reshape(bsz, seq, d)
    post, gatet, pos, lo = _select(aff.reshape(bsz, seq, LANES), tri, cap)
    lo_flat = lo[:, :, :LO_STRIDE].reshape(-1)
    xin, gate = _gather(lo_flat, hn, post, gatet, cap)
    y = _experts(layer, xin, gate, w1, w3, w2)
    return _combine(lo_flat, x, y, pos)


def _group_major(a):
    lead = a.shape[:-1]
    a = a.reshape(lead + (3, N_GROUPS, ATTN_WIDTH))
    return jnp.swapaxes(a, -3, -2).reshape(lead + (3 * N_GROUPS * ATTN_WIDTH,))


def kernel(x, mix_norm, ffn_norm, gm_w_in, gm_b_in, gm_v_norm, gm_w_s, gm_b_s, gm_w_out, gm_b_out,
           at_w_qkv, at_q_norm, at_k_norm, at_w_o, moe_w_router, moe_b_router, moe_w1, moe_w3, moe_w2):
    bsz, seq, d = x.shape
    depth = mix_norm.shape[0]
    n_mixers = 2

    lane_head = jnp.arange(MXU_DIM) // HEAD_DIM
    head_ones = (lane_head[:, None] == lane_head[None, :]).astype(BF16)
    tri = (jnp.arange(TOKEN_BLOCK)[:, None] < jnp.arange(TOKEN_BLOCK)[None, :]).astype(BF16)
    biases = [_attn_bias(dil) for _, dil in DIL_PATTERNS]
    pad_e = LANES - N_EXPERTS
    w1, w3, w2 = moe_w1.astype(BF16), moe_w3.astype(BF16), moe_w2.astype(BF16)

    for i in range(depth):
        j = i // n_mixers
        gain = mix_norm[i][None, :]
        if i % n_mixers == 0:
            x = _gmlp_layer(
                x.reshape(bsz * seq, d), gain, gm_w_in[j].astype(BF16), gm_b_in[j][None, :],
                gm_v_norm[j][None, :], gm_w_s[j].astype(BF16), gm_b_s[j].T, gm_w_out[j].astype(BF16),
                gm_b_out[j][None, :]).reshape(bsz, seq, d)
        else:
            def per_head(g):
                return jnp.tile(g[:, None, :], (1, HEADS, 1)).reshape(N_GROUPS, ATTN_WIDTH)
            qk_gain = jnp.stack([per_head(at_q_norm[j]) * (HEAD_DIM ** -0.5), per_head(at_k_norm[j]),
                                 jnp.ones((N_GROUPS, ATTN_WIDTH), F32)]).reshape(1, -1)
            qkvs = _qkv_proj(x, gain, _group_major(at_w_qkv[j]).astype(BF16), _group_major(qk_gain), head_ones)
            outs = [_attn_group(g, qkvs[g], biases[g]) for g in range(N_GROUPS)]
            x = _attn_merge(outs, x, at_w_o[j].astype(BF16))
        w_r = jnp.pad(moe_w_router[i], ((0, 0), (0, pad_e)))
        b_r = jnp.pad(moe_b_router[i], (0, pad_e), constant_values=NEG_INF)[None, :]
        x = _ec_moe(i, x, ffn_norm[i][None, :], w_r, b_r, w1, w3, w2, tri)
    return x
```

```python
import functools

import jax
import jax.numpy as jnp
from jax import lax
from jax.experimental import pallas as pl
from jax.experimental.pallas import tpu as pltpu

F32 = jnp.float32
BF16 = jnp.bfloat16
I32 = jnp.int32

EPS = 1e-6
NEG_INF = -1e30

LANES = 128
MXU_DIM = 256
VMEM_LIMIT_BYTES = 56 * 1024 * 1024

GMLP_CHUNK = 128
GMLP_GROUPS = 8
DIL_PATTERNS = ((128, 1), (512, 4), (2048, 16))
N_GROUPS = len(DIL_PATTERNS)
HEADS = 8
HEAD_DIM = 64
ATTN_WIDTH = HEADS * HEAD_DIM
GROUP_WIDTH = 3 * ATTN_WIDTH
QUERY_BLOCK = 128
HALF_WINDOW = 64
N_EXPERTS = 16
CAPACITY_FACTOR = 2

GMLP_ROWS = 256
QKV_ROWS = 512
MERGE_ROWS = 256
ATTN_ROWS = 2 * QUERY_BLOCK
ROUTER_ROWS = 512
TOKEN_BLOCK = MXU_DIM
PACK = 4
FIRST_WINDOW = MXU_DIM // PACK
GATHER_WINDOW = 128
SCATTER_WINDOW = MXU_DIM
LO_STRIDE = 32
EXPERT_CHUNK = 2 * MXU_DIM


def _params(*semantics):
    return pltpu.CompilerParams(dimension_semantics=semantics, vmem_limit_bytes=VMEM_LIMIT_BYTES)


def _const_spec(shape):
    zeros = (0,) * len(shape)
    return pl.BlockSpec(shape, lambda *_: zeros)


def _rms(x, gain):
    return x * lax.rsqrt(jnp.mean(x * x, axis=-1, keepdims=True) + EPS) * gain


def _dot(a, b):
    return jnp.dot(a, b, preferred_element_type=F32)


def _floor_to(x, multiple):
    shift = multiple.bit_length() - 1
    return lax.shift_left(lax.shift_right_logical(x, shift), shift)


def _ceil_div(x, divisor):
    return lax.shift_right_logical(x + (divisor - 1), divisor.bit_length() - 1)


def _gmlp_body(x_ref, g_ref, win_ref, bin_ref, vg_ref, ws_ref, bst_ref, wout_ref, bout_ref, o_ref,
               v_s, uv_s):
    x = x_ref[...]
    rows, width = v_s.shape
    gw = width // GMLP_GROUPS
    h = _rms(x, g_ref[...]).astype(BF16)
    zv = jax.nn.gelu(_dot(h, win_ref[:, width:]) + bin_ref[:, width:])
    r = lax.rsqrt(jnp.mean(zv * zv, axis=-1, keepdims=True) + EPS)
    v_s[...] = (zv * r * vg_ref[...]).astype(BF16)
    pair = 2 * gw
    for p in range(GMLP_GROUPS // 2):
        zu = jax.nn.gelu(_dot(h, win_ref[:, p * pair:(p + 1) * pair]) + bin_ref[:, p * pair:(p + 1) * pair])
        for gl in range(2):
            g = 2 * p + gl
            cols = slice(g * gw, (g + 1) * gw)
            bias = bst_ref[:, g:g + 1]
            for c in range(rows // GMLP_CHUNK):
                rs = slice(c * GMLP_CHUNK, (c + 1) * GMLP_CHUNK)
                sv = _dot(ws_ref[g], v_s[rs, cols]) + bias
                uv_s[rs, cols] = (zu[rs, gl * gw:(gl + 1) * gw] * sv).astype(BF16)
    o_ref[...] = x + _dot(uv_s[...], wout_ref[...]) + bout_ref[...]


def _gmlp_layer(x2, gain, w_in, b_in, v_gain, w_s, b_st, w_out, b_out):
    n, d = x2.shape
    width = w_out.shape[0]
    rows = GMLP_ROWS
    return pl.pallas_call(
        _gmlp_body,
        out_shape=jax.ShapeDtypeStruct((n, d), F32),
        grid=(n // rows,),
        in_specs=[
            pl.BlockSpec((rows, d), lambda i: (i, 0)),
            _const_spec(gain.shape), _const_spec(w_in.shape), _const_spec(b_in.shape),
            _const_spec(v_gain.shape), _const_spec(w_s.shape), _const_spec(b_st.shape),
            _const_spec(w_out.shape), _const_spec(b_out.shape),
        ],
        out_specs=pl.BlockSpec((rows, d), lambda i: (i, 0)),
        scratch_shapes=[pltpu.VMEM((rows, width), BF16), pltpu.VMEM((rows, width), BF16)],
        compiler_params=_params("parallel"),
        name="gmlp_layer",
    )(x2, gain, w_in, b_in, v_gain, w_s, b_st, w_out, b_out)


def _qkv_body(x_ref, g_ref, w_ref, gain_ref, bd_ref, *rest):
    out_refs, t_s = rest[:N_GROUPS], rest[N_GROUPS]
    rows = t_s.shape[1]
    h = _rms(x_ref[...], g_ref[...]).astype(BF16)
    bd = bd_ref[...]

    def put(col0, t):
        for c in range(t.shape[1] // LANES):
            t_s[col0 // LANES + c] = t[:, c * LANES:(c + 1) * LANES]

    for g in range(N_GROUPS):
        tg = _dot(h, w_ref[:, g * GROUP_WIDTH:(g + 1) * GROUP_WIDTH])
        for c in range(2 * ATTN_WIDTH // MXU_DIM):
            col0 = g * GROUP_WIDTH + c * MXU_DIM
            t = tg[:, c * MXU_DIM:(c + 1) * MXU_DIM]
            t2 = t * t
            hi = t2.astype(BF16)
            lo = (t2 - hi.astype(F32)).astype(BF16)
            ss = _dot(hi, bd) + _dot(lo, bd)
            put(col0, t * lax.rsqrt(ss * (1.0 / HEAD_DIM) + EPS) * gain_ref[:, col0:col0 + MXU_DIM])
        put(g * GROUP_WIDTH + 2 * ATTN_WIDTH, tg[:, 2 * ATTN_WIDTH:])
    for g, (_, dil) in enumerate(DIL_PATTERNS):
        for r in range(dil):
            rs = slice(None) if dil == 1 else pl.ds(r, rows // dil, stride=dil)
            for c in range(GROUP_WIDTH // LANES):
                out_refs[g][0, r, :, c * LANES:(c + 1) * LANES] = (
                    t_s[g * (GROUP_WIDTH // LANES) + c, rs, :].astype(BF16))


def _qkv_proj(x, gain, w_qkv, qk_gain, head_ones):
    bsz, seq, d = x.shape
    width = w_qkv.shape[1]
    rows = QKV_ROWS
    out_shape, out_specs = [], []
    for _, dil in DIL_PATTERNS:
        out_shape.append(jax.ShapeDtypeStruct((bsz, dil, seq // dil, GROUP_WIDTH), BF16))
        out_specs.append(pl.BlockSpec((1, dil, rows // dil, GROUP_WIDTH), lambda b, i: (b, 0, i, 0)))
    return pl.pallas_call(
        _qkv_body,
        out_shape=out_shape,
        grid=(bsz, seq // rows),
        in_specs=[
            pl.BlockSpec((None, rows, d), lambda b, i: (b, i, 0)),
            _const_spec(gain.shape), _const_spec(w_qkv.shape), _const_spec(qk_gain.shape),
            _const_spec(head_ones.shape),
        ],
        out_specs=out_specs,
        scratch_shapes=[pltpu.VMEM((width // LANES, rows, LANES), F32)],
        compiler_params=_params("parallel", "parallel"),
        name="qkv_proj",
    )(x, gain, w_qkv, qk_gain, head_ones)


def _attn_body(q_ref, kp_ref, kc_ref, kn_ref, vp_ref, vc_ref, vn_ref, bias_ref, o_ref, l_ref):
    hw = HALF_WINDOW
    qb = QUERY_BLOCK
    n_sub = q_ref.shape[2] // qb
    last_block = pl.num_programs(2) * n_sub - 1
    k_all = jnp.concatenate([kp_ref[0, 0, qb - hw:, :], kc_ref[0, 0], kn_ref[0, 0, :hw, :]], axis=0)
    v_all = jnp.concatenate([vp_ref[0, 0, qb - hw:, :], vc_ref[0, 0], vn_ref[0, 0, :hw, :]], axis=0)
    heads_per_tile = MXU_DIM // HEAD_DIM
    head_of_lane = lax.broadcasted_iota(I32, (qb, MXU_DIM), 1) // HEAD_DIM
    for sub in range(n_sub):
        block = pl.program_id(2) * n_sub + sub
        variant = jnp.where(block == 0, 0, jnp.where(block == last_block, 2, 1))
        q = q_ref[0, 0, sub * qb:(sub + 1) * qb, :]
        k = k_all[sub * qb:sub * qb + qb + 2 * hw]
        v = v_all[sub * qb:sub * qb + qb + 2 * hw]
        for half in range(ATTN_WIDTH // MXU_DIM):
            cs = slice(half * MXU_DIM, (half + 1) * MXU_DIM)
            qh, kh, vh = q[:, cs], k[:, cs], v[:, cs]
            qs = jnp.concatenate(
                [jnp.where(head_of_lane == hh, qh, jnp.zeros_like(qh)) for hh in range(heads_per_tile)], axis=0)
            s = lax.dot_general(qs, kh, (((1,), (1,)), ((), ())), preferred_element_type=F32)
            s = s + bias_ref[variant, half]
            m = jnp.max(s, axis=-1, keepdims=True)
            p = jnp.exp(s - m)
            l = jnp.sum(p, axis=-1, keepdims=True)
            pv = _dot(p.astype(BF16), vh) / l
            lse = m + jnp.log(l)
            o_half = jnp.zeros((qb, MXU_DIM), F32)
            l_half = jnp.zeros((qb, MXU_DIM), F32)
            for hh in range(heads_per_tile):
                rs = slice(hh * qb, (hh + 1) * qb)
                sel = head_of_lane == hh
                o_half = jnp.where(sel, pv[rs], o_half)
                l_half = jnp.where(sel, lse[rs], l_half)
            o_ref[0, 0, sub * qb:(sub + 1) * qb, cs] = o_half.astype(BF16)
            l_ref[0, 0, sub * qb:(sub + 1) * qb, cs] = l_half


def _attn_group(g, qkv_g, bias):
    bsz, dil, sub_len, _ = qkv_g.shape
    rows = ATTN_ROWS
    per_step = rows // QUERY_BLOCK
    nb = sub_len // QUERY_BLOCK

    def main(which):
        return pl.BlockSpec((1, 1, rows, ATTN_WIDTH), lambda b, r, i: (b, r, i, which))

    def halo(which, offset):
        def index(b, r, i):
            return (b, r, jnp.clip(i * per_step + offset, 0, nb - 1), which)
        return pl.BlockSpec((1, 1, QUERY_BLOCK, ATTN_WIDTH), index)

    return pl.pallas_call(
        _attn_body,
        out_shape=[jax.ShapeDtypeStruct((bsz, dil, sub_len, ATTN_WIDTH), BF16),
                   jax.ShapeDtypeStruct((bsz, dil, sub_len, ATTN_WIDTH), F32)],
        grid=(bsz, dil, sub_len // rows),
        in_specs=[main(0), halo(1, -1), main(1), halo(1, per_step), halo(2, -1), main(2), halo(2, per_step),
                  _const_spec(bias.shape)],
        out_specs=[main(0), main(0)],
        compiler_params=_params("parallel", "parallel", "arbitrary"),
        name=f"dilated_attn_g{g}",
    )(*([qkv_g] * 7), bias)


def _attn_bias(dil):
    slopes = jnp.exp2(-8.0 * jnp.arange(1, HEADS + 1, dtype=F32) / HEADS)
    kb = QUERY_BLOCK + 2 * HALF_WINDOW
    rel = jnp.arange(kb)[None, :] - HALF_WINDOW - jnp.arange(QUERY_BLOCK)[:, None]
    band = jnp.abs(rel) <= HALF_WINDOW
    alibi = -slopes[:, None, None] * (dil * jnp.abs(rel)).astype(F32)[None]
    j = jnp.arange(kb)
    edge = jnp.stack([j >= HALF_WINDOW, j >= 0, j < kb - HALF_WINDOW])
    ok = band[None, None] & edge[:, None, None, :]
    tab = jnp.where(ok, alibi[None], NEG_INF)
    per_tile = MXU_DIM // HEAD_DIM
    return tab.reshape(3, HEADS // per_tile, per_tile * QUERY_BLOCK, kb)


def _merge_body(*refs):
    o_refs, l_refs = refs[:N_GROUPS], refs[N_GROUPS:2 * N_GROUPS]
    x_ref, wo_ref, out_ref, o_s, l_s = refs[2 * N_GROUPS:]
    rows = x_ref.shape[0]
    tiles = ATTN_WIDTH // LANES
    for g, (_, dil) in enumerate(DIL_PATTERNS):
        for r in range(dil):
            rs = slice(None) if dil == 1 else pl.ds(r, rows // dil, stride=dil)
            for c in range(tiles):
                cs = slice(c * LANES, (c + 1) * LANES)
                o_s[g * tiles + c, rs, :] = o_refs[g][0, r, :, cs].astype(F32)
                l_s[g * tiles + c, rs, :] = l_refs[g][0, r, :, cs]
    merged = []
    for c in range(tiles):
        lses = [l_s[g * tiles + c] for g in range(N_GROUPS)]
        mx = functools.reduce(jnp.maximum, lses)
        num = jnp.zeros_like(mx)
        den = jnp.zeros_like(mx)
        for g in range(N_GROUPS):
            e = jnp.exp(lses[g] - mx)
            num = num + e * o_s[g * tiles + c]
            den = den + e
        merged.append((num / den).astype(BF16))
    out_ref[...] = x_ref[...] + _dot(jnp.concatenate(merged, axis=1), wo_ref[...])


def _attn_merge(outs, x, w_o):
    bsz, seq, d = x.shape
    rows = MERGE_ROWS
    args, in_specs = [], []
    for which in range(2):
        for g, (_, dil) in enumerate(DIL_PATTERNS):
            args.append(outs[g][which])
            in_specs.append(pl.BlockSpec((1, dil, rows // dil, ATTN_WIDTH), lambda b, i: (b, 0, i, 0)))
    args += [x, w_o]
    in_specs += [pl.BlockSpec((None, rows, d), lambda b, i: (b, i, 0)), _const_spec(w_o.shape)]
    return pl.pallas_call(
        _merge_body,
        out_shape=jax.ShapeDtypeStruct((bsz, seq, d), F32),
        grid=(bsz, seq // rows),
        in_specs=in_specs,
        out_specs=pl.BlockSpec((None, rows, d), lambda b, i: (b, i, 0)),
        scratch_shapes=[pltpu.VMEM((N_GROUPS * ATTN_WIDTH // LANES, rows, LANES), F32),
                        pltpu.VMEM((N_GROUPS * ATTN_WIDTH // LANES, rows, LANES), F32)],
        compiler_params=_params("parallel", "parallel"),
        name="attn_merge",
    )(*args)


def _router_body(x_ref, g_ref, wr_ref, br_ref, hn_ref, aff_ref):
    hn = _rms(x_ref[...], g_ref[...])
    hi = hn.astype(BF16)
    lo = (hn - hi.astype(F32)).astype(BF16)
    hn_ref[...] = hi
    both = _dot(hi, wr_ref[...])
    logits = both[:, :LANES] + both[:, LANES:] + _dot(lo, wr_ref[:, :LANES]) + br_ref[...]
    m = jnp.max(logits, axis=-1, keepdims=True)
    e = jnp.exp(logits - m)
    aff_ref[...] = e / jnp.sum(e, axis=-1, keepdims=True)


def _router(x2, gain, w_r, b_r):
    n, d = x2.shape
    rows = ROUTER_ROWS
    return pl.pallas_call(
        _router_body,
        out_shape=[jax.ShapeDtypeStruct((n, d), BF16), jax.ShapeDtypeStruct((n, LANES), F32)],
        grid=(n // rows,),
        in_specs=[pl.BlockSpec((rows, d), lambda i: (i, 0)), _const_spec(gain.shape),
                  _const_spec(w_r.shape), _const_spec(b_r.shape)],
        out_specs=[pl.BlockSpec((rows, d), lambda i: (i, 0)), pl.BlockSpec((rows, LANES), lambda i: (i, 0))],
        compiler_params=_params("parallel"),
        name="moe_router",
    )(x2, gain, w_r, b_r)


def _select_body(cap, aff_ref, tri_ref, post_ref, gatet_ref, pos_ref, lo_ref):
    seq = aff_ref.shape[1]
    aff_t = aff_ref[0].T[:N_EXPERTS]
    key = pltpu.bitcast(aff_t, I32)
    thr = jnp.zeros((N_EXPERTS, 1), I32)
    for bit in range(30, -1, -1):
        cand = thr | (1 << bit)
        cnt = jnp.sum(jnp.where(key >= cand, 1.0, 0.0), axis=1, keepdims=True)
        thr = jnp.where(cnt >= cap, cand, thr)
    gt = key > thr
    eq = key == thr
    need = cap - jnp.sum(jnp.where(gt, 1.0, 0.0), axis=1, keepdims=True)
    tri = tri_ref[...]
    nblk = seq // TOKEN_BLOCK

    def prefix(mask_f, want_offsets):
        off = jnp.zeros((N_EXPERTS, 1), F32)
        parts, offs = [], []
        for jb in range(nblk):
            blk = mask_f[:, jb * TOKEN_BLOCK:(jb + 1) * TOKEN_BLOCK]
            offs.append(off)
            parts.append(_dot(blk.astype(BF16), tri) + off)
            off = off + jnp.sum(blk, axis=1, keepdims=True)
        offs.append(off)
        return jnp.concatenate(parts, axis=1), (offs if want_offsets else None)

    eq_rank, _ = prefix(jnp.where(eq, 1.0, 0.0), False)
    sel = gt | (eq & (eq_rank < need))
    rank, offs = prefix(jnp.where(sel, 1.0, 0.0), True)
    post = jnp.where(sel, rank, -1.0)
    post_ref[0] = post.astype(I32)
    gatet_ref[0] = jnp.where(sel, aff_t, 0.0)
    padded = jnp.concatenate([post, jnp.full((LANES - N_EXPERTS, seq), -1.0, F32)], axis=0)
    pos_ref[0] = padded.T.astype(I32)
    lane = lax.broadcasted_iota(I32, (N_EXPERTS, LANES), 1)
    lo = jnp.zeros((N_EXPERTS, LANES), F32)
    for jb, off in enumerate(offs):
        lo = jnp.where(lane == jb, off, lo)
    lo_ref[0] = lo.astype(I32)


def _select(aff, tri, cap):
    bsz, seq, _ = aff.shape
    return pl.pallas_call(
        functools.partial(_select_body, cap),
        out_shape=[jax.ShapeDtypeStruct((bsz, N_EXPERTS, seq), I32),
                   jax.ShapeDtypeStruct((bsz, N_EXPERTS, seq), F32),
                   jax.ShapeDtypeStruct((bsz, seq, LANES), I32),
                   jax.ShapeDtypeStruct((bsz, N_EXPERTS, LANES), I32)],
        grid=(bsz,),
        in_specs=[pl.BlockSpec((1, seq, LANES), lambda b: (b, 0, 0)), _const_spec(tri.shape)],
        out_specs=[pl.BlockSpec((1, N_EXPERTS, seq), lambda b: (b, 0, 0)),
                   pl.BlockSpec((1, N_EXPERTS, seq), lambda b: (b, 0, 0)),
                   pl.BlockSpec((1, seq, LANES), lambda b: (b, 0, 0)),
                   pl.BlockSpec((1, N_EXPERTS, LANES), lambda b: (b, 0, 0))],
        compiler_params=_params("parallel"),
        name="moe_select",
    )(aff, tri)


def _gather_body(cap, lo_ref, hn_ref, post_ref, gatet_ref, xin_ref, gate_ref, xin_s, gate_s):
    b = pl.program_id(0)
    eg = pl.program_id(1)
    seq = hn_ref.shape[1]
    nblk = seq // TOKEN_BLOCK
    xin_s[...] = jnp.zeros_like(xin_s)
    gate_s[...] = jnp.zeros_like(gate_s)

    def table(i):
        return (b * N_EXPERTS + eg * PACK + i) * LO_STRIDE

    def accumulate(i, s0, rows, hit, gate_row, contrib):
        xin_s[i, pl.ds(s0, rows), :] += contrib
        gsum = jnp.sum(jnp.where(hit, gate_row, 0.0), axis=1, keepdims=True)
        gate_s[i, pl.ds(s0, rows), :] += jnp.broadcast_to(gsum, (rows, LANES))

    first_iota = lax.broadcasted_iota(I32, (FIRST_WINDOW, TOKEN_BLOCK), 0)
    for j in range(nblk):
        ts = slice(j * TOKEN_BLOCK, (j + 1) * TOKEN_BLOCK)
        starts, hits = [], []
        for i in range(PACK):
            start = pl.multiple_of(_floor_to(lo_ref[table(i) + j], 8), 8)
            starts.append(start)
            hits.append(post_ref[0, i, :, ts] == (first_iota + start))
        onehot = jnp.concatenate([jnp.where(h, 1.0, 0.0).astype(BF16) for h in hits], axis=0)
        contrib = _dot(onehot, hn_ref[0, ts, :])
        for i in range(PACK):
            accumulate(i, starts[i], FIRST_WINDOW, hits[i], gatet_ref[0, i, :, ts],
                       contrib[i * FIRST_WINDOW:(i + 1) * FIRST_WINDOW])

    more_iota = lax.broadcasted_iota(I32, (GATHER_WINDOW, TOKEN_BLOCK), 0)
    for j in range(nblk):
        ts = slice(j * TOKEN_BLOCK, (j + 1) * TOKEN_BLOCK)

        def expert(i, carry, j=j, ts=ts):
            done = _floor_to(lo_ref[table(i) + j], 8) + FIRST_WINDOW
            left = jnp.maximum(lo_ref[table(i) + j + 1] - done, 0)

            def window(w, carry):
                s0 = pl.multiple_of(done + w * GATHER_WINDOW, 8)
                hit = post_ref[0, i, :, ts] == (more_iota + s0)
                contrib = _dot(jnp.where(hit, 1.0, 0.0).astype(BF16), hn_ref[0, ts, :])
                accumulate(i, s0, GATHER_WINDOW, hit, gatet_ref[0, i, :, ts], contrib)
                return carry

            return lax.fori_loop(0, _ceil_div(left, GATHER_WINDOW), window, carry)

        lax.fori_loop(0, PACK, expert, 0)
    for i in range(PACK):
        xin_ref[0, i] = xin_s[i, :cap, :].astype(BF16)
        gate_ref[0, i] = gate_s[i, :cap, :]


def _gather(lo_flat, hn, post, gatet, cap):
    bsz, seq, d = hn.shape
    pad_rows = cap + GATHER_WINDOW
    post4 = post.reshape(bsz, N_EXPERTS, 1, seq)
    gatet4 = gatet.reshape(bsz, N_EXPERTS, 1, seq)
    grid_spec = pltpu.PrefetchScalarGridSpec(
        num_scalar_prefetch=1,
        grid=(bsz, N_EXPERTS // PACK),
        in_specs=[pl.BlockSpec((1, seq, d), lambda b, e, lo: (b, 0, 0)),
                  pl.BlockSpec((1, PACK, 1, seq), lambda b, e, lo: (b, e, 0, 0)),
                  pl.BlockSpec((1, PACK, 1, seq), lambda b, e, lo: (b, e, 0, 0))],
        out_specs=[pl.BlockSpec((1, PACK, cap, d), lambda b, e, lo: (b, e, 0, 0)),
                   pl.BlockSpec((1, PACK, cap, LANES), lambda b, e, lo: (b, e, 0, 0))],
        scratch_shapes=[pltpu.VMEM((PACK, pad_rows, d), F32), pltpu.VMEM((PACK, pad_rows, LANES), F32)],
    )
    return pl.pallas_call(
        functools.partial(_gather_body, cap),
        out_shape=[jax.ShapeDtypeStruct((bsz, N_EXPERTS, cap, d), BF16),
                   jax.ShapeDtypeStruct((bsz, N_EXPERTS, cap, LANES), F32)],
        grid_spec=grid_spec,
        compiler_params=_params("parallel", "arbitrary"),
        name="moe_gather",
    )(lo_flat, hn, post4, gatet4)


def _expert_body(xin_ref, gate_ref, w1c_ref, w3c_ref, w2c_ref, y_ref, w1_s, w3_s, w2_s):
    r = pl.program_id(0)
    b = pl.program_id(1)
    fill = lax.rem(r, 2)
    d_rows = w1c_ref.shape[2]
    f_rows = w2c_ref.shape[2]
    d0 = pl.multiple_of(b * d_rows, d_rows)
    f0 = pl.multiple_of(b * f_rows, f_rows)
    w1_s[fill, pl.ds(d0, d_rows), :] = w1c_ref[0, 0].astype(BF16)
    w3_s[fill, pl.ds(d0, d_rows), :] = w3c_ref[0, 0].astype(BF16)
    w2_s[fill, pl.ds(f0, f_rows), :] = w2c_ref[0, 0].astype(BF16)

    @pl.when(r == 0)
    def _():
        y_ref[0, 0] = jnp.zeros(y_ref.shape[2:], BF16)

    @pl.when(r > 0)
    def _():
        use = 1 - fill
        xin = xin_ref[0, 0]
        chunk = EXPERT_CHUNK
        y = jnp.zeros(y_ref.shape[2:], F32)
        for c in range(w1_s.shape[2] // chunk):
            fs = slice(c * chunk, (c + 1) * chunk)
            hid = jax.nn.silu(_dot(xin, w1_s[use, :, fs])) * _dot(xin, w3_s[use, :, fs])
            y = y + _dot(hid.astype(BF16), w2_s[use, fs, :])
        y_ref[0, 0] = (y * gate_ref[0, 0, :, 0:1]).astype(BF16)


def _experts(layer, xin, gate, w1, w3, w2):
    bsz, ne, cap, d = xin.shape
    f = w1.shape[3]
    d_rows, f_rows = d // bsz, f // bsz
    assert d_rows * bsz == d and f_rows * bsz == f

    def act(width):
        return pl.BlockSpec((1, 1, cap, width), lambda r, b: (b, jnp.maximum(r - 1, 0), 0, 0))

    def weight_slice(rows, cols):
        return pl.BlockSpec((1, 1, rows, cols), lambda r, b: (layer, jnp.minimum(r, ne - 1), b, 0))

    return pl.pallas_call(
        _expert_body,
        out_shape=jax.ShapeDtypeStruct((bsz, ne, cap, d), BF16),
        grid=(ne + 1, bsz),
        in_specs=[act(d), act(LANES), weight_slice(d_rows, f), weight_slice(d_rows, f), weight_slice(f_rows, d)],
        out_specs=act(d),
        scratch_shapes=[pltpu.VMEM((2, d, f), BF16), pltpu.VMEM((2, d, f), BF16), pltpu.VMEM((2, f, d), BF16)],
        compiler_params=_params("arbitrary", "arbitrary"),
        name="moe_experts",
    )(xin, gate, w1, w3, w2)


def _combine_body(lo_ref, x_ref, y_ref, pos_ref, o_ref, acc_s):
    b = pl.program_id(0)
    j = pl.program_id(1)
    cap = y_ref.shape[2]
    pos = pos_ref[0]

    def bounds(e):
        base = (b * N_EXPERTS + e) * LO_STRIDE
        return lo_ref[base + j], lo_ref[base + j + 1]

    lane = lax.broadcasted_iota(I32, (TOKEN_BLOCK, PACK * FIRST_WINDOW), 1)
    lane_expert = lane // FIRST_WINDOW
    lane_slot = lane % FIRST_WINDOW
    acc = x_ref[0]
    for q in range(N_EXPERTS // PACK):
        target = jnp.full(lane.shape, -1, I32)
        y_rows = []
        for i in range(PACK):
            e = q * PACK + i
            start = _floor_to(bounds(e)[0], 16)
            s0c = pl.multiple_of(jnp.minimum(start, cap - FIRST_WINDOW), 16)
            pcol = pos[:, e:e + 1]
            rel = jnp.where(pcol >= start, pcol - s0c, -1)
            target = jnp.where(lane_expert == i, rel, target)
            y_rows.append(y_ref[0, e, pl.ds(s0c, FIRST_WINDOW), :])
        onehot = jnp.where(target == lane_slot, 1.0, 0.0).astype(BF16)
        acc = acc + _dot(onehot, jnp.concatenate(y_rows, axis=0))
    acc_s[...] = acc

    more_lane = lax.broadcasted_iota(I32, (TOKEN_BLOCK, SCATTER_WINDOW), 1)
    for e in range(N_EXPERTS):
        lo, hi = bounds(e)
        done = _floor_to(lo, 16) + FIRST_WINDOW
        pcol = pos[:, e:e + 1]

        def window(w, carry, e=e, done=done, pcol=pcol):
            s0 = done + w * SCATTER_WINDOW
            s0c = pl.multiple_of(jnp.minimum(s0, cap - SCATTER_WINDOW), 16)
            hit = ((pcol - s0c) == more_lane) & (pcol >= s0)
            onehot = jnp.where(hit, 1.0, 0.0).astype(BF16)
            acc_s[...] += _dot(onehot, y_ref[0, e, pl.ds(s0c, SCATTER_WINDOW), :])
            return carry

        lax.fori_loop(0, _ceil_div(jnp.maximum(hi - done, 0), SCATTER_WINDOW), window, 0)
    o_ref[0] = acc_s[...]


def _combine(lo_flat, x, y, pos):
    bsz, seq, d = x.shape
    _, ne, cap, _ = y.shape
    grid_spec = pltpu.PrefetchScalarGridSpec(
        num_scalar_prefetch=1,
        grid=(bsz, seq // TOKEN_BLOCK),
        in_specs=[pl.BlockSpec((1, TOKEN_BLOCK, d), lambda b, j, lo: (b, j, 0)),
                  pl.BlockSpec((1, ne, cap, d), lambda b, j, lo: (b, 0, 0, 0)),
                  pl.BlockSpec((1, TOKEN_BLOCK, LANES), lambda b, j, lo: (b, j, 0))],
        out_specs=pl.BlockSpec((1, TOKEN_BLOCK, d), lambda b, j, lo: (b, j, 0)),
        scratch_shapes=[pltpu.VMEM((TOKEN_BLOCK, d), F32)],
    )
    return pl.pallas_call(
        _combine_body,
        out_shape=jax.ShapeDtypeStruct((bsz, seq, d), F32),
        grid_spec=grid_spec,
        compiler_params=_params("parallel", "arbitrary"),
        name="moe_combine",
    )(lo_flat, x, y, pos)


def _ec_moe(layer, x, gain, w_r, b_r, w1, w3, w2, tri):
    bsz, seq, d = x.shape
    cap = CAPACITY_FACTOR * seq // N_EXPERTS
    hn, aff = _router(x.reshape(bsz * seq, d), gain, w_r, b_r)
    hn = hn.reshape(bsz, seq, d)
    post, gatet, pos, lo = _select(aff.reshape(bsz, seq, LANES), tri, cap)
    lo_flat = lo[:, :, :LO_STRIDE].reshape(-1)
    xin, gate = _gather(lo_flat, hn, post, gatet, cap)
    y = _experts(layer, xin, gate, w1, w3, w2)
    return _combine(lo_flat, x, y, pos)


def _group_major(a):
    lead = a.shape[:-1]
    a = a.reshape(lead + (3, N_GROUPS, ATTN_WIDTH))
    return jnp.swapaxes(a, -3, -2).reshape(lead + (3 * N_GROUPS * ATTN_WIDTH,))


def kernel(x, mix_norm, ffn_norm, gm_w_in, gm_b_in, gm_v_norm, gm_w_s, gm_b_s, gm_w_out, gm_b_out,
           at_w_qkv, at_q_norm, at_k_norm, at_w_o, moe_w_router, moe_b_router, moe_w1, moe_w3, moe_w2):
    bsz, seq, d = x.shape
    depth = mix_norm.shape[0]
    n_mixers = 2

    lane_head = jnp.arange(MXU_DIM) // HEAD_DIM
    head_ones = (lane_head[:, None] == lane_head[None, :]).astype(BF16)
    tri = (jnp.arange(TOKEN_BLOCK)[:, None] < jnp.arange(TOKEN_BLOCK)[None, :]).astype(BF16)
    biases = [_attn_bias(dil) for _, dil in DIL_PATTERNS]
    pad_e = LANES - N_EXPERTS

    for i in range(depth):
        j = i // n_mixers
        gain = mix_norm[i][None, :]
        if i % n_mixers == 0:
            x = _gmlp_layer(
                x.reshape(bsz * seq, d), gain, gm_w_in[j].astype(BF16), gm_b_in[j][None, :],
                gm_v_norm[j][None, :], gm_w_s[j].astype(BF16), gm_b_s[j].T, gm_w_out[j].astype(BF16),
                gm_b_out[j][None, :]).reshape(bsz, seq, d)
        else:
            def per_head(g):
                return jnp.tile(g[:, None, :], (1, HEADS, 1)).reshape(N_GROUPS, ATTN_WIDTH)
            qk_gain = jnp.stack([per_head(at_q_norm[j]) * (HEAD_DIM ** -0.5), per_head(at_k_norm[j]),
                                 jnp.ones((N_GROUPS, ATTN_WIDTH), F32)]).reshape(1, -1)
            qkvs = _qkv_proj(x, gain, _group_major(at_w_qkv[j]).astype(BF16), _group_major(qk_gain), head_ones)
            outs = [_attn_group(g, qkvs[g], biases[g]) for g in range(N_GROUPS)]
            x = _attn_merge(outs, x, at_w_o[j].astype(BF16))
        w_r = jnp.pad(moe_w_router[i], ((0, 0), (0, pad_e)))
        w_hi = w_r.astype(BF16)
        w_split = jnp.concatenate([w_hi, (w_r - w_hi.astype(F32)).astype(BF16)], axis=1)
        b_r = jnp.pad(moe_b_router[i], (0, pad_e), constant_values=NEG_INF)[None, :]
        x = _ec_moe(i, x, ffn_norm[i][None, :], w_split, b_r, moe_w1, moe_w3, moe_w2, tri)
    return x
```

```python
import functools

import jax
import jax.numpy as jnp
from jax import lax
from jax.experimental import pallas as pl
from jax.experimental.pallas import tpu as pltpu

F32 = jnp.float32
BF16 = jnp.bfloat16
I32 = jnp.int32

EPS = 1e-6
NEG_INF = -1e30

LANES = 128
MXU_DIM = 256
VMEM_LIMIT_BYTES = 56 * 1024 * 1024

GMLP_CHUNK = 128
GMLP_GROUPS = 8
DIL_PATTERNS = ((128, 1), (512, 4), (2048, 16))
N_GROUPS = len(DIL_PATTERNS)
HEADS = 8
HEAD_DIM = 64
ATTN_WIDTH = HEADS * HEAD_DIM
GROUP_WIDTH = 3 * ATTN_WIDTH
QUERY_BLOCK = 128
HALF_WINDOW = 64
N_EXPERTS = 16
CAPACITY_FACTOR = 2

GMLP_ROWS = 512
QKV_ROWS = 512
MERGE_ROWS = 256
ATTN_ROWS = 2 * QUERY_BLOCK
TOKEN_BLOCK = MXU_DIM
PACK = 4
FIRST_WINDOW = MXU_DIM // PACK
GATHER_WINDOW = 128
SCATTER_WINDOW = MXU_DIM
LO_STRIDE = 32
EXPERT_CHUNK = 2 * MXU_DIM


def _params(*semantics):
    return pltpu.CompilerParams(dimension_semantics=semantics, vmem_limit_bytes=VMEM_LIMIT_BYTES)


def _const_spec(shape):
    zeros = (0,) * len(shape)
    return pl.BlockSpec(shape, lambda *_: zeros, pipeline_mode=pl.Buffered(1))


def _rms(x, gain):
    return x * lax.rsqrt(jnp.mean(x * x, axis=-1, keepdims=True) + EPS) * gain


def _dot(a, b):
    return jnp.dot(a, b, preferred_element_type=F32)


def _floor_to(x, multiple):
    shift = multiple.bit_length() - 1
    return lax.shift_left(lax.shift_right_logical(x, shift), shift)


def _ceil_div(x, divisor):
    return lax.shift_right_logical(x + (divisor - 1), divisor.bit_length() - 1)


def _route(x, g_ref, wr_ref, br_ref, hn_ref, aff_ref):
    hn = _rms(x, g_ref[...])
    hi = hn.astype(BF16)
    lo = (hn - hi.astype(F32)).astype(BF16)
    hn_ref[...] = hi
    both = _dot(hi, wr_ref[...])
    logits = both[:, :LANES] + both[:, LANES:] + _dot(lo, wr_ref[:, :LANES]) + br_ref[...]
    m = jnp.max(logits, axis=-1, keepdims=True)
    e = jnp.exp(logits - m)
    aff_ref[...] = e / jnp.sum(e, axis=-1, keepdims=True)


def _gmlp_body(x_ref, g_ref, win_ref, bin_ref, vg_ref, ws_ref, bst_ref, wout_ref, bout_ref,
               rg_ref, wr_ref, br_ref, o_ref, hn_ref, aff_ref, v_s, uv_s):
    x = x_ref[...]
    rows, width = v_s.shape
    gw = width // GMLP_GROUPS
    h = _rms(x, g_ref[...]).astype(BF16)
    zv = jax.nn.gelu(_dot(h, win_ref[:, width:]) + bin_ref[:, width:])
    r = lax.rsqrt(jnp.mean(zv * zv, axis=-1, keepdims=True) + EPS)
    v_s[...] = (zv * r * vg_ref[...]).astype(BF16)
    pair = 2 * gw
    for p in range(GMLP_GROUPS // 2):
        zu = jax.nn.gelu(_dot(h, win_ref[:, p * pair:(p + 1) * pair]) + bin_ref[:, p * pair:(p + 1) * pair])
        for gl in range(2):
            g = 2 * p + gl
            cols = slice(g * gw, (g + 1) * gw)
            bias = bst_ref[:, g:g + 1]
            for c in range(rows // GMLP_CHUNK):
                rs = slice(c * GMLP_CHUNK, (c + 1) * GMLP_CHUNK)
                sv = _dot(ws_ref[g], v_s[rs, cols]) + bias
                uv_s[rs, cols] = (zu[rs, gl * gw:(gl + 1) * gw] * sv).astype(BF16)
    out = x + _dot(uv_s[...], wout_ref[...]) + bout_ref[...]
    o_ref[...] = out
    _route(out, rg_ref, wr_ref, br_ref, hn_ref, aff_ref)


def _gmlp_layer(x2, gain, w_in, b_in, v_gain, w_s, b_st, w_out, b_out, route):
    n, d = x2.shape
    width = w_out.shape[0]
    rows = GMLP_ROWS
    return pl.pallas_call(
        _gmlp_body,
        out_shape=[jax.ShapeDtypeStruct((n, d), F32), jax.ShapeDtypeStruct((n, d), BF16),
                   jax.ShapeDtypeStruct((n, LANES), F32)],
        grid=(n // rows,),
        in_specs=[
            pl.BlockSpec((rows, d), lambda i: (i, 0)),
            _const_spec(gain.shape), _const_spec(w_in.shape), _const_spec(b_in.shape),
            _const_spec(v_gain.shape), _const_spec(w_s.shape), _const_spec(b_st.shape),
            _const_spec(w_out.shape), _const_spec(b_out.shape),
        ] + [_const_spec(a.shape) for a in route],
        out_specs=[pl.BlockSpec((rows, d), lambda i: (i, 0)), pl.BlockSpec((rows, d), lambda i: (i, 0)),
                   pl.BlockSpec((rows, LANES), lambda i: (i, 0))],
        scratch_shapes=[pltpu.VMEM((rows, width), BF16), pltpu.VMEM((rows, width), BF16)],
        compiler_params=_params("parallel"),
        name="gmlp_layer",
    )(x2, gain, w_in, b_in, v_gain, w_s, b_st, w_out, b_out, *route)


def _qkv_body(x_ref, g_ref, w_ref, gain_ref, bd_ref, *rest):
    out_refs, t_s = rest[:N_GROUPS], rest[N_GROUPS]
    rows = t_s.shape[1]
    h = _rms(x_ref[...], g_ref[...]).astype(BF16)
    bd = bd_ref[...]

    def put(col0, t):
        for c in range(t.shape[1] // LANES):
            t_s[col0 // LANES + c] = t[:, c * LANES:(c + 1) * LANES]

    for g in range(N_GROUPS):
        tg = _dot(h, w_ref[:, g * GROUP_WIDTH:(g + 1) * GROUP_WIDTH])
        for c in range(2 * ATTN_WIDTH // MXU_DIM):
            col0 = g * GROUP_WIDTH + c * MXU_DIM
            t = tg[:, c * MXU_DIM:(c + 1) * MXU_DIM]
            t2 = t * t
            hi = t2.astype(BF16)
            lo = (t2 - hi.astype(F32)).astype(BF16)
            ss = _dot(hi, bd) + _dot(lo, bd)
            put(col0, t * lax.rsqrt(ss * (1.0 / HEAD_DIM) + EPS) * gain_ref[:, col0:col0 + MXU_DIM])
        put(g * GROUP_WIDTH + 2 * ATTN_WIDTH, tg[:, 2 * ATTN_WIDTH:])
    for g, (_, dil) in enumerate(DIL_PATTERNS):
        for r in range(dil):
            rs = slice(None) if dil == 1 else pl.ds(r, rows // dil, stride=dil)
            for c in range(GROUP_WIDTH // LANES):
                out_refs[g][0, r, :, c * LANES:(c + 1) * LANES] = (
                    t_s[g * (GROUP_WIDTH // LANES) + c, rs, :].astype(BF16))


def _qkv_proj(x, gain, w_qkv, qk_gain, head_ones):
    bsz, seq, d = x.shape
    width = w_qkv.shape[1]
    rows = QKV_ROWS
    out_shape, out_specs = [], []
    for _, dil in DIL_PATTERNS:
        out_shape.append(jax.ShapeDtypeStruct((bsz, dil, seq // dil, GROUP_WIDTH), BF16))
        out_specs.append(pl.BlockSpec((1, dil, rows // dil, GROUP_WIDTH), lambda b, i: (b, 0, i, 0)))
    return pl.pallas_call(
        _qkv_body,
        out_shape=out_shape,
        grid=(bsz, seq // rows),
        in_specs=[
            pl.BlockSpec((None, rows, d), lambda b, i: (b, i, 0)),
            _const_spec(gain.shape), _const_spec(w_qkv.shape), _const_spec(qk_gain.shape),
            _const_spec(head_ones.shape),
        ],
        out_specs=out_specs,
        scratch_shapes=[pltpu.VMEM((width // LANES, rows, LANES), F32)],
        compiler_params=_params("parallel", "parallel"),
        name="qkv_proj",
    )(x, gain, w_qkv, qk_gain, head_ones)


def _attn_body(q_ref, kp_ref, kc_ref, kn_ref, vp_ref, vc_ref, vn_ref, bias_ref, o_ref, l_ref):
    hw = HALF_WINDOW
    qb = QUERY_BLOCK
    n_sub = q_ref.shape[2] // qb
    last_block = pl.num_programs(2) * n_sub - 1
    k_all = jnp.concatenate([kp_ref[0, 0, qb - hw:, :], kc_ref[0, 0], kn_ref[0, 0, :hw, :]], axis=0)
    v_all = jnp.concatenate([vp_ref[0, 0, qb - hw:, :], vc_ref[0, 0], vn_ref[0, 0, :hw, :]], axis=0)
    heads_per_tile = MXU_DIM // HEAD_DIM
    head_of_lane = lax.broadcasted_iota(I32, (qb, MXU_DIM), 1) // HEAD_DIM
    for sub in range(n_sub):
        block = pl.program_id(2) * n_sub + sub
        variant = jnp.where(block == 0, 0, jnp.where(block == last_block, 2, 1))
        q = q_ref[0, 0, sub * qb:(sub + 1) * qb, :]
        k = k_all[sub * qb:sub * qb + qb + 2 * hw]
        v = v_all[sub * qb:sub * qb + qb + 2 * hw]
        for half in range(ATTN_WIDTH // MXU_DIM):
            cs = slice(half * MXU_DIM, (half + 1) * MXU_DIM)
            qh, kh, vh = q[:, cs], k[:, cs], v[:, cs]
            qs = jnp.concatenate(
                [jnp.where(head_of_lane == hh, qh, jnp.zeros_like(qh)) for hh in range(heads_per_tile)], axis=0)
            s = lax.dot_general(qs, kh, (((1,), (1,)), ((), ())), preferred_element_type=F32)
            s = s + bias_ref[variant, half]
            m = jnp.max(s, axis=-1, keepdims=True)
            p = jnp.exp(s - m)
            l = jnp.sum(p, axis=-1, keepdims=True)
            pv = _dot(p.astype(BF16), vh) / l
            lse = m + jnp.log(l)
            o_half = jnp.zeros((qb, MXU_DIM), F32)
            l_half = jnp.zeros((qb, MXU_DIM), F32)
            for hh in range(heads_per_tile):
                rs = slice(hh * qb, (hh + 1) * qb)
                sel = head_of_lane == hh
                o_half = jnp.where(sel, pv[rs], o_half)
                l_half = jnp.where(sel, lse[rs], l_half)
            o_ref[0, 0, sub * qb:(sub + 1) * qb, cs] = o_half.astype(BF16)
            l_ref[0, 0, sub * qb:(sub + 1) * qb, cs] = l_half


def _attn_group(g, qkv_g, bias):
    bsz, dil, sub_len, _ = qkv_g.shape
    rows = ATTN_ROWS
    per_step = rows // QUERY_BLOCK
    nb = sub_len // QUERY_BLOCK

    def main(which):
        return pl.BlockSpec((1, 1, rows, ATTN_WIDTH), lambda b, r, i: (b, r, i, which))

    def halo(which, offset):
        def index(b, r, i):
            return (b, r, jnp.clip(i * per_step + offset, 0, nb - 1), which)
        return pl.BlockSpec((1, 1, QUERY_BLOCK, ATTN_WIDTH), index)

    return pl.pallas_call(
        _attn_body,
        out_shape=[jax.ShapeDtypeStruct((bsz, dil, sub_len, ATTN_WIDTH), BF16),
                   jax.ShapeDtypeStruct((bsz, dil, sub_len, ATTN_WIDTH), F32)],
        grid=(bsz, dil, sub_len // rows),
        in_specs=[main(0), halo(1, -1), main(1), halo(1, per_step), halo(2, -1), main(2), halo(2, per_step),
                  _const_spec(bias.shape)],
        out_specs=[main(0), main(0)],
        compiler_params=_params("parallel", "parallel", "arbitrary"),
        name=f"dilated_attn_g{g}",
    )(*([qkv_g] * 7), bias)


def _attn_bias(dil):
    slopes = jnp.exp2(-8.0 * jnp.arange(1, HEADS + 1, dtype=F32) / HEADS)
    kb = QUERY_BLOCK + 2 * HALF_WINDOW
    rel = jnp.arange(kb)[None, :] - HALF_WINDOW - jnp.arange(QUERY_BLOCK)[:, None]
    band = jnp.abs(rel) <= HALF_WINDOW
    alibi = -slopes[:, None, None] * (dil * jnp.abs(rel)).astype(F32)[None]
    j = jnp.arange(kb)
    edge = jnp.stack([j >= HALF_WINDOW, j >= 0, j < kb - HALF_WINDOW])
    ok = band[None, None] & edge[:, None, None, :]
    tab = jnp.where(ok, alibi[None], NEG_INF)
    per_tile = MXU_DIM // HEAD_DIM
    return tab.reshape(3, HEADS // per_tile, per_tile * QUERY_BLOCK, kb)


def _merge_body(*refs):
    o_refs, l_refs = refs[:N_GROUPS], refs[N_GROUPS:2 * N_GROUPS]
    x_ref, wo_ref, rg_ref, wr_ref, br_ref, out_ref, hn_ref, aff_ref, o_s, l_s = refs[2 * N_GROUPS:]
    rows = x_ref.shape[0]
    tiles = ATTN_WIDTH // LANES
    for g, (_, dil) in enumerate(DIL_PATTERNS):
        for r in range(dil):
            rs = slice(None) if dil == 1 else pl.ds(r, rows // dil, stride=dil)
            for c in range(tiles):
                cs = slice(c * LANES, (c + 1) * LANES)
                o_s[g * tiles + c, rs, :] = o_refs[g][0, r, :, cs].astype(F32)
                l_s[g * tiles + c, rs, :] = l_refs[g][0, r, :, cs]
    merged = []
    for c in range(tiles):
        lses = [l_s[g * tiles + c] for g in range(N_GROUPS)]
        mx = functools.reduce(jnp.maximum, lses)
        num = jnp.zeros_like(mx)
        den = jnp.zeros_like(mx)
        for g in range(N_GROUPS):
            e = jnp.exp(lses[g] - mx)
            num = num + e * o_s[g * tiles + c]
            den = den + e
        merged.append((num / den).astype(BF16))
    out = x_ref[...] + _dot(jnp.concatenate(merged, axis=1), wo_ref[...])
    out_ref[...] = out
    _route(out, rg_ref, wr_ref, br_ref, hn_ref, aff_ref)


def _attn_merge(outs, x, w_o, route):
    bsz, seq, d = x.shape
    rows = MERGE_ROWS
    args, in_specs = [], []
    for which in range(2):
        for g, (_, dil) in enumerate(DIL_PATTERNS):
            args.append(outs[g][which])
            in_specs.append(pl.BlockSpec((1, dil, rows // dil, ATTN_WIDTH), lambda b, i: (b, 0, i, 0)))
    args += [x, w_o, *route]
    in_specs += [pl.BlockSpec((None, rows, d), lambda b, i: (b, i, 0)), _const_spec(w_o.shape)]
    in_specs += [_const_spec(a.shape) for a in route]

    def row_block(width):
        return pl.BlockSpec((None, rows, width), lambda b, i: (b, i, 0))

    return pl.pallas_call(
        _merge_body,
        out_shape=[jax.ShapeDtypeStruct((bsz, seq, d), F32), jax.ShapeDtypeStruct((bsz, seq, d), BF16),
                   jax.ShapeDtypeStruct((bsz, seq, LANES), F32)],
        grid=(bsz, seq // rows),
        in_specs=in_specs,
        out_specs=[row_block(d), row_block(d), row_block(LANES)],
        scratch_shapes=[pltpu.VMEM((N_GROUPS * ATTN_WIDTH // LANES, rows, LANES), F32),
                        pltpu.VMEM((N_GROUPS * ATTN_WIDTH // LANES, rows, LANES), F32)],
        compiler_params=_params("parallel", "parallel"),
        name="attn_merge",
    )(*args)


def _select_body(cap, aff_ref, tri_ref, post_ref, gatet_ref, pos_ref, lo_ref):
    seq = aff_ref.shape[1]
    aff_t = aff_ref[0].T[:N_EXPERTS]
    key = pltpu.bitcast(aff_t, I32)
    thr = jnp.zeros((N_EXPERTS, 1), I32)
    for bit in range(30, -1, -1):
        cand = thr | (1 << bit)
        cnt = jnp.sum(jnp.where(key >= cand, 1.0, 0.0), axis=1, keepdims=True)
        thr = jnp.where(cnt >= cap, cand, thr)
    gt = key > thr
    eq = key == thr
    need = cap - jnp.sum(jnp.where(gt, 1.0, 0.0), axis=1, keepdims=True)
    tri = tri_ref[...]
    nblk = seq // TOKEN_BLOCK

    def prefix(mask_f, want_offsets):
        off = jnp.zeros((N_EXPERTS, 1), F32)
        parts, offs = [], []
        for jb in range(nblk):
            blk = mask_f[:, jb * TOKEN_BLOCK:(jb + 1) * TOKEN_BLOCK]
            offs.append(off)
            parts.append(_dot(blk.astype(BF16), tri) + off)
            off = off + jnp.sum(blk, axis=1, keepdims=True)
        offs.append(off)
        return jnp.concatenate(parts, axis=1), (offs if want_offsets else None)

    eq_rank, _ = prefix(jnp.where(eq, 1.0, 0.0), False)
    sel = gt | (eq & (eq_rank < need))
    rank, offs = prefix(jnp.where(sel, 1.0, 0.0), True)
    post = jnp.where(sel, rank, -1.0)
    post_ref[0] = post.astype(I32)
    gatet_ref[0] = jnp.where(sel, aff_t, 0.0)
    padded = jnp.concatenate([post, jnp.full((LANES - N_EXPERTS, seq), -1.0, F32)], axis=0)
    pos_ref[0] = padded.T.astype(I32)
    lane = lax.broadcasted_iota(I32, (N_EXPERTS, LANES), 1)
    lo = jnp.zeros((N_EXPERTS, LANES), F32)
    for jb, off in enumerate(offs):
        lo = jnp.where(lane == jb, off, lo)
    lo_ref[0] = lo.astype(I32)


def _select(aff, tri, cap):
    bsz, seq, _ = aff.shape
    return pl.pallas_call(
        functools.partial(_select_body, cap),
        out_shape=[jax.ShapeDtypeStruct((bsz, N_EXPERTS, seq), I32),
                   jax.ShapeDtypeStruct((bsz, N_EXPERTS, seq), F32),
                   jax.ShapeDtypeStruct((bsz, seq, LANES), I32),
                   jax.ShapeDtypeStruct((bsz, N_EXPERTS, LANES), I32)],
        grid=(bsz,),
        in_specs=[pl.BlockSpec((1, seq, LANES), lambda b: (b, 0, 0)), _const_spec(tri.shape)],
        out_specs=[pl.BlockSpec((1, N_EXPERTS, seq), lambda b: (b, 0, 0)),
                   pl.BlockSpec((1, N_EXPERTS, seq), lambda b: (b, 0, 0)),
                   pl.BlockSpec((1, seq, LANES), lambda b: (b, 0, 0)),
                   pl.BlockSpec((1, N_EXPERTS, LANES), lambda b: (b, 0, 0))],
        compiler_params=_params("parallel"),
        name="moe_select",
    )(aff, tri)


def _gather_body(lo_ref, hn_ref, post_ref, gatet_ref, xin_ref, gate_ref):
    b = pl.program_id(0)
    eg = pl.program_id(1)
    seq = hn_ref.shape[1]
    cap = xin_ref.shape[2]
    nblk = seq // TOKEN_BLOCK
    xin_ref[...] = jnp.zeros_like(xin_ref)
    gate_ref[...] = jnp.zeros_like(gate_ref)

    def table(i):
        return (b * N_EXPERTS + eg * PACK + i) * LO_STRIDE

    def window(i, ts, s0, rows, iota):
        s0c = pl.multiple_of(jnp.minimum(s0, cap - rows), 16)
        slot = iota + s0c
        hit = (post_ref[0, i, :, ts] == slot) & (slot >= s0)

        def add(contrib):
            xin_ref[0, i, pl.ds(s0c, rows), :] += contrib.astype(BF16)
            gsum = jnp.sum(jnp.where(hit, gatet_ref[0, i, :, ts], 0.0), axis=1, keepdims=True)
            gate_ref[0, i, pl.ds(s0c, rows), :] += jnp.broadcast_to(gsum, (rows, LANES))

        return jnp.where(hit, 1.0, 0.0).astype(BF16), add

    first_iota = lax.broadcasted_iota(I32, (FIRST_WINDOW, TOKEN_BLOCK), 0)
    for j in range(nblk):
        ts = slice(j * TOKEN_BLOCK, (j + 1) * TOKEN_BLOCK)
        rows, adds = zip(*[window(i, ts, _floor_to(lo_ref[table(i) + j], 16), FIRST_WINDOW, first_iota)
                           for i in range(PACK)])
        contrib = _dot(jnp.concatenate(rows, axis=0), hn_ref[0, ts, :])
        for i in range(PACK):
            adds[i](contrib[i * FIRST_WINDOW:(i + 1) * FIRST_WINDOW])

    def left_over(i, j):
        done = _floor_to(lo_ref[table(i) + j], 16) + FIRST_WINDOW
        return jnp.maximum(lo_ref[table(i) + j + 1] - done, 0), done

    most = functools.reduce(jnp.maximum, [left_over(i, j)[0] for i in range(PACK) for j in range(nblk)])

    @pl.when(most > 0)
    def _():
        more_iota = lax.broadcasted_iota(I32, (GATHER_WINDOW, TOKEN_BLOCK), 0)
        for j in range(nblk):
            ts = slice(j * TOKEN_BLOCK, (j + 1) * TOKEN_BLOCK)

            def expert(i, carry, j=j, ts=ts):
                left, done = left_over(i, j)

                def further(w, carry):
                    onehot, add = window(i, ts, done + w * GATHER_WINDOW, GATHER_WINDOW, more_iota)
                    add(_dot(onehot, hn_ref[0, ts, :]))
                    return carry

                return lax.fori_loop(0, _ceil_div(left, GATHER_WINDOW), further, carry)

            lax.fori_loop(0, PACK, expert, 0)


def _gather(lo_flat, hn, post, gatet, cap):
    bsz, seq, d = hn.shape
    post4 = post.reshape(bsz, N_EXPERTS, 1, seq)
    gatet4 = gatet.reshape(bsz, N_EXPERTS, 1, seq)
    grid_spec = pltpu.PrefetchScalarGridSpec(
        num_scalar_prefetch=1,
        grid=(bsz, N_EXPERTS // PACK),
        in_specs=[pl.BlockSpec((1, seq, d), lambda b, e, lo: (b, 0, 0)),
                  pl.BlockSpec((1, PACK, 1, seq), lambda b, e, lo: (b, e, 0, 0)),
                  pl.BlockSpec((1, PACK, 1, seq), lambda b, e, lo: (b, e, 0, 0))],
        out_specs=[pl.BlockSpec((1, PACK, cap, d), lambda b, e, lo: (b, e, 0, 0)),
                   pl.BlockSpec((1, PACK, cap, LANES), lambda b, e, lo: (b, e, 0, 0))],
    )
    return pl.pallas_call(
        _gather_body,
        out_shape=[jax.ShapeDtypeStruct((bsz, N_EXPERTS, cap, d), BF16),
                   jax.ShapeDtypeStruct((bsz, N_EXPERTS, cap, LANES), F32)],
        grid_spec=grid_spec,
        compiler_params=_params("parallel", "arbitrary"),
        name="moe_gather",
    )(lo_flat, hn, post4, gatet4)


def _expert_body(xin_ref, gate_ref, w1c_ref, w3c_ref, w2c_ref, y_ref, w1_s, w3_s, w2_s):
    r = pl.program_id(0)
    b = pl.program_id(1)
    fill = lax.rem(r, 2)
    d_rows = w1c_ref.shape[2]
    f_rows = w2c_ref.shape[2]
    d0 = pl.multiple_of(b * d_rows, d_rows)
    f0 = pl.multiple_of(b * f_rows, f_rows)
    w1_s[fill, pl.ds(d0, d_rows), :] = w1c_ref[0, 0].astype(BF16)
    w3_s[fill, pl.ds(d0, d_rows), :] = w3c_ref[0, 0].astype(BF16)
    w2_s[fill, pl.ds(f0, f_rows), :] = w2c_ref[0, 0].astype(BF16)

    @pl.when(r == 0)
    def _():
        y_ref[0, 0] = jnp.zeros(y_ref.shape[2:], BF16)

    @pl.when(r > 0)
    def _():
        use = 1 - fill
        xin = xin_ref[0, 0]
        chunk = EXPERT_CHUNK
        y = jnp.zeros(y_ref.shape[2:], F32)
        for c in range(w1_s.shape[2] // chunk):
            fs = slice(c * chunk, (c + 1) * chunk)
            hid = jax.nn.silu(_dot(xin, w1_s[use, :, fs])) * _dot(xin, w3_s[use, :, fs])
            y = y + _dot(hid.astype(BF16), w2_s[use, fs, :])
        y_ref[0, 0] = (y * gate_ref[0, 0, :, 0:1]).astype(BF16)


def _experts(layer, xin, gate, w1, w3, w2):
    bsz, ne, cap, d = xin.shape
    f = w1.shape[3]
    d_rows, f_rows = d // bsz, f // bsz
    assert d_rows * bsz == d and f_rows * bsz == f

    def act(width):
        return pl.BlockSpec((1, 1, cap, width), lambda r, b: (b, jnp.maximum(r - 1, 0), 0, 0))

    def weight_slice(rows, cols):
        return pl.BlockSpec((1, 1, rows, cols), lambda r, b: (layer, jnp.minimum(r, ne - 1), b, 0))

    return pl.pallas_call(
        _expert_body,
        out_shape=jax.ShapeDtypeStruct((bsz, ne, cap, d), BF16),
        grid=(ne + 1, bsz),
        in_specs=[act(d), act(LANES), weight_slice(d_rows, f), weight_slice(d_rows, f), weight_slice(f_rows, d)],
        out_specs=act(d),
        scratch_shapes=[pltpu.VMEM((2, d, f), BF16), pltpu.VMEM((2, d, f), BF16), pltpu.VMEM((2, f, d), BF16)],
        compiler_params=_params("arbitrary", "arbitrary"),
        name="moe_experts",
    )(xin, gate, w1, w3, w2)


def _combine_body(lo_ref, x_ref, y_ref, pos_ref, o_ref, acc_s):
    b = pl.program_id(0)
    j = pl.program_id(1)
    cap = y_ref.shape[2]
    pos = pos_ref[0]

    def bounds(e):
        base = (b * N_EXPERTS + e) * LO_STRIDE
        return lo_ref[base + j], lo_ref[base + j + 1]

    lane = lax.broadcasted_iota(I32, (TOKEN_BLOCK, PACK * FIRST_WINDOW), 1)
    lane_expert = lane // FIRST_WINDOW
    lane_slot = lane % FIRST_WINDOW
    acc = x_ref[0]
    for q in range(N_EXPERTS // PACK):
        target = jnp.full(lane.shape, -1, I32)
        y_rows = []
        for i in range(PACK):
            e = q * PACK + i
            start = _floor_to(bounds(e)[0], 16)
            s0c = pl.multiple_of(jnp.minimum(start, cap - FIRST_WINDOW), 16)
            pcol = pos[:, e:e + 1]
            rel = jnp.where(pcol >= start, pcol - s0c, -1)
            target = jnp.where(lane_expert == i, rel, target)
            y_rows.append(y_ref[0, e, pl.ds(s0c, FIRST_WINDOW), :])
        onehot = jnp.where(target == lane_slot, 1.0, 0.0).astype(BF16)
        acc = acc + _dot(onehot, jnp.concatenate(y_rows, axis=0))
    acc_s[...] = acc

    def left_over(e):
        lo, hi = bounds(e)
        done = _floor_to(lo, 16) + FIRST_WINDOW
        return jnp.maximum(hi - done, 0), done

    most = functools.reduce(jnp.maximum, [left_over(e)[0] for e in range(N_EXPERTS)])

    @pl.when(most > 0)
    def _():
        more_lane = lax.broadcasted_iota(I32, (TOKEN_BLOCK, SCATTER_WINDOW), 1)
        for e in range(N_EXPERTS):
            left, done = left_over(e)
            pcol = pos[:, e:e + 1]

            def window(w, carry, e=e, done=done, pcol=pcol):
                s0 = done + w * SCATTER_WINDOW
                s0c = pl.multiple_of(jnp.minimum(s0, cap - SCATTER_WINDOW), 16)
                hit = ((pcol - s0c) == more_lane) & (pcol >= s0)
                onehot = jnp.where(hit, 1.0, 0.0).astype(BF16)
                acc_s[...] += _dot(onehot, y_ref[0, e, pl.ds(s0c, SCATTER_WINDOW), :])
                return carry

            lax.fori_loop(0, _ceil_div(left, SCATTER_WINDOW), window, 0)

    o_ref[0] = acc_s[...]


def _combine(lo_flat, x, y, pos):
    bsz, seq, d = x.shape
    _, ne, cap, _ = y.shape
    grid_spec = pltpu.PrefetchScalarGridSpec(
        num_scalar_prefetch=1,
        grid=(bsz, seq // TOKEN_BLOCK),
        in_specs=[pl.BlockSpec((1, TOKEN_BLOCK, d), lambda b, j, lo: (b, j, 0)),
                  pl.BlockSpec((1, ne, cap, d), lambda b, j, lo: (b, 0, 0, 0)),
                  pl.BlockSpec((1, TOKEN_BLOCK, LANES), lambda b, j, lo: (b, j, 0))],
        out_specs=pl.BlockSpec((1, TOKEN_BLOCK, d), lambda b, j, lo: (b, j, 0)),
        scratch_shapes=[pltpu.VMEM((TOKEN_BLOCK, d), F32)],
    )
    return pl.pallas_call(
        _combine_body,
        out_shape=jax.ShapeDtypeStruct((bsz, seq, d), F32),
        grid_spec=grid_spec,
        compiler_params=_params("parallel", "arbitrary"),
        name="moe_combine",
    )(lo_flat, x, y, pos)


def _ec_moe(layer, x, hn, aff, w1, w3, w2, tri):
    bsz, seq, d = x.shape
    cap = CAPACITY_FACTOR * seq // N_EXPERTS
    post, gatet, pos, lo = _select(aff, tri, cap)
    lo_flat = lo[:, :, :LO_STRIDE].reshape(-1)
    xin, gate = _gather(lo_flat, hn, post, gatet, cap)
    y = _experts(layer, xin, gate, w1, w3, w2)
    return _combine(lo_flat, x, y, pos)


def _group_major(a):
    lead = a.shape[:-1]
    a = a.reshape(lead + (3, N_GROUPS, ATTN_WIDTH))
    return jnp.swapaxes(a, -3, -2).reshape(lead + (3 * N_GROUPS * ATTN_WIDTH,))


def kernel(x, mix_norm, ffn_norm, gm_w_in, gm_b_in, gm_v_norm, gm_w_s, gm_b_s, gm_w_out, gm_b_out,
           at_w_qkv, at_q_norm, at_k_norm, at_w_o, moe_w_router, moe_b_router, moe_w1, moe_w3, moe_w2):
    bsz, seq, d = x.shape
    depth = mix_norm.shape[0]
    n_mixers = 2

    lane_head = jnp.arange(MXU_DIM) // HEAD_DIM
    head_ones = (lane_head[:, None] == lane_head[None, :]).astype(BF16)
    tri = (jnp.arange(TOKEN_BLOCK)[:, None] < jnp.arange(TOKEN_BLOCK)[None, :]).astype(BF16)
    biases = [_attn_bias(dil) for _, dil in DIL_PATTERNS]
    pad_e = LANES - N_EXPERTS

    for i in range(depth):
        j = i // n_mixers
        gain = mix_norm[i][None, :]
        w_r = jnp.pad(moe_w_router[i], ((0, 0), (0, pad_e)))
        w_hi = w_r.astype(BF16)
        w_split = jnp.concatenate([w_hi, (w_r - w_hi.astype(F32)).astype(BF16)], axis=1)
        b_r = jnp.pad(moe_b_router[i], (0, pad_e), constant_values=NEG_INF)[None, :]
        route = (ffn_norm[i][None, :], w_split, b_r)
        if i % n_mixers == 0:
            x, hn, aff = _gmlp_layer(
                x.reshape(bsz * seq, d), gain, gm_w_in[j].astype(BF16), gm_b_in[j][None, :],
                gm_v_norm[j][None, :], gm_w_s[j].astype(BF16), gm_b_s[j].T, gm_w_out[j].astype(BF16),
                gm_b_out[j][None, :], route)
        else:
            def per_head(g):
                return jnp.tile(g[:, None, :], (1, HEADS, 1)).reshape(N_GROUPS, ATTN_WIDTH)
            qk_gain = jnp.stack([per_head(at_q_norm[j]) * (HEAD_DIM ** -0.5), per_head(at_k_norm[j]),
                                 jnp.ones((N_GROUPS, ATTN_WIDTH), F32)]).reshape(1, -1)
            qkvs = _qkv_proj(x, gain, _group_major(at_w_qkv[j]).astype(BF16), _group_major(qk_gain), head_ones)
            outs = [_attn_group(g, qkvs[g], biases[g]) for g in range(N_GROUPS)]
            x, hn, aff = _attn_merge(outs, x, at_w_o[j].astype(BF16), route)
        x = _ec_moe(i, x.reshape(bsz, seq, d), hn.reshape(bsz, seq, d), aff.reshape(bsz, seq, LANES),
                    moe_w1, moe_w3, moe_w2, tri)
    return x
```

```python
import functools

import jax
import jax.numpy as jnp
from jax import lax
from jax.experimental import pallas as pl
from jax.experimental.pallas import tpu as pltpu

F32 = jnp.float32
BF16 = jnp.bfloat16
I32 = jnp.int32

EPS = 1e-6
NEG_INF = -1e30

LANES = 128
MXU_DIM = 256
VMEM_LIMIT_BYTES = 56 * 1024 * 1024

GMLP_CHUNK = 128
GMLP_GROUPS = 8
DIL_PATTERNS = ((128, 1), (512, 4), (2048, 16))
N_GROUPS = len(DIL_PATTERNS)
HEADS = 8
HEAD_DIM = 64
ATTN_WIDTH = HEADS * HEAD_DIM
GROUP_WIDTH = 3 * ATTN_WIDTH
QUERY_BLOCK = 128
HALF_WINDOW = 64
N_EXPERTS = 16
CAPACITY_FACTOR = 2

GMLP_ROWS = 512
QKV_ROWS = 512
MERGE_ROWS = 256
ATTN_BLOCKS = 4
TOKEN_BLOCK = MXU_DIM
COMBINE_BLOCKS = 2
PACK = 4
FIRST_WINDOW = MXU_DIM // PACK
GATHER_WINDOW = 128
SCATTER_WINDOW = MXU_DIM
LO_STRIDE = 32
EXPERT_CHUNK = 2 * MXU_DIM


def _params(*semantics):
    return pltpu.CompilerParams(dimension_semantics=semantics, vmem_limit_bytes=VMEM_LIMIT_BYTES)


def _const_spec(shape):
    zeros = (0,) * len(shape)
    return pl.BlockSpec(shape, lambda *_: zeros, pipeline_mode=pl.Buffered(1))


def _rms(x, gain):
    return x * lax.rsqrt(jnp.mean(x * x, axis=-1, keepdims=True) + EPS) * gain


def _dot(a, b):
    return jnp.dot(a, b, preferred_element_type=F32)


def _floor_to(x, multiple):
    shift = multiple.bit_length() - 1
    return lax.shift_left(lax.shift_right_logical(x, shift), shift)


def _ceil_div(x, divisor):
    return lax.shift_right_logical(x + (divisor - 1), divisor.bit_length() - 1)


def _route(x, g_ref, wr_ref, br_ref, hn_ref, aff_ref):
    hn = _rms(x, g_ref[...])
    hi = hn.astype(BF16)
    lo = (hn - hi.astype(F32)).astype(BF16)
    hn_ref[...] = hi
    both = _dot(hi, wr_ref[...])
    logits = both[:, :LANES] + both[:, LANES:] + _dot(lo, wr_ref[:, :LANES]) + br_ref[...]
    m = jnp.max(logits, axis=-1, keepdims=True)
    e = jnp.exp(logits - m)
    aff_ref[...] = e / jnp.sum(e, axis=-1, keepdims=True)


def _gmlp_body(x_ref, g_ref, win_ref, bin_ref, vg_ref, ws_ref, bst_ref, wout_ref, bout_ref,
               rg_ref, wr_ref, br_ref, o_ref, hn_ref, aff_ref, v_s, uv_s):
    x = x_ref[...]
    rows, width = v_s.shape
    gw = width // GMLP_GROUPS
    h = _rms(x, g_ref[...]).astype(BF16)
    zv = jax.nn.gelu(_dot(h, win_ref[:, width:]) + bin_ref[:, width:])
    r = lax.rsqrt(jnp.mean(zv * zv, axis=-1, keepdims=True) + EPS)
    v_s[...] = (zv * r * vg_ref[...]).astype(BF16)
    pair = 2 * gw
    for p in range(GMLP_GROUPS // 2):
        zu = jax.nn.gelu(_dot(h, win_ref[:, p * pair:(p + 1) * pair]) + bin_ref[:, p * pair:(p + 1) * pair])
        for gl in range(2):
            g = 2 * p + gl
            cols = slice(g * gw, (g + 1) * gw)
            bias = bst_ref[:, g:g + 1]
            for c in range(rows // GMLP_CHUNK):
                rs = slice(c * GMLP_CHUNK, (c + 1) * GMLP_CHUNK)
                sv = _dot(ws_ref[g], v_s[rs, cols]) + bias
                uv_s[rs, cols] = (zu[rs, gl * gw:(gl + 1) * gw] * sv).astype(BF16)
    out = x + _dot(uv_s[...], wout_ref[...]) + bout_ref[...]
    o_ref[...] = out
    _route(out, rg_ref, wr_ref, br_ref, hn_ref, aff_ref)


def _gmlp_layer(x2, gain, w_in, b_in, v_gain, w_s, b_st, w_out, b_out, route):
    n, d = x2.shape
    width = w_out.shape[0]
    rows = GMLP_ROWS
    return pl.pallas_call(
        _gmlp_body,
        out_shape=[jax.ShapeDtypeStruct((n, d), F32), jax.ShapeDtypeStruct((n, d), BF16),
                   jax.ShapeDtypeStruct((n, LANES), F32)],
        grid=(n // rows,),
        in_specs=[
            pl.BlockSpec((rows, d), lambda i: (i, 0)),
            _const_spec(gain.shape), _const_spec(w_in.shape), _const_spec(b_in.shape),
            _const_spec(v_gain.shape), _const_spec(w_s.shape), _const_spec(b_st.shape),
            _const_spec(w_out.shape), _const_spec(b_out.shape),
        ] + [_const_spec(a.shape) for a in route],
        out_specs=[pl.BlockSpec((rows, d), lambda i: (i, 0)), pl.BlockSpec((rows, d), lambda i: (i, 0)),
                   pl.BlockSpec((rows, LANES), lambda i: (i, 0))],
        scratch_shapes=[pltpu.VMEM((rows, width), BF16), pltpu.VMEM((rows, width), BF16)],
        compiler_params=_params("parallel"),
        name="gmlp_layer",
    )(x2, gain, w_in, b_in, v_gain, w_s, b_st, w_out, b_out, *route)


def _qkv_body(x_ref, g_ref, w_ref, gain_ref, bd_ref, *rest):
    out_refs, t_s = rest[:N_GROUPS], rest[N_GROUPS]
    rows = t_s.shape[1]
    h = _rms(x_ref[...], g_ref[...]).astype(BF16)
    bd = bd_ref[...]

    def put(col0, t):
        for c in range(t.shape[1] // LANES):
            t_s[col0 // LANES + c] = t[:, c * LANES:(c + 1) * LANES]

    for g in range(N_GROUPS):
        tg = _dot(h, w_ref[:, g * GROUP_WIDTH:(g + 1) * GROUP_WIDTH])
        for c in range(2 * ATTN_WIDTH // MXU_DIM):
            col0 = g * GROUP_WIDTH + c * MXU_DIM
            t = tg[:, c * MXU_DIM:(c + 1) * MXU_DIM]
            t2 = t * t
            hi = t2.astype(BF16)
            lo = (t2 - hi.astype(F32)).astype(BF16)
            ss = _dot(hi, bd) + _dot(lo, bd)
            put(col0, t * lax.rsqrt(ss * (1.0 / HEAD_DIM) + EPS) * gain_ref[:, col0:col0 + MXU_DIM])
        put(g * GROUP_WIDTH + 2 * ATTN_WIDTH, tg[:, 2 * ATTN_WIDTH:])
    for g, (_, dil) in enumerate(DIL_PATTERNS):
        for r in range(dil):
            rs = slice(None) if dil == 1 else pl.ds(r, rows // dil, stride=dil)
            for c in range(GROUP_WIDTH // LANES):
                out_refs[g][0, r, :, c * LANES:(c + 1) * LANES] = (
                    t_s[g * (GROUP_WIDTH // LANES) + c, rs, :].astype(BF16))


def _qkv_proj(x, gain, w_qkv, qk_gain, head_ones):
    bsz, seq, d = x.shape
    width = w_qkv.shape[1]
    rows = QKV_ROWS
    out_shape, out_specs = [], []
    for _, dil in DIL_PATTERNS:
        out_shape.append(jax.ShapeDtypeStruct((bsz, dil, seq // dil, GROUP_WIDTH), BF16))
        out_specs.append(pl.BlockSpec((1, dil, rows // dil, GROUP_WIDTH), lambda b, i: (b, 0, i, 0)))
    return pl.pallas_call(
        _qkv_body,
        out_shape=out_shape,
        grid=(bsz, seq // rows),
        in_specs=[
            pl.BlockSpec((None, rows, d), lambda b, i: (b, i, 0)),
            _const_spec(gain.shape), _const_spec(w_qkv.shape), _const_spec(qk_gain.shape),
            _const_spec(head_ones.shape),
        ],
        out_specs=out_specs,
        scratch_shapes=[pltpu.VMEM((width // LANES, rows, LANES), F32)],
        compiler_params=_params("parallel", "parallel"),
        name="qkv_proj",
    )(x, gain, w_qkv, qk_gain, head_ones)


def _attn_body(q_ref, kp_ref, kc_ref, kn_ref, vp_ref, vc_ref, vn_ref, bias_ref, o_ref, l_ref):
    hw = HALF_WINDOW
    qb = QUERY_BLOCK
    n_sub = q_ref.shape[2] // qb
    last_block = pl.num_programs(2) * n_sub - 1
    heads_per_tile = MXU_DIM // HEAD_DIM
    head_of_lane = lax.broadcasted_iota(I32, (qb, MXU_DIM), 1) // HEAD_DIM
    for rr in range(q_ref.shape[1]):
        k_all = jnp.concatenate([kp_ref[0, rr, qb - hw:, :], kc_ref[0, rr], kn_ref[0, rr, :hw, :]], axis=0)
        v_all = jnp.concatenate([vp_ref[0, rr, qb - hw:, :], vc_ref[0, rr], vn_ref[0, rr, :hw, :]], axis=0)
        for sub in range(n_sub):
            block = pl.program_id(2) * n_sub + sub
            variant = jnp.where(block == 0, 0, jnp.where(block == last_block, 2, 1))
            q = q_ref[0, rr, sub * qb:(sub + 1) * qb, :]
            k = k_all[sub * qb:sub * qb + qb + 2 * hw]
            v = v_all[sub * qb:sub * qb + qb + 2 * hw]
            for half in range(ATTN_WIDTH // MXU_DIM):
                cs = slice(half * MXU_DIM, (half + 1) * MXU_DIM)
                qh, kh, vh = q[:, cs], k[:, cs], v[:, cs]
                qs = jnp.concatenate(
                    [jnp.where(head_of_lane == hh, qh, jnp.zeros_like(qh)) for hh in range(heads_per_tile)], axis=0)
                s = lax.dot_general(qs, kh, (((1,), (1,)), ((), ())), preferred_element_type=F32)
                s = s + bias_ref[variant, half]
                m = jnp.max(s, axis=-1, keepdims=True)
                p = jnp.exp(s - m)
                l = jnp.sum(p, axis=-1, keepdims=True)
                pv = _dot(p.astype(BF16), vh) / l
                lse = m + jnp.log(l)
                o_half = jnp.zeros((qb, MXU_DIM), F32)
                l_half = jnp.zeros((qb, MXU_DIM), F32)
                for hh in range(heads_per_tile):
                    rs = slice(hh * qb, (hh + 1) * qb)
                    sel = head_of_lane == hh
                    o_half = jnp.where(sel, pv[rs], o_half)
                    l_half = jnp.where(sel, lse[rs], l_half)
                o_ref[0, rr, sub * qb:(sub + 1) * qb, cs] = o_half.astype(BF16)
                l_ref[0, rr, sub * qb:(sub + 1) * qb, cs] = l_half


def _attn_group(g, qkv_g, bias):
    bsz, dil, sub_len, _ = qkv_g.shape
    nb = sub_len // QUERY_BLOCK
    per_step = min(ATTN_BLOCKS, nb)
    rows = per_step * QUERY_BLOCK
    res = min(ATTN_BLOCKS // per_step, dil)

    def main(which):
        return pl.BlockSpec((1, res, rows, ATTN_WIDTH), lambda b, r, i: (b, r, i, which))

    def halo(which, offset):
        def index(b, r, i):
            return (b, r, jnp.clip(i * per_step + offset, 0, nb - 1), which)
        return pl.BlockSpec((1, res, QUERY_BLOCK, ATTN_WIDTH), index)

    return pl.pallas_call(
        _attn_body,
        out_shape=[jax.ShapeDtypeStruct((bsz, dil, sub_len, ATTN_WIDTH), BF16),
                   jax.ShapeDtypeStruct((bsz, dil, sub_len, ATTN_WIDTH), F32)],
        grid=(bsz, dil // res, nb // per_step),
        in_specs=[main(0), halo(1, -1), main(1), halo(1, per_step), halo(2, -1), main(2), halo(2, per_step),
                  _const_spec(bias.shape)],
        out_specs=[main(0), main(0)],
        compiler_params=_params("parallel", "parallel", "arbitrary"),
        name=f"dilated_attn_g{g}",
    )(*([qkv_g] * 7), bias)


def _attn_bias(dil):
    slopes = jnp.exp2(-8.0 * jnp.arange(1, HEADS + 1, dtype=F32) / HEADS)
    kb = QUERY_BLOCK + 2 * HALF_WINDOW
    rel = jnp.arange(kb)[None, :] - HALF_WINDOW - jnp.arange(QUERY_BLOCK)[:, None]
    band = jnp.abs(rel) <= HALF_WINDOW
    alibi = -slopes[:, None, None] * (dil * jnp.abs(rel)).astype(F32)[None]
    j = jnp.arange(kb)
    edge = jnp.stack([j >= HALF_WINDOW, j >= 0, j < kb - HALF_WINDOW])
    ok = band[None, None] & edge[:, None, None, :]
    tab = jnp.where(ok, alibi[None], NEG_INF)
    per_tile = MXU_DIM // HEAD_DIM
    return tab.reshape(3, HEADS // per_tile, per_tile * QUERY_BLOCK, kb)


def _merge_body(*refs):
    o_refs, l_refs = refs[:N_GROUPS], refs[N_GROUPS:2 * N_GROUPS]
    x_ref, wo_ref, rg_ref, wr_ref, br_ref, out_ref, hn_ref, aff_ref, o_s, l_s = refs[2 * N_GROUPS:]
    rows = x_ref.shape[0]
    tiles = ATTN_WIDTH // LANES
    for g, (_, dil) in enumerate(DIL_PATTERNS):
        for r in range(dil):
            rs = slice(None) if dil == 1 else pl.ds(r, rows // dil, stride=dil)
            for c in range(tiles):
                cs = slice(c * LANES, (c + 1) * LANES)
                o_s[g * tiles + c, rs, :] = o_refs[g][0, r, :, cs].astype(F32)
                l_s[g * tiles + c, rs, :] = l_refs[g][0, r, :, cs]
    merged = []
    for c in range(tiles):
        lses = [l_s[g * tiles + c] for g in range(N_GROUPS)]
        mx = functools.reduce(jnp.maximum, lses)
        num = jnp.zeros_like(mx)
        den = jnp.zeros_like(mx)
        for g in range(N_GROUPS):
            e = jnp.exp(lses[g] - mx)
            num = num + e * o_s[g * tiles + c]
            den = den + e
        merged.append((num / den).astype(BF16))
    out = x_ref[...] + _dot(jnp.concatenate(merged, axis=1), wo_ref[...])
    out_ref[...] = out
    _route(out, rg_ref, wr_ref, br_ref, hn_ref, aff_ref)


def _attn_merge(outs, x, w_o, route):
    bsz, seq, d = x.shape
    rows = MERGE_ROWS
    args, in_specs = [], []
    for which in range(2):
        for g, (_, dil) in enumerate(DIL_PATTERNS):
            args.append(outs[g][which])
            in_specs.append(pl.BlockSpec((1, dil, rows // dil, ATTN_WIDTH), lambda b, i: (b, 0, i, 0)))
    args += [x, w_o, *route]
    in_specs += [pl.BlockSpec((None, rows, d), lambda b, i: (b, i, 0)), _const_spec(w_o.shape)]
    in_specs += [_const_spec(a.shape) for a in route]

    def row_block(width):
        return pl.BlockSpec((None, rows, width), lambda b, i: (b, i, 0))

    return pl.pallas_call(
        _merge_body,
        out_shape=[jax.ShapeDtypeStruct((bsz, seq, d), F32), jax.ShapeDtypeStruct((bsz, seq, d), BF16),
                   jax.ShapeDtypeStruct((bsz, seq, LANES), F32)],
        grid=(bsz, seq // rows),
        in_specs=in_specs,
        out_specs=[row_block(d), row_block(d), row_block(LANES)],
        scratch_shapes=[pltpu.VMEM((N_GROUPS * ATTN_WIDTH // LANES, rows, LANES), F32),
                        pltpu.VMEM((N_GROUPS * ATTN_WIDTH // LANES, rows, LANES), F32)],
        compiler_params=_params("parallel", "parallel"),
        name="attn_merge",
    )(*args)


def _select_body(cap, aff_ref, tri_ref, post_ref, gatet_ref, pos_ref, lo_ref):
    seq = aff_ref.shape[1]
    aff_t = aff_ref[0].T[:N_EXPERTS]
    key = pltpu.bitcast(aff_t, I32)
    thr = jnp.zeros((N_EXPERTS, 1), I32)
    for bit in range(30, -1, -1):
        cand = thr | (1 << bit)
        cnt = jnp.sum(jnp.where(key >= cand, 1.0, 0.0), axis=1, keepdims=True)
        thr = jnp.where(cnt >= cap, cand, thr)
    gt = key > thr
    eq = key == thr
    need = cap - jnp.sum(jnp.where(gt, 1.0, 0.0), axis=1, keepdims=True)
    tri = tri_ref[...]
    nblk = seq // TOKEN_BLOCK

    def prefix(mask_f, want_offsets):
        off = jnp.zeros((N_EXPERTS, 1), F32)
        parts, offs = [], []
        for jb in range(nblk):
            blk = mask_f[:, jb * TOKEN_BLOCK:(jb + 1) * TOKEN_BLOCK]
            offs.append(off)
            parts.append(_dot(blk.astype(BF16), tri) + off)
            off = off + jnp.sum(blk, axis=1, keepdims=True)
        offs.append(off)
        return jnp.concatenate(parts, axis=1), (offs if want_offsets else None)

    eq_rank, _ = prefix(jnp.where(eq, 1.0, 0.0), False)
    sel = gt | (eq & (eq_rank < need))
    rank, offs = prefix(jnp.where(sel, 1.0, 0.0), True)
    post = jnp.where(sel, rank, -1.0)
    post_ref[0] = post.astype(I32)
    gatet_ref[0] = jnp.where(sel, aff_t, 0.0)
    padded = jnp.concatenate([post, jnp.full((LANES - N_EXPERTS, seq), -1.0, F32)], axis=0)
    pos_ref[0] = padded.T.astype(I32)
    lane = lax.broadcasted_iota(I32, (N_EXPERTS, LANES), 1)
    lo = jnp.zeros((N_EXPERTS, LANES), F32)
    for jb, off in enumerate(offs):
        lo = jnp.where(lane == jb, off, lo)
    lo_ref[0] = lo.astype(I32)


def _select(aff, tri, cap):
    bsz, seq, _ = aff.shape
    return pl.pallas_call(
        functools.partial(_select_body, cap),
        out_shape=[jax.ShapeDtypeStruct((bsz, N_EXPERTS, seq), I32),
                   jax.ShapeDtypeStruct((bsz, N_EXPERTS, seq), F32),
                   jax.ShapeDtypeStruct((bsz, seq, LANES), I32),
                   jax.ShapeDtypeStruct((bsz, N_EXPERTS, LANES), I32)],
        grid=(bsz,),
        in_specs=[pl.BlockSpec((1, seq, LANES), lambda b: (b, 0, 0)), _const_spec(tri.shape)],
        out_specs=[pl.BlockSpec((1, N_EXPERTS, seq), lambda b: (b, 0, 0)),
                   pl.BlockSpec((1, N_EXPERTS, seq), lambda b: (b, 0, 0)),
                   pl.BlockSpec((1, seq, LANES), lambda b: (b, 0, 0)),
                   pl.BlockSpec((1, N_EXPERTS, LANES), lambda b: (b, 0, 0))],
        compiler_params=_params("parallel"),
        name="moe_select",
    )(aff, tri)


def _gather_body(lo_ref, hn_ref, post_ref, gatet_ref, xin_ref, gate_ref):
    b = pl.program_id(0)
    eg = pl.program_id(1)
    seq = hn_ref.shape[1]
    cap = xin_ref.shape[2]
    nblk = seq // TOKEN_BLOCK
    xin_ref[...] = jnp.zeros_like(xin_ref)
    gate_ref[...] = jnp.zeros_like(gate_ref)

    def table(i):
        return (b * N_EXPERTS + eg * PACK + i) * LO_STRIDE

    def window(i, ts, s0, rows, iota):
        s0c = pl.multiple_of(jnp.minimum(s0, cap - rows), 16)
        slot = iota + s0c
        hit = (post_ref[0, i, :, ts] == slot) & (slot >= s0)

        def add(contrib):
            xin_ref[0, i, pl.ds(s0c, rows), :] += contrib.astype(BF16)
            gsum = jnp.sum(jnp.where(hit, gatet_ref[0, i, :, ts], 0.0), axis=1, keepdims=True)
            gate_ref[0, i, pl.ds(s0c, rows), :] += jnp.broadcast_to(gsum, (rows, LANES))

        return jnp.where(hit, 1.0, 0.0).astype(BF16), add

    first_iota = lax.broadcasted_iota(I32, (FIRST_WINDOW, TOKEN_BLOCK), 0)
    for j in range(nblk):
        ts = slice(j * TOKEN_BLOCK, (j + 1) * TOKEN_BLOCK)
        rows, adds = zip(*[window(i, ts, _floor_to(lo_ref[table(i) + j], 16), FIRST_WINDOW, first_iota)
                           for i in range(PACK)])
        contrib = _dot(jnp.concatenate(rows, axis=0), hn_ref[0, ts, :])
        for i in range(PACK):
            adds[i](contrib[i * FIRST_WINDOW:(i + 1) * FIRST_WINDOW])

    def left_over(i, j):
        done = _floor_to(lo_ref[table(i) + j], 16) + FIRST_WINDOW
        return jnp.maximum(lo_ref[table(i) + j + 1] - done, 0), done

    most = functools.reduce(jnp.maximum, [left_over(i, j)[0] for i in range(PACK) for j in range(nblk)])

    @pl.when(most > 0)
    def _():
        more_iota = lax.broadcasted_iota(I32, (GATHER_WINDOW, TOKEN_BLOCK), 0)
        for j in range(nblk):
            ts = slice(j * TOKEN_BLOCK, (j + 1) * TOKEN_BLOCK)

            def expert(i, carry, j=j, ts=ts):
                left, done = left_over(i, j)

                def further(w, carry):
                    onehot, add = window(i, ts, done + w * GATHER_WINDOW, GATHER_WINDOW, more_iota)
                    add(_dot(onehot, hn_ref[0, ts, :]))
                    return carry

                return lax.fori_loop(0, _ceil_div(left, GATHER_WINDOW), further, carry)

            lax.fori_loop(0, PACK, expert, 0)


def _gather(lo_flat, hn, post, gatet, cap):
    bsz, seq, d = hn.shape
    post4 = post.reshape(bsz, N_EXPERTS, 1, seq)
    gatet4 = gatet.reshape(bsz, N_EXPERTS, 1, seq)
    grid_spec = pltpu.PrefetchScalarGridSpec(
        num_scalar_prefetch=1,
        grid=(bsz, N_EXPERTS // PACK),
        in_specs=[pl.BlockSpec((1, seq, d), lambda b, e, lo: (b, 0, 0)),
                  pl.BlockSpec((1, PACK, 1, seq), lambda b, e, lo: (b, e, 0, 0)),
                  pl.BlockSpec((1, PACK, 1, seq), lambda b, e, lo: (b, e, 0, 0))],
        out_specs=[pl.BlockSpec((1, PACK, cap, d), lambda b, e, lo: (b, e, 0, 0)),
                   pl.BlockSpec((1, PACK, cap, LANES), lambda b, e, lo: (b, e, 0, 0))],
    )
    return pl.pallas_call(
        _gather_body,
        out_shape=[jax.ShapeDtypeStruct((bsz, N_EXPERTS, cap, d), BF16),
                   jax.ShapeDtypeStruct((bsz, N_EXPERTS, cap, LANES), F32)],
        grid_spec=grid_spec,
        compiler_params=_params("parallel", "arbitrary"),
        name="moe_gather",
    )(lo_flat, hn, post4, gatet4)


def _expert_body(xin_ref, gate_ref, w1c_ref, w3c_ref, w2c_ref, y_ref, w1_s, w3_s, w2_s):
    r = pl.program_id(0)
    b = pl.program_id(1)
    fill = lax.rem(r, 2)

    def cast_slice():
        d_rows = w1c_ref.shape[2]
        f_rows = w2c_ref.shape[2]
        d0 = pl.multiple_of(b * d_rows, d_rows)
        f0 = pl.multiple_of(b * f_rows, f_rows)
        w1_s[fill, pl.ds(d0, d_rows), :] = w1c_ref[0, 0].astype(BF16)
        w3_s[fill, pl.ds(d0, d_rows), :] = w3c_ref[0, 0].astype(BF16)
        w2_s[fill, pl.ds(f0, f_rows), :] = w2c_ref[0, 0].astype(BF16)

    @pl.when(r == 0)
    def _():
        cast_slice()
        y_ref[0, 0] = jnp.zeros(y_ref.shape[2:], BF16)

    @pl.when(r > 0)
    def _():
        cast_slice()
        use = 1 - fill
        xin = xin_ref[0, 0]
        chunk = EXPERT_CHUNK
        y = jnp.zeros(y_ref.shape[2:], F32)
        for c in range(w1_s.shape[2] // chunk):
            fs = slice(c * chunk, (c + 1) * chunk)
            hid = jax.nn.silu(_dot(xin, w1_s[use, :, fs])) * _dot(xin, w3_s[use, :, fs])
            y = y + _dot(hid.astype(BF16), w2_s[use, fs, :])
        y_ref[0, 0] = (y * gate_ref[0, 0, :, 0:1]).astype(BF16)


def _experts(layer, xin, gate, w1, w3, w2):
    bsz, ne, cap, d = xin.shape
    f = w1.shape[3]
    d_rows, f_rows = d // bsz, f // bsz
    assert d_rows * bsz == d and f_rows * bsz == f

    def act(width):
        return pl.BlockSpec((1, 1, cap, width), lambda r, b: (b, jnp.maximum(r - 1, 0), 0, 0))

    def weight_slice(rows, cols):
        return pl.BlockSpec((1, 1, rows, cols), lambda r, b: (layer, jnp.minimum(r, ne - 1), b, 0))

    return pl.pallas_call(
        _expert_body,
        out_shape=jax.ShapeDtypeStruct((bsz, ne, cap, d), BF16),
        grid=(ne + 1, bsz),
        in_specs=[act(d), act(LANES), weight_slice(d_rows, f), weight_slice(d_rows, f), weight_slice(f_rows, d)],
        out_specs=act(d),
        scratch_shapes=[pltpu.VMEM((2, d, f), BF16), pltpu.VMEM((2, d, f), BF16), pltpu.VMEM((2, f, d), BF16)],
        compiler_params=_params("arbitrary", "arbitrary"),
        name="moe_experts",
    )(xin, gate, w1, w3, w2)


def _combine_body(lo_ref, x_ref, y_ref, pos_ref, o_ref):
    b = pl.program_id(0)
    cap = y_ref.shape[2]
    lane = lax.broadcasted_iota(I32, (TOKEN_BLOCK, PACK * FIRST_WINDOW), 1)
    lane_expert = lane // FIRST_WINDOW
    lane_slot = lane % FIRST_WINDOW
    pending = []
    for sub in range(COMBINE_BLOCKS):
        j = pl.program_id(1) * COMBINE_BLOCKS + sub
        ts = slice(sub * TOKEN_BLOCK, (sub + 1) * TOKEN_BLOCK)
        pos = pos_ref[0, ts, :]

        def bounds(e, j=j):
            base = (b * N_EXPERTS + e) * LO_STRIDE
            return lo_ref[base + j], lo_ref[base + j + 1]

        acc = x_ref[0, ts, :]
        for q in range(N_EXPERTS // PACK):
            target = jnp.full(lane.shape, -1, I32)
            y_rows = []
            for i in range(PACK):
                e = q * PACK + i
                start = _floor_to(bounds(e)[0], 16)
                s0c = pl.multiple_of(jnp.minimum(start, cap - FIRST_WINDOW), 16)
                pcol = pos[:, e:e + 1]
                rel = jnp.where(pcol >= start, pcol - s0c, -1)
                target = jnp.where(lane_expert == i, rel, target)
                y_rows.append(y_ref[0, e, pl.ds(s0c, FIRST_WINDOW), :])
            onehot = jnp.where(target == lane_slot, 1.0, 0.0).astype(BF16)
            acc = acc + _dot(onehot, jnp.concatenate(y_rows, axis=0))
        o_ref[0, ts, :] = acc

        def left_over(e, bounds=bounds):
            lo, hi = bounds(e)
            done = _floor_to(lo, 16) + FIRST_WINDOW
            return jnp.maximum(hi - done, 0), done

        pending.append((ts, pos, left_over))

    for ts, pos, left_over in pending:
        most = functools.reduce(jnp.maximum, [left_over(e)[0] for e in range(N_EXPERTS)])

        @pl.when(most > 0)
        def _(ts=ts, pos=pos, left_over=left_over):
            more_lane = lax.broadcasted_iota(I32, (TOKEN_BLOCK, SCATTER_WINDOW), 1)
            for e in range(N_EXPERTS):
                left, done = left_over(e)
                pcol = pos[:, e:e + 1]

                def window(w, carry, e=e, done=done, pcol=pcol):
                    s0 = done + w * SCATTER_WINDOW
                    s0c = pl.multiple_of(jnp.minimum(s0, cap - SCATTER_WINDOW), 16)
                    hit = ((pcol - s0c) == more_lane) & (pcol >= s0)
                    onehot = jnp.where(hit, 1.0, 0.0).astype(BF16)
                    o_ref[0, ts, :] += _dot(onehot, y_ref[0, e, pl.ds(s0c, SCATTER_WINDOW), :])
                    return carry

                lax.fori_loop(0, _ceil_div(left, SCATTER_WINDOW), window, 0)


def _combine(lo_flat, x, y, pos):
    bsz, seq, d = x.shape
    _, ne, cap, _ = y.shape
    rows = COMBINE_BLOCKS * TOKEN_BLOCK
    grid_spec = pltpu.PrefetchScalarGridSpec(
        num_scalar_prefetch=1,
        grid=(bsz, seq // rows),
        in_specs=[pl.BlockSpec((1, rows, d), lambda b, j, lo: (b, j, 0)),
                  pl.BlockSpec((1, ne, cap, d), lambda b, j, lo: (b, 0, 0, 0)),
                  pl.BlockSpec((1, rows, LANES), lambda b, j, lo: (b, j, 0))],
        out_specs=pl.BlockSpec((1, rows, d), lambda b, j, lo: (b, j, 0)),
    )
    return pl.pallas_call(
        _combine_body,
        out_shape=jax.ShapeDtypeStruct((bsz, seq, d), F32),
        grid_spec=grid_spec,
        compiler_params=_params("parallel", "arbitrary"),
        name="moe_combine",
    )(lo_flat, x, y, pos)


def _ec_moe(layer, x, hn, aff, w1, w3, w2, tri):
    bsz, seq, d = x.shape
    cap = CAPACITY_FACTOR * seq // N_EXPERTS
    post, gatet, pos, lo = _select(aff, tri, cap)
    lo_flat = lo[:, :, :LO_STRIDE].reshape(-1)
    xin, gate = _gather(lo_flat, hn, post, gatet, cap)
    y = _experts(layer, xin, gate, w1, w3, w2)
    return _combine(lo_flat, x, y, pos)


def _group_major(a):
    lead = a.shape[:-1]
    a = a.reshape(lead + (3, N_GROUPS, ATTN_WIDTH))
    return jnp.swapaxes(a, -3, -2).reshape(lead + (3 * N_GROUPS * ATTN_WIDTH,))


def kernel(x, mix_norm, ffn_norm, gm_w_in, gm_b_in, gm_v_norm, gm_w_s, gm_b_s, gm_w_out, gm_b_out,
           at_w_qkv, at_q_norm, at_k_norm, at_w_o, moe_w_router, moe_b_router, moe_w1, moe_w3, moe_w2):
    bsz, seq, d = x.shape
    depth = mix_norm.shape[0]
    n_mixers = 2

    lane_head = jnp.arange(MXU_DIM) // HEAD_DIM
    head_ones = (lane_head[:, None] == lane_head[None, :]).astype(BF16)
    tri = (jnp.arange(TOKEN_BLOCK)[:, None] < jnp.arange(TOKEN_BLOCK)[None, :]).astype(BF16)
    biases = [_attn_bias(dil) for _, dil in DIL_PATTERNS]
    pad_e = LANES - N_EXPERTS

    for i in range(depth):
        j = i // n_mixers
        gain = mix_norm[i][None, :]
        w_r = jnp.pad(moe_w_router[i], ((0, 0), (0, pad_e)))
        w_hi = w_r.astype(BF16)
        w_split = jnp.concatenate([w_hi, (w_r - w_hi.astype(F32)).astype(BF16)], axis=1)
        b_r = jnp.pad(moe_b_router[i], (0, pad_e), constant_values=NEG_INF)[None, :]
        route = (ffn_norm[i][None, :], w_split, b_r)
        if i % n_mixers == 0:
            x, hn, aff = _gmlp_layer(
                x.reshape(bsz * seq, d), gain, gm_w_in[j].astype(BF16), gm_b_in[j][None, :],
                gm_v_norm[j][None, :], gm_w_s[j].astype(BF16), gm_b_s[j].T, gm_w_out[j].astype(BF16),
                gm_b_out[j][None, :], route)
        else:
            def per_head(g):
                return jnp.tile(g[:, None, :], (1, HEADS, 1)).reshape(N_GROUPS, ATTN_WIDTH)
            qk_gain = jnp.stack([per_head(at_q_norm[j]) * (HEAD_DIM ** -0.5), per_head(at_k_norm[j]),
                                 jnp.ones((N_GROUPS, ATTN_WIDTH), F32)]).reshape(1, -1)
            qkvs = _qkv_proj(x, gain, _group_major(at_w_qkv[j]).astype(BF16), _group_major(qk_gain), head_ones)
            outs = [_attn_group(g, qkvs[g], biases[g]) for g in range(N_GROUPS)]
            x, hn, aff = _attn_merge(outs, x, at_w_o[j].astype(BF16), route)
        x = _ec_moe(i, x.reshape(bsz, seq, d), hn.reshape(bsz, seq, d), aff.reshape(bsz, seq, LANES),
                    moe_w1, moe_w3, moe_w2, tri)
    return x
```

```python
import functools

import jax
import jax.numpy as jnp
from jax import lax
from jax.experimental import pallas as pl
from jax.experimental.pallas import tpu as pltpu

F32 = jnp.float32
BF16 = jnp.bfloat16
I32 = jnp.int32

EPS = 1e-6
NEG_INF = -1e30

LANES = 128
MXU_DIM = 256
VMEM_LIMIT_BYTES = 56 * 1024 * 1024

GMLP_CHUNK = 128
GMLP_GROUPS = 8
DIL_PATTERNS = ((128, 1), (512, 4), (2048, 16))
N_GROUPS = len(DIL_PATTERNS)
HEADS = 8
HEAD_DIM = 64
ATTN_WIDTH = HEADS * HEAD_DIM
GROUP_WIDTH = 3 * ATTN_WIDTH
QUERY_BLOCK = 128
HALF_WINDOW = 64
N_EXPERTS = 16
CAPACITY_FACTOR = 2

GMLP_ROWS = 512
QKV_ROWS = 512
MERGE_ROWS = 512
ATTN_BLOCKS = 4
TOKEN_BLOCK = MXU_DIM
SELECT_SEQS = 2
COMBINE_BLOCKS = 2
PACK = 4
FIRST_WINDOW = MXU_DIM // PACK
GATHER_WINDOW = 128
SCATTER_WINDOW = MXU_DIM
LO_STRIDE = 32
EXPERT_CHUNK = 2 * MXU_DIM


def _params(*semantics):
    return pltpu.CompilerParams(dimension_semantics=semantics, vmem_limit_bytes=VMEM_LIMIT_BYTES)


def _const_spec(shape):
    zeros = (0,) * len(shape)
    return pl.BlockSpec(shape, lambda *_: zeros, pipeline_mode=pl.Buffered(1))


def _rms(x, gain):
    return x * lax.rsqrt(jnp.mean(x * x, axis=-1, keepdims=True) + EPS) * gain


def _dot(a, b):
    return jnp.dot(a, b, preferred_element_type=F32)


def _floor_to(x, multiple):
    shift = multiple.bit_length() - 1
    return lax.shift_left(lax.shift_right_logical(x, shift), shift)


def _ceil_div(x, divisor):
    return lax.shift_right_logical(x + (divisor - 1), divisor.bit_length() - 1)


def _route(x, g_ref, wr_ref, br_ref, hn_ref, aff_ref):
    hn = _rms(x, g_ref[...])
    hi = hn.astype(BF16)
    lo = (hn - hi.astype(F32)).astype(BF16)
    hn_ref[...] = hi
    both = _dot(hi, wr_ref[...])
    logits = both[:, :LANES] + both[:, LANES:] + _dot(lo, wr_ref[:, :LANES]) + br_ref[...]
    m = jnp.max(logits, axis=-1, keepdims=True)
    e = jnp.exp(logits - m)
    aff_ref[...] = e / jnp.sum(e, axis=-1, keepdims=True)


def _gmlp_body(x_ref, g_ref, win_ref, bin_ref, vg_ref, ws_ref, bst_ref, wout_ref, bout_ref,
               rg_ref, wr_ref, br_ref, o_ref, hn_ref, aff_ref, v_s, uv_s):
    x = x_ref[...]
    rows, width = v_s.shape
    gw = width // GMLP_GROUPS
    h = _rms(x, g_ref[...]).astype(BF16)
    zv = jax.nn.gelu(_dot(h, win_ref[:, width:]) + bin_ref[:, width:])
    r = lax.rsqrt(jnp.mean(zv * zv, axis=-1, keepdims=True) + EPS)
    v_s[...] = (zv * r * vg_ref[...]).astype(BF16)
    pair = 2 * gw
    for p in range(GMLP_GROUPS // 2):
        zu = jax.nn.gelu(_dot(h, win_ref[:, p * pair:(p + 1) * pair]) + bin_ref[:, p * pair:(p + 1) * pair])
        for gl in range(2):
            g = 2 * p + gl
            cols = slice(g * gw, (g + 1) * gw)
            bias = bst_ref[:, g:g + 1]
            for c in range(rows // GMLP_CHUNK):
                rs = slice(c * GMLP_CHUNK, (c + 1) * GMLP_CHUNK)
                sv = _dot(ws_ref[g], v_s[rs, cols]) + bias
                uv_s[rs, cols] = (zu[rs, gl * gw:(gl + 1) * gw] * sv).astype(BF16)
    out = x + _dot(uv_s[...], wout_ref[...]) + bout_ref[...]
    o_ref[...] = out
    _route(out, rg_ref, wr_ref, br_ref, hn_ref, aff_ref)


def _gmlp_layer(x2, gain, w_in, b_in, v_gain, w_s, b_st, w_out, b_out, route):
    n, d = x2.shape
    width = w_out.shape[0]
    rows = GMLP_ROWS
    return pl.pallas_call(
        _gmlp_body,
        out_shape=[jax.ShapeDtypeStruct((n, d), F32), jax.ShapeDtypeStruct((n, d), BF16),
                   jax.ShapeDtypeStruct((n, LANES), F32)],
        grid=(n // rows,),
        in_specs=[
            pl.BlockSpec((rows, d), lambda i: (i, 0)),
            _const_spec(gain.shape), _const_spec(w_in.shape), _const_spec(b_in.shape),
            _const_spec(v_gain.shape), _const_spec(w_s.shape), _const_spec(b_st.shape),
            _const_spec(w_out.shape), _const_spec(b_out.shape),
        ] + [_const_spec(a.shape) for a in route],
        out_specs=[pl.BlockSpec((rows, d), lambda i: (i, 0)), pl.BlockSpec((rows, d), lambda i: (i, 0)),
                   pl.BlockSpec((rows, LANES), lambda i: (i, 0))],
        scratch_shapes=[pltpu.VMEM((rows, width), BF16), pltpu.VMEM((rows, width), BF16)],
        compiler_params=_params("parallel"),
        name="gmlp_layer",
    )(x2, gain, w_in, b_in, v_gain, w_s, b_st, w_out, b_out, *route)


def _qkv_body(x_ref, g_ref, w_ref, gain_ref, bd_ref, *rest):
    out_refs, t_s = rest[:N_GROUPS], rest[N_GROUPS]
    rows = t_s.shape[1]
    h = _rms(x_ref[...], g_ref[...]).astype(BF16)
    bd = bd_ref[...]

    def put(col0, t):
        for c in range(t.shape[1] // LANES):
            t_s[col0 // LANES + c] = t[:, c * LANES:(c + 1) * LANES]

    for g in range(N_GROUPS):
        tg = _dot(h, w_ref[:, g * GROUP_WIDTH:(g + 1) * GROUP_WIDTH])
        for c in range(2 * ATTN_WIDTH // MXU_DIM):
            col0 = g * GROUP_WIDTH + c * MXU_DIM
            t = tg[:, c * MXU_DIM:(c + 1) * MXU_DIM]
            t2 = t * t
            hi = t2.astype(BF16)
            lo = (t2 - hi.astype(F32)).astype(BF16)
            ss = _dot(hi, bd) + _dot(lo, bd)
            put(col0, t * lax.rsqrt(ss * (1.0 / HEAD_DIM) + EPS) * gain_ref[:, col0:col0 + MXU_DIM])
        put(g * GROUP_WIDTH + 2 * ATTN_WIDTH, tg[:, 2 * ATTN_WIDTH:])
    for g, (_, dil) in enumerate(DIL_PATTERNS):
        for r in range(dil):
            rs = slice(None) if dil == 1 else pl.ds(r, rows // dil, stride=dil)
            for c in range(GROUP_WIDTH // LANES):
                out_refs[g][0, r, :, c * LANES:(c + 1) * LANES] = (
                    t_s[g * (GROUP_WIDTH // LANES) + c, rs, :].astype(BF16))


def _qkv_proj(x, gain, w_qkv, qk_gain, head_ones):
    bsz, seq, d = x.shape
    width = w_qkv.shape[1]
    rows = QKV_ROWS
    out_shape, out_specs = [], []
    for _, dil in DIL_PATTERNS:
        out_shape.append(jax.ShapeDtypeStruct((bsz, dil, seq // dil, GROUP_WIDTH), BF16))
        out_specs.append(pl.BlockSpec((1, dil, rows // dil, GROUP_WIDTH), lambda b, i: (b, 0, i, 0)))
    return pl.pallas_call(
        _qkv_body,
        out_shape=out_shape,
        grid=(bsz, seq // rows),
        in_specs=[
            pl.BlockSpec((None, rows, d), lambda b, i: (b, i, 0)),
            _const_spec(gain.shape), _const_spec(w_qkv.shape), _const_spec(qk_gain.shape),
            _const_spec(head_ones.shape),
        ],
        out_specs=out_specs,
        scratch_shapes=[pltpu.VMEM((width // LANES, rows, LANES), F32)],
        compiler_params=_params("parallel", "parallel"),
        name="qkv_proj",
    )(x, gain, w_qkv, qk_gain, head_ones)


def _attn_body(q_ref, kp_ref, kc_ref, kn_ref, vp_ref, vc_ref, vn_ref, bias_ref, o_ref, l_ref):
    hw = HALF_WINDOW
    qb = QUERY_BLOCK
    n_sub = q_ref.shape[2] // qb
    last_block = pl.num_programs(2) * n_sub - 1
    heads_per_tile = MXU_DIM // HEAD_DIM
    head_of_lane = lax.broadcasted_iota(I32, (qb, MXU_DIM), 1) // HEAD_DIM
    for rr in range(q_ref.shape[1]):
        k_all = jnp.concatenate([kp_ref[0, rr, qb - hw:, :], kc_ref[0, rr], kn_ref[0, rr, :hw, :]], axis=0)
        v_all = jnp.concatenate([vp_ref[0, rr, qb - hw:, :], vc_ref[0, rr], vn_ref[0, rr, :hw, :]], axis=0)
        for sub in range(n_sub):
            block = pl.program_id(2) * n_sub + sub
            variant = jnp.where(block == 0, 0, jnp.where(block == last_block, 2, 1))
            q = q_ref[0, rr, sub * qb:(sub + 1) * qb, :]
            k = k_all[sub * qb:sub * qb + qb + 2 * hw]
            v = v_all[sub * qb:sub * qb + qb + 2 * hw]
            for half in range(ATTN_WIDTH // MXU_DIM):
                cs = slice(half * MXU_DIM, (half + 1) * MXU_DIM)
                qh, kh, vh = q[:, cs], k[:, cs], v[:, cs]
                qs = jnp.concatenate(
                    [jnp.where(head_of_lane == hh, qh, jnp.zeros_like(qh)) for hh in range(heads_per_tile)], axis=0)
                s = lax.dot_general(qs, kh, (((1,), (1,)), ((), ())), preferred_element_type=F32)
                s = s + bias_ref[variant, half]
                m = jnp.max(s, axis=-1, keepdims=True)
                p = jnp.exp(s - m)
                l = jnp.sum(p, axis=-1, keepdims=True)
                pv = _dot(p.astype(BF16), vh) / l
                lse = m + jnp.log(l)
                o_half = jnp.zeros((qb, MXU_DIM), F32)
                l_half = jnp.zeros((qb, MXU_DIM), F32)
                for hh in range(heads_per_tile):
                    rs = slice(hh * qb, (hh + 1) * qb)
                    sel = head_of_lane == hh
                    o_half = jnp.where(sel, pv[rs], o_half)
                    l_half = jnp.where(sel, lse[rs], l_half)
                o_ref[0, rr, sub * qb:(sub + 1) * qb, cs] = o_half.astype(BF16)
                l_ref[0, rr, sub * qb:(sub + 1) * qb, cs] = l_half


def _attn_group(g, qkv_g, bias):
    bsz, dil, sub_len, _ = qkv_g.shape
    nb = sub_len // QUERY_BLOCK
    per_step = min(ATTN_BLOCKS, nb)
    rows = per_step * QUERY_BLOCK
    res = min(ATTN_BLOCKS // per_step, dil)

    def main(which):
        return pl.BlockSpec((1, res, rows, ATTN_WIDTH), lambda b, r, i: (b, r, i, which))

    def halo(which, offset):
        def index(b, r, i):
            return (b, r, jnp.clip(i * per_step + offset, 0, nb - 1), which)
        return pl.BlockSpec((1, res, QUERY_BLOCK, ATTN_WIDTH), index)

    return pl.pallas_call(
        _attn_body,
        out_shape=[jax.ShapeDtypeStruct((bsz, dil, sub_len, ATTN_WIDTH), BF16),
                   jax.ShapeDtypeStruct((bsz, dil, sub_len, ATTN_WIDTH), F32)],
        grid=(bsz, dil // res, nb // per_step),
        in_specs=[main(0), halo(1, -1), main(1), halo(1, per_step), halo(2, -1), main(2), halo(2, per_step),
                  _const_spec(bias.shape)],
        out_specs=[main(0), main(0)],
        compiler_params=_params("parallel", "parallel", "arbitrary"),
        name=f"dilated_attn_g{g}",
    )(*([qkv_g] * 7), bias)


def _attn_bias(dil):
    slopes = jnp.exp2(-8.0 * jnp.arange(1, HEADS + 1, dtype=F32) / HEADS)
    kb = QUERY_BLOCK + 2 * HALF_WINDOW
    rel = jnp.arange(kb)[None, :] - HALF_WINDOW - jnp.arange(QUERY_BLOCK)[:, None]
    band = jnp.abs(rel) <= HALF_WINDOW
    alibi = -slopes[:, None, None] * (dil * jnp.abs(rel)).astype(F32)[None]
    j = jnp.arange(kb)
    edge = jnp.stack([j >= HALF_WINDOW, j >= 0, j < kb - HALF_WINDOW])
    ok = band[None, None] & edge[:, None, None, :]
    tab = jnp.where(ok, alibi[None], NEG_INF)
    per_tile = MXU_DIM // HEAD_DIM
    return tab.reshape(3, HEADS // per_tile, per_tile * QUERY_BLOCK, kb)


def _merge_body(*refs):
    o_refs, l_refs = refs[:N_GROUPS], refs[N_GROUPS:2 * N_GROUPS]
    x_ref, wo_ref, rg_ref, wr_ref, br_ref, out_ref, hn_ref, aff_ref, o_s, l_s = refs[2 * N_GROUPS:]
    rows = x_ref.shape[0]
    tiles = ATTN_WIDTH // LANES
    for g, (_, dil) in enumerate(DIL_PATTERNS):
        for r in range(dil):
            rs = slice(None) if dil == 1 else pl.ds(r, rows // dil, stride=dil)
            for c in range(tiles):
                cs = slice(c * LANES, (c + 1) * LANES)
                o_s[g * tiles + c, rs, :] = o_refs[g][0, r, :, cs].astype(F32)
                l_s[g * tiles + c, rs, :] = l_refs[g][0, r, :, cs]
    merged = []
    for c in range(tiles):
        lses = [l_s[g * tiles + c] for g in range(N_GROUPS)]
        mx = functools.reduce(jnp.maximum, lses)
        num = jnp.zeros_like(mx)
        den = jnp.zeros_like(mx)
        for g in range(N_GROUPS):
            e = jnp.exp(lses[g] - mx)
            num = num + e * o_s[g * tiles + c]
            den = den + e
        merged.append((num / den).astype(BF16))
    out = x_ref[...] + _dot(jnp.concatenate(merged, axis=1), wo_ref[...])
    out_ref[...] = out
    _route(out, rg_ref, wr_ref, br_ref, hn_ref, aff_ref)


def _attn_merge(outs, x, w_o, route):
    bsz, seq, d = x.shape
    rows = MERGE_ROWS
    args, in_specs = [], []
    for which in range(2):
        for g, (_, dil) in enumerate(DIL_PATTERNS):
            args.append(outs[g][which])
            in_specs.append(pl.BlockSpec((1, dil, rows // dil, ATTN_WIDTH), lambda b, i: (b, 0, i, 0)))
    args += [x, w_o, *route]
    in_specs += [pl.BlockSpec((None, rows, d), lambda b, i: (b, i, 0)), _const_spec(w_o.shape)]
    in_specs += [_const_spec(a.shape) for a in route]

    def row_block(width):
        return pl.BlockSpec((None, rows, width), lambda b, i: (b, i, 0))

    return pl.pallas_call(
        _merge_body,
        out_shape=[jax.ShapeDtypeStruct((bsz, seq, d), F32), jax.ShapeDtypeStruct((bsz, seq, d), BF16),
                   jax.ShapeDtypeStruct((bsz, seq, LANES), F32)],
        grid=(bsz, seq // rows),
        in_specs=in_specs,
        out_specs=[row_block(d), row_block(d), row_block(LANES)],
        scratch_shapes=[pltpu.VMEM((N_GROUPS * ATTN_WIDTH // LANES, rows, LANES), F32),
                        pltpu.VMEM((N_GROUPS * ATTN_WIDTH // LANES, rows, LANES), F32)],
        compiler_params=_params("parallel", "parallel"),
        name="attn_merge",
    )(*args)


def _select_one(cap, s, aff_ref, tri_ref, post_ref, gatet_ref, pos_ref, lo_ref):
    seq = aff_ref.shape[1]
    aff_t = aff_ref[s].T[:N_EXPERTS]
    key = pltpu.bitcast(aff_t, I32)
    thr = jnp.zeros((N_EXPERTS, 1), I32)
    for bit in range(30, -1, -1):
        cand = thr | (1 << bit)
        cnt = jnp.sum(jnp.where(key >= cand, 1.0, 0.0), axis=1, keepdims=True)
        thr = jnp.where(cnt >= cap, cand, thr)
    gt = key > thr
    eq = key == thr
    need = cap - jnp.sum(jnp.where(gt, 1.0, 0.0), axis=1, keepdims=True)
    tri = tri_ref[...]
    nblk = seq // TOKEN_BLOCK

    def prefix(mask_f, want_offsets):
        off = jnp.zeros((N_EXPERTS, 1), F32)
        parts, offs = [], []
        for jb in range(nblk):
            blk = mask_f[:, jb * TOKEN_BLOCK:(jb + 1) * TOKEN_BLOCK]
            offs.append(off)
            parts.append(_dot(blk.astype(BF16), tri) + off)
            off = off + jnp.sum(blk, axis=1, keepdims=True)
        offs.append(off)
        return jnp.concatenate(parts, axis=1), (offs if want_offsets else None)

    eq_rank, _ = prefix(jnp.where(eq, 1.0, 0.0), False)
    sel = gt | (eq & (eq_rank < need))
    rank, offs = prefix(jnp.where(sel, 1.0, 0.0), True)
    post = jnp.where(sel, rank, -1.0)
    post_ref[s] = post.astype(I32)
    gatet_ref[s] = jnp.where(sel, aff_t, 0.0)
    padded = jnp.concatenate([post, jnp.full((LANES - N_EXPERTS, seq), -1.0, F32)], axis=0)
    pos_ref[s] = padded.T.astype(I32)
    lane = lax.broadcasted_iota(I32, (N_EXPERTS, LANES), 1)
    lo = jnp.zeros((N_EXPERTS, LANES), F32)
    for jb, off in enumerate(offs):
        lo = jnp.where(lane == jb, off, lo)
    lo_ref[s] = lo.astype(I32)


def _select_body(cap, aff_ref, *refs):
    for s in range(aff_ref.shape[0]):
        _select_one(cap, s, aff_ref, *refs)


def _select(aff, tri, cap):
    bsz, seq, _ = aff.shape
    n = SELECT_SEQS if bsz % SELECT_SEQS == 0 else 1
    return pl.pallas_call(
        functools.partial(_select_body, cap),
        out_shape=[jax.ShapeDtypeStruct((bsz, N_EXPERTS, seq), I32),
                   jax.ShapeDtypeStruct((bsz, N_EXPERTS, seq), F32),
                   jax.ShapeDtypeStruct((bsz, seq, LANES), I32),
                   jax.ShapeDtypeStruct((bsz, N_EXPERTS, LANES), I32)],
        grid=(bsz // n,),
        in_specs=[pl.BlockSpec((n, seq, LANES), lambda b: (b, 0, 0)), _const_spec(tri.shape)],
        out_specs=[pl.BlockSpec((n, N_EXPERTS, seq), lambda b: (b, 0, 0)),
                   pl.BlockSpec((n, N_EXPERTS, seq), lambda b: (b, 0, 0)),
                   pl.BlockSpec((n, seq, LANES), lambda b: (b, 0, 0)),
                   pl.BlockSpec((n, N_EXPERTS, LANES), lambda b: (b, 0, 0))],
        compiler_params=_params("parallel"),
        name="moe_select",
    )(aff, tri)


def _gather_body(lo_ref, hn_ref, post_ref, gatet_ref, xin_ref, gate_ref):
    b = pl.program_id(0)
    eg = pl.program_id(1)
    seq = hn_ref.shape[1]
    cap = xin_ref.shape[2]
    nblk = seq // TOKEN_BLOCK
    xin_ref[...] = jnp.zeros_like(xin_ref)
    gate_ref[...] = jnp.zeros_like(gate_ref)

    def table(i):
        return (b * N_EXPERTS + eg * PACK + i) * LO_STRIDE

    def window(i, ts, s0, rows, iota):
        s0c = pl.multiple_of(jnp.minimum(s0, cap - rows), 16)
        slot = iota + s0c
        hit = (post_ref[0, i, :, ts] == slot) & (slot >= s0)

        def add(contrib):
            xin_ref[0, i, pl.ds(s0c, rows), :] += contrib.astype(BF16)
            gsum = jnp.sum(jnp.where(hit, gatet_ref[0, i, :, ts], 0.0), axis=1, keepdims=True)
            gate_ref[0, i, pl.ds(s0c, rows), :] += jnp.broadcast_to(gsum, (rows, LANES))

        return jnp.where(hit, 1.0, 0.0).astype(BF16), add

    first_iota = lax.broadcasted_iota(I32, (FIRST_WINDOW, TOKEN_BLOCK), 0)
    for j in range(nblk):
        ts = slice(j * TOKEN_BLOCK, (j + 1) * TOKEN_BLOCK)
        rows, adds = zip(*[window(i, ts, _floor_to(lo_ref[table(i) + j], 16), FIRST_WINDOW, first_iota)
                           for i in range(PACK)])
        contrib = _dot(jnp.concatenate(rows, axis=0), hn_ref[0, ts, :])
        for i in range(PACK):
            adds[i](contrib[i * FIRST_WINDOW:(i + 1) * FIRST_WINDOW])

    def left_over(i, j):
        done = _floor_to(lo_ref[table(i) + j], 16) + FIRST_WINDOW
        return jnp.maximum(lo_ref[table(i) + j + 1] - done, 0), done

    most = functools.reduce(jnp.maximum, [left_over(i, j)[0] for i in range(PACK) for j in range(nblk)])

    @pl.when(most > 0)
    def _():
        more_iota = lax.broadcasted_iota(I32, (GATHER_WINDOW, TOKEN_BLOCK), 0)
        for j in range(nblk):
            ts = slice(j * TOKEN_BLOCK, (j + 1) * TOKEN_BLOCK)

            def expert(i, carry, j=j, ts=ts):
                left, done = left_over(i, j)

                def further(w, carry):
                    onehot, add = window(i, ts, done + w * GATHER_WINDOW, GATHER_WINDOW, more_iota)
                    add(_dot(onehot, hn_ref[0, ts, :]))
                    return carry

                return lax.fori_loop(0, _ceil_div(left, GATHER_WINDOW), further, carry)

            lax.fori_loop(0, PACK, expert, 0)


def _gather(lo_flat, hn, post, gatet, cap):
    bsz, seq, d = hn.shape
    post4 = post.reshape(bsz, N_EXPERTS, 1, seq)
    gatet4 = gatet.reshape(bsz, N_EXPERTS, 1, seq)
    grid_spec = pltpu.PrefetchScalarGridSpec(
        num_scalar_prefetch=1,
        grid=(bsz, N_EXPERTS // PACK),
        in_specs=[pl.BlockSpec((1, seq, d), lambda b, e, lo: (b, 0, 0)),
                  pl.BlockSpec((1, PACK, 1, seq), lambda b, e, lo: (b, e, 0, 0)),
                  pl.BlockSpec((1, PACK, 1, seq), lambda b, e, lo: (b, e, 0, 0))],
        out_specs=[pl.BlockSpec((1, PACK, cap, d), lambda b, e, lo: (b, e, 0, 0)),
                   pl.BlockSpec((1, PACK, cap, LANES), lambda b, e, lo: (b, e, 0, 0))],
    )
    return pl.pallas_call(
        _gather_body,
        out_shape=[jax.ShapeDtypeStruct((bsz, N_EXPERTS, cap, d), BF16),
                   jax.ShapeDtypeStruct((bsz, N_EXPERTS, cap, LANES), F32)],
        grid_spec=grid_spec,
        compiler_params=_params("parallel", "arbitrary"),
        name="moe_gather",
    )(lo_flat, hn, post4, gatet4)


def _expert_body(xin_ref, gate_ref, w1c_ref, w3c_ref, w2c_ref, y_ref, w1_s, w3_s, w2_s):
    r = pl.program_id(0)
    b = pl.program_id(1)
    fill = lax.rem(r, 2)

    def cast_slice():
        d_rows = w1c_ref.shape[2]
        f_rows = w2c_ref.shape[2]
        d0 = pl.multiple_of(b * d_rows, d_rows)
        f0 = pl.multiple_of(b * f_rows, f_rows)
        w1_s[fill, pl.ds(d0, d_rows), :] = w1c_ref[0, 0].astype(BF16)
        w3_s[fill, pl.ds(d0, d_rows), :] = w3c_ref[0, 0].astype(BF16)
        w2_s[fill, pl.ds(f0, f_rows), :] = w2c_ref[0, 0].astype(BF16)

    @pl.when(r == 0)
    def _():
        cast_slice()
        y_ref[0, 0] = jnp.zeros(y_ref.shape[2:], BF16)

    @pl.when(r > 0)
    def _():
        cast_slice()
        use = 1 - fill
        xin = xin_ref[0, 0]
        chunk = EXPERT_CHUNK
        y = jnp.zeros(y_ref.shape[2:], F32)
        for c in range(w1_s.shape[2] // chunk):
            fs = slice(c * chunk, (c + 1) * chunk)
            hid = jax.nn.silu(_dot(xin, w1_s[use, :, fs])) * _dot(xin, w3_s[use, :, fs])
            y = y + _dot(hid.astype(BF16), w2_s[use, fs, :])
        y_ref[0, 0] = (y * gate_ref[0, 0, :, 0:1]).astype(BF16)


def _experts(layer, xin, gate, w1, w3, w2):
    bsz, ne, cap, d = xin.shape
    f = w1.shape[3]
    d_rows, f_rows = d // bsz, f // bsz
    assert d_rows * bsz == d and f_rows * bsz == f

    def act(width):
        return pl.BlockSpec((1, 1, cap, width), lambda r, b: (b, jnp.maximum(r - 1, 0), 0, 0))

    out_spec = pl.BlockSpec((1, 1, cap, d), lambda r, b: (b, r, 0, 0))

    def weight_slice(rows, cols):
        return pl.BlockSpec((1, 1, rows, cols), lambda r, b: (layer, jnp.minimum(r, ne - 1), b, 0))

    return pl.pallas_call(
        _expert_body,
        out_shape=jax.ShapeDtypeStruct((bsz, ne + 1, cap, d), BF16),
        grid=(ne + 1, bsz),
        in_specs=[act(d), act(LANES), weight_slice(d_rows, f), weight_slice(d_rows, f), weight_slice(f_rows, d)],
        out_specs=out_spec,
        scratch_shapes=[pltpu.VMEM((2, d, f), BF16), pltpu.VMEM((2, d, f), BF16), pltpu.VMEM((2, f, d), BF16)],
        compiler_params=_params("arbitrary", "arbitrary"),
        name="moe_experts",
    )(xin, gate, w1, w3, w2)


def _combine_body(lo_ref, x_ref, y_ref, pos_ref, o_ref):
    b = pl.program_id(0)
    cap = y_ref.shape[2]
    lane = lax.broadcasted_iota(I32, (TOKEN_BLOCK, PACK * FIRST_WINDOW), 1)
    lane_expert = lane // FIRST_WINDOW
    lane_slot = lane % FIRST_WINDOW
    pending = []
    for sub in range(COMBINE_BLOCKS):
        j = pl.program_id(1) * COMBINE_BLOCKS + sub
        ts = slice(sub * TOKEN_BLOCK, (sub + 1) * TOKEN_BLOCK)
        pos = pos_ref[0, ts, :]

        def bounds(e, j=j):
            base = (b * N_EXPERTS + e) * LO_STRIDE
            return lo_ref[base + j], lo_ref[base + j + 1]

        acc = x_ref[0, ts, :]
        for q in range(N_EXPERTS // PACK):
            target = jnp.full(lane.shape, -1, I32)
            y_rows = []
            for i in range(PACK):
                e = q * PACK + i
                start = _floor_to(bounds(e)[0], 16)
                s0c = pl.multiple_of(jnp.minimum(start, cap - FIRST_WINDOW), 16)
                pcol = pos[:, e:e + 1]
                rel = jnp.where(pcol >= start, pcol - s0c, -1)
                target = jnp.where(lane_expert == i, rel, target)
                y_rows.append(y_ref[0, e + 1, pl.ds(s0c, FIRST_WINDOW), :])
            onehot = jnp.where(target == lane_slot, 1.0, 0.0).astype(BF16)
            acc = acc + _dot(onehot, jnp.concatenate(y_rows, axis=0))
        o_ref[0, ts, :] = acc

        def left_over(e, bounds=bounds):
            lo, hi = bounds(e)
            done = _floor_to(lo, 16) + FIRST_WINDOW
            return jnp.maximum(hi - done, 0), done

        pending.append((ts, pos, left_over))

    for ts, pos, left_over in pending:
        most = functools.reduce(jnp.maximum, [left_over(e)[0] for e in range(N_EXPERTS)])

        @pl.when(most > 0)
        def _(ts=ts, pos=pos, left_over=left_over):
            more_lane = lax.broadcasted_iota(I32, (TOKEN_BLOCK, SCATTER_WINDOW), 1)
            for e in range(N_EXPERTS):
                left, done = left_over(e)
                pcol = pos[:, e:e + 1]

                def window(w, carry, e=e, done=done, pcol=pcol):
                    s0 = done + w * SCATTER_WINDOW
                    s0c = pl.multiple_of(jnp.minimum(s0, cap - SCATTER_WINDOW), 16)
                    hit = ((pcol - s0c) == more_lane) & (pcol >= s0)
                    onehot = jnp.where(hit, 1.0, 0.0).astype(BF16)
                    o_ref[0, ts, :] += _dot(onehot, y_ref[0, e + 1, pl.ds(s0c, SCATTER_WINDOW), :])
                    return carry

                lax.fori_loop(0, _ceil_div(left, SCATTER_WINDOW), window, 0)


def _combine(lo_flat, x, y, pos):
    bsz, seq, d = x.shape
    _, slots, cap, _ = y.shape
    rows = COMBINE_BLOCKS * TOKEN_BLOCK
    grid_spec = pltpu.PrefetchScalarGridSpec(
        num_scalar_prefetch=1,
        grid=(bsz, seq // rows),
        in_specs=[pl.BlockSpec((1, rows, d), lambda b, j, lo: (b, j, 0)),
                  pl.BlockSpec((1, slots, cap, d), lambda b, j, lo: (b, 0, 0, 0)),
                  pl.BlockSpec((1, rows, LANES), lambda b, j, lo: (b, j, 0))],
        out_specs=pl.BlockSpec((1, rows, d), lambda b, j, lo: (b, j, 0)),
    )
    return pl.pallas_call(
        _combine_body,
        out_shape=jax.ShapeDtypeStruct((bsz, seq, d), F32),
        grid_spec=grid_spec,
        compiler_params=_params("parallel", "arbitrary"),
        name="moe_combine",
    )(lo_flat, x, y, pos)


def _ec_moe(layer, x, hn, aff, w1, w3, w2, tri):
    bsz, seq, d = x.shape
    cap = CAPACITY_FACTOR * seq // N_EXPERTS
    post, gatet, pos, lo = _select(aff, tri, cap)
    lo_flat = lo[:, :, :LO_STRIDE].reshape(-1)
    xin, gate = _gather(lo_flat, hn, post, gatet, cap)
    y = _experts(layer, xin, gate, w1, w3, w2)
    return _combine(lo_flat, x, y, pos)


def _group_major(a):
    lead = a.shape[:-1]
    a = a.reshape(lead + (3, N_GROUPS, ATTN_WIDTH))
    return jnp.swapaxes(a, -3, -2).reshape(lead + (3 * N_GROUPS * ATTN_WIDTH,))


def kernel(x, mix_norm, ffn_norm, gm_w_in, gm_b_in, gm_v_norm, gm_w_s, gm_b_s, gm_w_out, gm_b_out,
           at_w_qkv, at_q_norm, at_k_norm, at_w_o, moe_w_router, moe_b_router, moe_w1, moe_w3, moe_w2):
    bsz, seq, d = x.shape
    depth = mix_norm.shape[0]
    n_mixers = 2

    lane_head = jnp.arange(MXU_DIM) // HEAD_DIM
    head_ones = (lane_head[:, None] == lane_head[None, :]).astype(BF16)
    tri = (jnp.arange(TOKEN_BLOCK)[:, None] < jnp.arange(TOKEN_BLOCK)[None, :]).astype(BF16)
    biases = [_attn_bias(dil) for _, dil in DIL_PATTERNS]
    pad_e = LANES - N_EXPERTS

    for i in range(depth):
        j = i // n_mixers
        gain = mix_norm[i][None, :]
        w_r = jnp.pad(moe_w_router[i], ((0, 0), (0, pad_e)))
        w_hi = w_r.astype(BF16)
        w_split = jnp.concatenate([w_hi, (w_r - w_hi.astype(F32)).astype(BF16)], axis=1)
        b_r = jnp.pad(moe_b_router[i], (0, pad_e), constant_values=NEG_INF)[None, :]
        route = (ffn_norm[i][None, :], w_split, b_r)
        if i % n_mixers == 0:
            x, hn, aff = _gmlp_layer(
                x.reshape(bsz * seq, d), gain, gm_w_in[j].astype(BF16), gm_b_in[j][None, :],
                gm_v_norm[j][None, :], gm_w_s[j].astype(BF16), gm_b_s[j].T, gm_w_out[j].astype(BF16),
                gm_b_out[j][None, :], route)
        else:
            def per_head(g):
                return jnp.tile(g[:, None, :], (1, HEADS, 1)).reshape(N_GROUPS, ATTN_WIDTH)
            qk_gain = jnp.stack([per_head(at_q_norm[j]) * (HEAD_DIM ** -0.5), per_head(at_k_norm[j]),
                                 jnp.ones((N_GROUPS, ATTN_WIDTH), F32)]).reshape(1, -1)
            qkvs = _qkv_proj(x, gain, _group_major(at_w_qkv[j]).astype(BF16), _group_major(qk_gain), head_ones)
            outs = [_attn_group(g, qkvs[g], biases[g]) for g in range(N_GROUPS)]
            x, hn, aff = _attn_merge(outs, x, at_w_o[j].astype(BF16), route)
        x = _ec_moe(i, x.reshape(bsz, seq, d), hn.reshape(bsz, seq, d), aff.reshape(bsz, seq, LANES),
                    moe_w1, moe_w3, moe_w2, tri)
    return x
```

```python
import functools

import jax
import jax.numpy as jnp
from jax import lax
from jax.experimental import pallas as pl
from jax.experimental.pallas import tpu as pltpu

F32 = jnp.float32
BF16 = jnp.bfloat16
I32 = jnp.int32

EPS = 1e-6
NEG_INF = -1e30

LANES = 128
MXU_DIM = 256
VMEM_LIMIT_BYTES = 56 * 1024 * 1024

GMLP_CHUNK = 128
GMLP_GROUPS = 8
DIL_PATTERNS = ((128, 1), (512, 4), (2048, 16))
N_GROUPS = len(DIL_PATTERNS)
HEADS = 8
HEAD_DIM = 64
ATTN_WIDTH = HEADS * HEAD_DIM
GROUP_WIDTH = 3 * ATTN_WIDTH
QUERY_BLOCK = 128
HALF_WINDOW = 64
N_EXPERTS = 16
CAPACITY_FACTOR = 2

GMLP_ROWS = 512
QKV_ROWS = 512
MERGE_ROWS = 512
ATTN_BLOCKS = 8
TOKEN_BLOCK = MXU_DIM
SELECT_SEQS = 2
COMBINE_BLOCKS = 2
PACK = 4
FIRST_WINDOW = MXU_DIM // PACK
GATHER_WINDOW = 128
SCATTER_WINDOW = MXU_DIM
LO_STRIDE = 32
EXPERT_SEQS = 2
EXPERT_CHUNK = 2 * MXU_DIM


def _params(*semantics):
    return pltpu.CompilerParams(dimension_semantics=semantics, vmem_limit_bytes=VMEM_LIMIT_BYTES)


def _const_spec(shape):
    zeros = (0,) * len(shape)
    return pl.BlockSpec(shape, lambda *_: zeros, pipeline_mode=pl.Buffered(1))


def _rms(x, gain):
    return x * lax.rsqrt(jnp.mean(x * x, axis=-1, keepdims=True) + EPS) * gain


def _dot(a, b):
    return jnp.dot(a, b, preferred_element_type=F32)


def _floor_to(x, multiple):
    shift = multiple.bit_length() - 1
    return lax.shift_left(lax.shift_right_logical(x, shift), shift)


def _ceil_div(x, divisor):
    return lax.shift_right_logical(x + (divisor - 1), divisor.bit_length() - 1)


def _route(x, g_ref, wr_ref, br_ref, hn_ref, aff_ref):
    hn = _rms(x, g_ref[...])
    hi = hn.astype(BF16)
    lo = (hn - hi.astype(F32)).astype(BF16)
    hn_ref[...] = hi
    both = _dot(hi, wr_ref[...])
    logits = both[:, :LANES] + both[:, LANES:] + _dot(lo, wr_ref[:, :LANES]) + br_ref[...]
    m = jnp.max(logits, axis=-1, keepdims=True)
    e = jnp.exp(logits - m)
    aff_ref[...] = e / jnp.sum(e, axis=-1, keepdims=True)


def _gmlp_body(x_ref, g_ref, win_ref, bin_ref, vg_ref, ws_ref, bst_ref, wout_ref, bout_ref,
               rg_ref, wr_ref, br_ref, o_ref, hn_ref, aff_ref, v_s, uv_s):
    x = x_ref[...]
    rows, width = v_s.shape
    gw = width // GMLP_GROUPS
    h = _rms(x, g_ref[...]).astype(BF16)
    zv = jax.nn.gelu(_dot(h, win_ref[:, width:]) + bin_ref[:, width:])
    r = lax.rsqrt(jnp.mean(zv * zv, axis=-1, keepdims=True) + EPS)
    v_s[...] = (zv * r * vg_ref[...]).astype(BF16)
    pair = 2 * gw
    for p in range(GMLP_GROUPS // 2):
        zu = jax.nn.gelu(_dot(h, win_ref[:, p * pair:(p + 1) * pair]) + bin_ref[:, p * pair:(p + 1) * pair])
        for gl in range(2):
            g = 2 * p + gl
            cols = slice(g * gw, (g + 1) * gw)
            bias = bst_ref[:, g:g + 1]
            for c in range(rows // GMLP_CHUNK):
                rs = slice(c * GMLP_CHUNK, (c + 1) * GMLP_CHUNK)
                sv = _dot(ws_ref[g], v_s[rs, cols]) + bias
                uv_s[rs, cols] = (zu[rs, gl * gw:(gl + 1) * gw] * sv).astype(BF16)
    out = x + _dot(uv_s[...], wout_ref[...]) + bout_ref[...]
    o_ref[...] = out
    _route(out, rg_ref, wr_ref, br_ref, hn_ref, aff_ref)


def _gmlp_layer(x2, gain, w_in, b_in, v_gain, w_s, b_st, w_out, b_out, route):
    n, d = x2.shape
    width = w_out.shape[0]
    rows = GMLP_ROWS
    return pl.pallas_call(
        _gmlp_body,
        out_shape=[jax.ShapeDtypeStruct((n, d), F32), jax.ShapeDtypeStruct((n, d), BF16),
                   jax.ShapeDtypeStruct((n, LANES), F32)],
        grid=(n // rows,),
        in_specs=[
            pl.BlockSpec((rows, d), lambda i: (i, 0)),
            _const_spec(gain.shape), _const_spec(w_in.shape), _const_spec(b_in.shape),
            _const_spec(v_gain.shape), _const_spec(w_s.shape), _const_spec(b_st.shape),
            _const_spec(w_out.shape), _const_spec(b_out.shape),
        ] + [_const_spec(a.shape) for a in route],
        out_specs=[pl.BlockSpec((rows, d), lambda i: (i, 0)), pl.BlockSpec((rows, d), lambda i: (i, 0)),
                   pl.BlockSpec((rows, LANES), lambda i: (i, 0))],
        scratch_shapes=[pltpu.VMEM((rows, width), BF16), pltpu.VMEM((rows, width), BF16)],
        compiler_params=_params("parallel"),
        name="gmlp_layer",
    )(x2, gain, w_in, b_in, v_gain, w_s, b_st, w_out, b_out, *route)


def _qkv_body(x_ref, g_ref, w_ref, gain_ref, bd_ref, *rest):
    out_refs, t_s = rest[:N_GROUPS], rest[N_GROUPS]
    rows = t_s.shape[1]
    h = _rms(x_ref[...], g_ref[...]).astype(BF16)
    bd = bd_ref[...]

    def put(col0, t):
        for c in range(t.shape[1] // LANES):
            t_s[col0 // LANES + c] = t[:, c * LANES:(c + 1) * LANES]

    for g in range(N_GROUPS):
        tg = _dot(h, w_ref[:, g * GROUP_WIDTH:(g + 1) * GROUP_WIDTH])
        for c in range(2 * ATTN_WIDTH // MXU_DIM):
            col0 = g * GROUP_WIDTH + c * MXU_DIM
            t = tg[:, c * MXU_DIM:(c + 1) * MXU_DIM]
            t2 = t * t
            hi = t2.astype(BF16)
            lo = (t2 - hi.astype(F32)).astype(BF16)
            ss = _dot(hi, bd) + _dot(lo, bd)
            put(col0, t * lax.rsqrt(ss * (1.0 / HEAD_DIM) + EPS) * gain_ref[:, col0:col0 + MXU_DIM])
        put(g * GROUP_WIDTH + 2 * ATTN_WIDTH, tg[:, 2 * ATTN_WIDTH:])
    for g, (_, dil) in enumerate(DIL_PATTERNS):
        for r in range(dil):
            rs = slice(None) if dil == 1 else pl.ds(r, rows // dil, stride=dil)
            for c in range(GROUP_WIDTH // LANES):
                out_refs[g][0, r, :, c * LANES:(c + 1) * LANES] = (
                    t_s[g * (GROUP_WIDTH // LANES) + c, rs, :].astype(BF16))


def _qkv_proj(x, gain, w_qkv, qk_gain, head_ones):
    bsz, seq, d = x.shape
    width = w_qkv.shape[1]
    rows = QKV_ROWS
    out_shape, out_specs = [], []
    for _, dil in DIL_PATTERNS:
        out_shape.append(jax.ShapeDtypeStruct((bsz, dil, seq // dil, GROUP_WIDTH), BF16))
        out_specs.append(pl.BlockSpec((1, dil, rows // dil, GROUP_WIDTH), lambda b, i: (b, 0, i, 0)))
    return pl.pallas_call(
        _qkv_body,
        out_shape=out_shape,
        grid=(bsz, seq // rows),
        in_specs=[
            pl.BlockSpec((None, rows, d), lambda b, i: (b, i, 0)),
            _const_spec(gain.shape), _const_spec(w_qkv.shape), _const_spec(qk_gain.shape),
            _const_spec(head_ones.shape),
        ],
        out_specs=out_specs,
        scratch_shapes=[pltpu.VMEM((width // LANES, rows, LANES), F32)],
        compiler_params=_params("parallel", "parallel"),
        name="qkv_proj",
    )(x, gain, w_qkv, qk_gain, head_ones)


def _attn_body(q_ref, kp_ref, kc_ref, kn_ref, vp_ref, vc_ref, vn_ref, bias_ref, o_ref, l_ref):
    hw = HALF_WINDOW
    qb = QUERY_BLOCK
    n_sub = q_ref.shape[2] // qb
    last_block = pl.num_programs(2) * n_sub - 1
    heads_per_tile = MXU_DIM // HEAD_DIM
    head_of_lane = lax.broadcasted_iota(I32, (qb, MXU_DIM), 1) // HEAD_DIM
    for rr in range(q_ref.shape[1]):
        k_all = jnp.concatenate([kp_ref[0, rr, qb - hw:, :], kc_ref[0, rr], kn_ref[0, rr, :hw, :]], axis=0)
        v_all = jnp.concatenate([vp_ref[0, rr, qb - hw:, :], vc_ref[0, rr], vn_ref[0, rr, :hw, :]], axis=0)
        for sub in range(n_sub):
            block = pl.program_id(2) * n_sub + sub
            variant = jnp.where(block == 0, 0, jnp.where(block == last_block, 2, 1))
            q = q_ref[0, rr, sub * qb:(sub + 1) * qb, :]
            k = k_all[sub * qb:sub * qb + qb + 2 * hw]
            v = v_all[sub * qb:sub * qb + qb + 2 * hw]
            for half in range(ATTN_WIDTH // MXU_DIM):
                cs = slice(half * MXU_DIM, (half + 1) * MXU_DIM)
                qh, kh, vh = q[:, cs], k[:, cs], v[:, cs]
                qs = jnp.concatenate(
                    [jnp.where(head_of_lane == hh, qh, jnp.zeros_like(qh)) for hh in range(heads_per_tile)], axis=0)
                s = lax.dot_general(qs, kh, (((1,), (1,)), ((), ())), preferred_element_type=F32)
                s = s + bias_ref[variant, half]
                m = jnp.max(s, axis=-1, keepdims=True)
                p = jnp.exp(s - m)
                l = jnp.sum(p, axis=-1, keepdims=True)
                pv = _dot(p.astype(BF16), vh) / l
                lse = m + jnp.log(l)
                o_half = jnp.zeros((qb, MXU_DIM), F32)
                l_half = jnp.zeros((qb, MXU_DIM), F32)
                for hh in range(heads_per_tile):
                    rs = slice(hh * qb, (hh + 1) * qb)
                    sel = head_of_lane == hh
                    o_half = jnp.where(sel, pv[rs], o_half)
                    l_half = jnp.where(sel, lse[rs], l_half)
                o_ref[0, rr, sub * qb:(sub + 1) * qb, cs] = o_half.astype(BF16)
                l_ref[0, rr, sub * qb:(sub + 1) * qb, cs] = l_half


def _attn_group(g, qkv_g, bias):
    bsz, dil, sub_len, _ = qkv_g.shape
    nb = sub_len // QUERY_BLOCK
    per_step = min(ATTN_BLOCKS, nb)
    rows = per_step * QUERY_BLOCK
    res = min(ATTN_BLOCKS // per_step, dil)

    def main(which):
        return pl.BlockSpec((1, res, rows, ATTN_WIDTH), lambda b, r, i: (b, r, i, which))

    def halo(which, offset):
        def index(b, r, i):
            return (b, r, jnp.clip(i * per_step + offset, 0, nb - 1), which)
        return pl.BlockSpec((1, res, QUERY_BLOCK, ATTN_WIDTH), index)

    return pl.pallas_call(
        _attn_body,
        out_shape=[jax.ShapeDtypeStruct((bsz, dil, sub_len, ATTN_WIDTH), BF16),
                   jax.ShapeDtypeStruct((bsz, dil, sub_len, ATTN_WIDTH), F32)],
        grid=(bsz, dil // res, nb // per_step),
        in_specs=[main(0), halo(1, -1), main(1), halo(1, per_step), halo(2, -1), main(2), halo(2, per_step),
                  _const_spec(bias.shape)],
        out_specs=[main(0), main(0)],
        compiler_params=_params("parallel", "parallel", "arbitrary"),
        name=f"dilated_attn_g{g}",
    )(*([qkv_g] * 7), bias)


def _attn_bias(dil):
    slopes = jnp.exp2(-8.0 * jnp.arange(1, HEADS + 1, dtype=F32) / HEADS)
    kb = QUERY_BLOCK + 2 * HALF_WINDOW
    rel = jnp.arange(kb)[None, :] - HALF_WINDOW - jnp.arange(QUERY_BLOCK)[:, None]
    band = jnp.abs(rel) <= HALF_WINDOW
    alibi = -slopes[:, None, None] * (dil * jnp.abs(rel)).astype(F32)[None]
    j = jnp.arange(kb)
    edge = jnp.stack([j >= HALF_WINDOW, j >= 0, j < kb - HALF_WINDOW])
    ok = band[None, None] & edge[:, None, None, :]
    tab = jnp.where(ok, alibi[None], NEG_INF)
    per_tile = MXU_DIM // HEAD_DIM
    return tab.reshape(3, HEADS // per_tile, per_tile * QUERY_BLOCK, kb)


def _merge_body(*refs):
    o_refs, l_refs = refs[:N_GROUPS], refs[N_GROUPS:2 * N_GROUPS]
    x_ref, wo_ref, rg_ref, wr_ref, br_ref, out_ref, hn_ref, aff_ref, o_s, l_s = refs[2 * N_GROUPS:]
    rows = x_ref.shape[0]
    tiles = ATTN_WIDTH // LANES
    for g, (_, dil) in enumerate(DIL_PATTERNS):
        for r in range(dil):
            rs = slice(None) if dil == 1 else pl.ds(r, rows // dil, stride=dil)
            for c in range(tiles):
                cs = slice(c * LANES, (c + 1) * LANES)
                o_s[g * tiles + c, rs, :] = o_refs[g][0, r, :, cs].astype(F32)
                l_s[g * tiles + c, rs, :] = l_refs[g][0, r, :, cs]
    merged = []
    for c in range(tiles):
        lses = [l_s[g * tiles + c] for g in range(N_GROUPS)]
        mx = functools.reduce(jnp.maximum, lses)
        num = jnp.zeros_like(mx)
        den = jnp.zeros_like(mx)
        for g in range(N_GROUPS):
            e = jnp.exp(lses[g] - mx)
            num = num + e * o_s[g * tiles + c]
            den = den + e
        merged.append((num / den).astype(BF16))
    out = x_ref[...] + _dot(jnp.concatenate(merged, axis=1), wo_ref[...])
    out_ref[...] = out
    _route(out, rg_ref, wr_ref, br_ref, hn_ref, aff_ref)


def _attn_merge(outs, x, w_o, route):
    bsz, seq, d = x.shape
    rows = MERGE_ROWS
    args, in_specs = [], []
    for which in range(2):
        for g, (_, dil) in enumerate(DIL_PATTERNS):
            args.append(outs[g][which])
            in_specs.append(pl.BlockSpec((1, dil, rows // dil, ATTN_WIDTH), lambda b, i: (b, 0, i, 0)))
    args += [x, w_o, *route]
    in_specs += [pl.BlockSpec((None, rows, d), lambda b, i: (b, i, 0)), _const_spec(w_o.shape)]
    in_specs += [_const_spec(a.shape) for a in route]

    def row_block(width):
        return pl.BlockSpec((None, rows, width), lambda b, i: (b, i, 0))

    return pl.pallas_call(
        _merge_body,
        out_shape=[jax.ShapeDtypeStruct((bsz, seq, d), F32), jax.ShapeDtypeStruct((bsz, seq, d), BF16),
                   jax.ShapeDtypeStruct((bsz, seq, LANES), F32)],
        grid=(bsz, seq // rows),
        in_specs=in_specs,
        out_specs=[row_block(d), row_block(d), row_block(LANES)],
        scratch_shapes=[pltpu.VMEM((N_GROUPS * ATTN_WIDTH // LANES, rows, LANES), F32),
                        pltpu.VMEM((N_GROUPS * ATTN_WIDTH // LANES, rows, LANES), F32)],
        compiler_params=_params("parallel", "parallel"),
        name="attn_merge",
    )(*args)


def _select_one(cap, s, aff_ref, tri_ref, post_ref, gatet_ref, pos_ref, lo_ref):
    seq = aff_ref.shape[1]
    aff_t = aff_ref[s].T[:N_EXPERTS]
    key = pltpu.bitcast(aff_t, I32)
    thr = jnp.zeros((N_EXPERTS, 1), I32)
    for bit in range(30, -1, -1):
        cand = thr | (1 << bit)
        cnt = jnp.sum(jnp.where(key >= cand, 1.0, 0.0), axis=1, keepdims=True)
        thr = jnp.where(cnt >= cap, cand, thr)
    gt = key > thr
    eq = key == thr
    need = cap - jnp.sum(jnp.where(gt, 1.0, 0.0), axis=1, keepdims=True)
    tri = tri_ref[...]
    nblk = seq // TOKEN_BLOCK

    def prefix(mask_f, want_offsets):
        off = jnp.zeros((N_EXPERTS, 1), F32)
        parts, offs = [], []
        for jb in range(nblk):
            blk = mask_f[:, jb * TOKEN_BLOCK:(jb + 1) * TOKEN_BLOCK]
            offs.append(off)
            parts.append(_dot(blk.astype(BF16), tri) + off)
            off = off + jnp.sum(blk, axis=1, keepdims=True)
        offs.append(off)
        return jnp.concatenate(parts, axis=1), (offs if want_offsets else None)

    eq_rank, _ = prefix(jnp.where(eq, 1.0, 0.0), False)
    sel = gt | (eq & (eq_rank < need))
    rank, offs = prefix(jnp.where(sel, 1.0, 0.0), True)
    post = jnp.where(sel, rank, -1.0)
    post_ref[s] = post.astype(I32)
    gatet_ref[s] = jnp.where(sel, aff_t, 0.0)
    padded = jnp.concatenate([post, jnp.full((LANES - N_EXPERTS, seq), -1.0, F32)], axis=0)
    pos_ref[s] = padded.T.astype(I32)
    lane = lax.broadcasted_iota(I32, (N_EXPERTS, LANES), 1)
    lo = jnp.zeros((N_EXPERTS, LANES), F32)
    for jb, off in enumerate(offs):
        lo = jnp.where(lane == jb, off, lo)
    lo_ref[s] = lo.astype(I32)


def _select_body(cap, aff_ref, *refs):
    for s in range(aff_ref.shape[0]):
        _select_one(cap, s, aff_ref, *refs)


def _select(aff, tri, cap):
    bsz, seq, _ = aff.shape
    n = SELECT_SEQS if bsz % SELECT_SEQS == 0 else 1
    return pl.pallas_call(
        functools.partial(_select_body, cap),
        out_shape=[jax.ShapeDtypeStruct((bsz, N_EXPERTS, seq), I32),
                   jax.ShapeDtypeStruct((bsz, N_EXPERTS, seq), F32),
                   jax.ShapeDtypeStruct((bsz, seq, LANES), I32),
                   jax.ShapeDtypeStruct((bsz, N_EXPERTS, LANES), I32)],
        grid=(bsz // n,),
        in_specs=[pl.BlockSpec((n, seq, LANES), lambda b: (b, 0, 0)), _const_spec(tri.shape)],
        out_specs=[pl.BlockSpec((n, N_EXPERTS, seq), lambda b: (b, 0, 0)),
                   pl.BlockSpec((n, N_EXPERTS, seq), lambda b: (b, 0, 0)),
                   pl.BlockSpec((n, seq, LANES), lambda b: (b, 0, 0)),
                   pl.BlockSpec((n, N_EXPERTS, LANES), lambda b: (b, 0, 0))],
        compiler_params=_params("parallel"),
        name="moe_select",
    )(aff, tri)


def _gather_body(lo_ref, hn_ref, post_ref, gatet_ref, xin_ref, gate_ref):
    b = pl.program_id(0)
    eg = pl.program_id(1)
    seq = hn_ref.shape[1]
    cap = xin_ref.shape[2]
    nblk = seq // TOKEN_BLOCK
    xin_ref[...] = jnp.zeros_like(xin_ref)
    gate_ref[...] = jnp.zeros_like(gate_ref)

    def table(i):
        return (b * N_EXPERTS + eg * PACK + i) * LO_STRIDE

    def window(i, ts, s0, rows, iota):
        s0c = pl.multiple_of(jnp.minimum(s0, cap - rows), 16)
        slot = iota + s0c
        hit = (post_ref[0, i, :, ts] == slot) & (slot >= s0)

        def add(contrib):
            xin_ref[0, i, pl.ds(s0c, rows), :] += contrib.astype(BF16)
            gsum = jnp.sum(jnp.where(hit, gatet_ref[0, i, :, ts], 0.0), axis=1, keepdims=True)
            gate_ref[0, i, pl.ds(s0c, rows), :] += jnp.broadcast_to(gsum, (rows, LANES))

        return jnp.where(hit, 1.0, 0.0).astype(BF16), add

    first_iota = lax.broadcasted_iota(I32, (FIRST_WINDOW, TOKEN_BLOCK), 0)
    for j in range(nblk):
        ts = slice(j * TOKEN_BLOCK, (j + 1) * TOKEN_BLOCK)
        rows, adds = zip(*[window(i, ts, _floor_to(lo_ref[table(i) + j], 16), FIRST_WINDOW, first_iota)
                           for i in range(PACK)])
        contrib = _dot(jnp.concatenate(rows, axis=0), hn_ref[0, ts, :])
        for i in range(PACK):
            adds[i](contrib[i * FIRST_WINDOW:(i + 1) * FIRST_WINDOW])

    def left_over(i, j):
        done = _floor_to(lo_ref[table(i) + j], 16) + FIRST_WINDOW
        return jnp.maximum(lo_ref[table(i) + j + 1] - done, 0), done

    most = functools.reduce(jnp.maximum, [left_over(i, j)[0] for i in range(PACK) for j in range(nblk)])

    @pl.when(most > 0)
    def _():
        more_iota = lax.broadcasted_iota(I32, (GATHER_WINDOW, TOKEN_BLOCK), 0)
        for j in range(nblk):
            ts = slice(j * TOKEN_BLOCK, (j + 1) * TOKEN_BLOCK)

            def expert(i, carry, j=j, ts=ts):
                left, done = left_over(i, j)

                def further(w, carry):
                    onehot, add = window(i, ts, done + w * GATHER_WINDOW, GATHER_WINDOW, more_iota)
                    add(_dot(onehot, hn_ref[0, ts, :]))
                    return carry

                return lax.fori_loop(0, _ceil_div(left, GATHER_WINDOW), further, carry)

            lax.fori_loop(0, PACK, expert, 0)


def _gather(lo_flat, hn, post, gatet, cap):
    bsz, seq, d = hn.shape
    post4 = post.reshape(bsz, N_EXPERTS, 1, seq)
    gatet4 = gatet.reshape(bsz, N_EXPERTS, 1, seq)
    grid_spec = pltpu.PrefetchScalarGridSpec(
        num_scalar_prefetch=1,
        grid=(bsz, N_EXPERTS // PACK),
        in_specs=[pl.BlockSpec((1, seq, d), lambda b, e, lo: (b, 0, 0)),
                  pl.BlockSpec((1, PACK, 1, seq), lambda b, e, lo: (b, e, 0, 0)),
                  pl.BlockSpec((1, PACK, 1, seq), lambda b, e, lo: (b, e, 0, 0))],
        out_specs=[pl.BlockSpec((1, PACK, cap, d), lambda b, e, lo: (b, e, 0, 0)),
                   pl.BlockSpec((1, PACK, cap, LANES), lambda b, e, lo: (b, e, 0, 0))],
    )
    return pl.pallas_call(
        _gather_body,
        out_shape=[jax.ShapeDtypeStruct((bsz, N_EXPERTS, cap, d), BF16),
                   jax.ShapeDtypeStruct((bsz, N_EXPERTS, cap, LANES), F32)],
        grid_spec=grid_spec,
        compiler_params=_params("parallel", "arbitrary"),
        name="moe_gather",
    )(lo_flat, hn, post4, gatet4)


def _expert_body(xin_ref, gate_ref, w1c_ref, w3c_ref, w2c_ref, y_ref, w1_s, w3_s, w2_s):
    r = pl.program_id(0)
    step = pl.program_id(1)
    fill = lax.rem(r, 2)

    def cast_slice():
        d_rows = w1c_ref.shape[2]
        f_rows = w2c_ref.shape[2]
        d0 = pl.multiple_of(step * d_rows, d_rows)
        f0 = pl.multiple_of(step * f_rows, f_rows)
        w1_s[fill, pl.ds(d0, d_rows), :] = w1c_ref[0, 0].astype(BF16)
        w3_s[fill, pl.ds(d0, d_rows), :] = w3c_ref[0, 0].astype(BF16)
        w2_s[fill, pl.ds(f0, f_rows), :] = w2c_ref[0, 0].astype(BF16)

    @pl.when(r == 0)
    def _():
        cast_slice()
        y_ref[...] = jnp.zeros_like(y_ref)

    @pl.when(r > 0)
    def _():
        cast_slice()
        use = 1 - fill
        chunk = EXPERT_CHUNK
        for s in range(xin_ref.shape[0]):
            xin = xin_ref[s, 0]
            y = jnp.zeros(y_ref.shape[2:], F32)
            for c in range(w1_s.shape[2] // chunk):
                fs = slice(c * chunk, (c + 1) * chunk)
                hid = jax.nn.silu(_dot(xin, w1_s[use, :, fs])) * _dot(xin, w3_s[use, :, fs])
                y = y + _dot(hid.astype(BF16), w2_s[use, fs, :])
            y_ref[s, 0] = (y * gate_ref[s, 0, :, 0:1]).astype(BF16)


def _experts(layer, xin, gate, w1, w3, w2):
    bsz, ne, cap, d = xin.shape
    f = w1.shape[3]
    seqs = EXPERT_SEQS if bsz % EXPERT_SEQS == 0 else 1
    steps = bsz // seqs
    d_rows, f_rows = d // steps, f // steps
    assert d_rows * steps == d and f_rows * steps == f

    def act(width):
        return pl.BlockSpec((seqs, 1, cap, width), lambda r, b: (b, jnp.maximum(r - 1, 0), 0, 0))

    out_spec = pl.BlockSpec((seqs, 1, cap, d), lambda r, b: (b, r, 0, 0))

    def weight_slice(rows, cols):
        return pl.BlockSpec((1, 1, rows, cols), lambda r, b: (layer, jnp.minimum(r, ne - 1), b, 0))

    return pl.pallas_call(
        _expert_body,
        out_shape=jax.ShapeDtypeStruct((bsz, ne + 1, cap, d), BF16),
        grid=(ne + 1, steps),
        in_specs=[act(d), act(LANES), weight_slice(d_rows, f), weight_slice(d_rows, f), weight_slice(f_rows, d)],
        out_specs=out_spec,
        scratch_shapes=[pltpu.VMEM((2, d, f), BF16), pltpu.VMEM((2, d, f), BF16), pltpu.VMEM((2, f, d), BF16)],
        compiler_params=_params("arbitrary", "arbitrary"),
        name="moe_experts",
    )(xin, gate, w1, w3, w2)


def _combine_body(lo_ref, x_ref, y_ref, pos_ref, o_ref):
    b = pl.program_id(0)
    cap = y_ref.shape[2]
    lane = lax.broadcasted_iota(I32, (TOKEN_BLOCK, PACK * FIRST_WINDOW), 1)
    lane_expert = lane // FIRST_WINDOW
    lane_slot = lane % FIRST_WINDOW
    pending = []
    for sub in range(COMBINE_BLOCKS):
        j = pl.program_id(1) * COMBINE_BLOCKS + sub
        ts = slice(sub * TOKEN_BLOCK, (sub + 1) * TOKEN_BLOCK)
        pos = pos_ref[0, ts, :]

        def bounds(e, j=j):
            base = (b * N_EXPERTS + e) * LO_STRIDE
            return lo_ref[base + j], lo_ref[base + j + 1]

        acc = x_ref[0, ts, :]
        for q in range(N_EXPERTS // PACK):
            target = jnp.full(lane.shape, -1, I32)
            y_rows = []
            for i in range(PACK):
                e = q * PACK + i
                start = _floor_to(bounds(e)[0], 16)
                s0c = pl.multiple_of(jnp.minimum(start, cap - FIRST_WINDOW), 16)
                pcol = pos[:, e:e + 1]
                rel = jnp.where(pcol >= start, pcol - s0c, -1)
                target = jnp.where(lane_expert == i, rel, target)
                y_rows.append(y_ref[0, e + 1, pl.ds(s0c, FIRST_WINDOW), :])
            onehot = jnp.where(target == lane_slot, 1.0, 0.0).astype(BF16)
            acc = acc + _dot(onehot, jnp.concatenate(y_rows, axis=0))
        o_ref[0, ts, :] = acc

        def left_over(e, bounds=bounds):
            lo, hi = bounds(e)
            done = _floor_to(lo, 16) + FIRST_WINDOW
            return jnp.maximum(hi - done, 0), done

        pending.append((ts, pos, left_over))

    for ts, pos, left_over in pending:
        most = functools.reduce(jnp.maximum, [left_over(e)[0] for e in range(N_EXPERTS)])

        @pl.when(most > 0)
        def _(ts=ts, pos=pos, left_over=left_over):
            more_lane = lax.broadcasted_iota(I32, (TOKEN_BLOCK, SCATTER_WINDOW), 1)
            for e in range(N_EXPERTS):
                left, done = left_over(e)
                pcol = pos[:, e:e + 1]

                def window(w, carry, e=e, done=done, pcol=pcol):
                    s0 = done + w * SCATTER_WINDOW
                    s0c = pl.multiple_of(jnp.minimum(s0, cap - SCATTER_WINDOW), 16)
                    hit = ((pcol - s0c) == more_lane) & (pcol >= s0)
                    onehot = jnp.where(hit, 1.0, 0.0).astype(BF16)
                    o_ref[0, ts, :] += _dot(onehot, y_ref[0, e + 1, pl.ds(s0c, SCATTER_WINDOW), :])
                    return carry

                lax.fori_loop(0, _ceil_div(left, SCATTER_WINDOW), window, 0)


def _combine(lo_flat, x, y, pos):
    bsz, seq, d = x.shape
    _, slots, cap, _ = y.shape
    rows = COMBINE_BLOCKS * TOKEN_BLOCK
    grid_spec = pltpu.PrefetchScalarGridSpec(
        num_scalar_prefetch=1,
        grid=(bsz, seq // rows),
        in_specs=[pl.BlockSpec((1, rows, d), lambda b, j, lo: (b, j, 0)),
                  pl.BlockSpec((1, slots, cap, d), lambda b, j, lo: (b, 0, 0, 0)),
                  pl.BlockSpec((1, rows, LANES), lambda b, j, lo: (b, j, 0))],
        out_specs=pl.BlockSpec((1, rows, d), lambda b, j, lo: (b, j, 0)),
    )
    return pl.pallas_call(
        _combine_body,
        out_shape=jax.ShapeDtypeStruct((bsz, seq, d), F32),
        grid_spec=grid_spec,
        compiler_params=_params("parallel", "arbitrary"),
        name="moe_combine",
    )(lo_flat, x, y, pos)


def _ec_moe(layer, x, hn, aff, w1, w3, w2, tri):
    bsz, seq, d = x.shape
    cap = CAPACITY_FACTOR * seq // N_EXPERTS
    post, gatet, pos, lo = _select(aff, tri, cap)
    lo_flat = lo[:, :, :LO_STRIDE].reshape(-1)
    xin, gate = _gather(lo_flat, hn, post, gatet, cap)
    y = _experts(layer, xin, gate, w1, w3, w2)
    return _combine(lo_flat, x, y, pos)


def _group_major(a):
    lead = a.shape[:-1]
    a = a.reshape(lead + (3, N_GROUPS, ATTN_WIDTH))
    return jnp.swapaxes(a, -3, -2).reshape(lead + (3 * N_GROUPS * ATTN_WIDTH,))


def kernel(x, mix_norm, ffn_norm, gm_w_in, gm_b_in, gm_v_norm, gm_w_s, gm_b_s, gm_w_out, gm_b_out,
           at_w_qkv, at_q_norm, at_k_norm, at_w_o, moe_w_router, moe_b_router, moe_w1, moe_w3, moe_w2):
    bsz, seq, d = x.shape
    depth = mix_norm.shape[0]
    n_mixers = 2

    lane_head = jnp.arange(MXU_DIM) // HEAD_DIM
    head_ones = (lane_head[:, None] == lane_head[None, :]).astype(BF16)
    tri = (jnp.arange(TOKEN_BLOCK)[:, None] < jnp.arange(TOKEN_BLOCK)[None, :]).astype(BF16)
    biases = [_attn_bias(dil) for _, dil in DIL_PATTERNS]
    pad_e = LANES - N_EXPERTS

    for i in range(depth):
        j = i // n_mixers
        gain = mix_norm[i][None, :]
        w_r = jnp.pad(moe_w_router[i], ((0, 0), (0, pad_e)))
        w_hi = w_r.astype(BF16)
        w_split = jnp.concatenate([w_hi, (w_r - w_hi.astype(F32)).astype(BF16)], axis=1)
        b_r = jnp.pad(moe_b_router[i], (0, pad_e), constant_values=NEG_INF)[None, :]
        route = (ffn_norm[i][None, :], w_split, b_r)
        if i % n_mixers == 0:
            x, hn, aff = _gmlp_layer(
                x.reshape(bsz * seq, d), gain, gm_w_in[j].astype(BF16), gm_b_in[j][None, :],
                gm_v_norm[j][None, :], gm_w_s[j].astype(BF16), gm_b_s[j].T, gm_w_out[j].astype(BF16),
                gm_b_out[j][None, :], route)
        else:
            def per_head(g):
                return jnp.tile(g[:, None, :], (1, HEADS, 1)).reshape(N_GROUPS, ATTN_WIDTH)
            qk_gain = jnp.stack([per_head(at_q_norm[j]) * (HEAD_DIM ** -0.5), per_head(at_k_norm[j]),
                                 jnp.ones((N_GROUPS, ATTN_WIDTH), F32)]).reshape(1, -1)
            qkvs = _qkv_proj(x, gain, _group_major(at_w_qkv[j]).astype(BF16), _group_major(qk_gain), head_ones)
            outs = [_attn_group(g, qkvs[g], biases[g]) for g in range(N_GROUPS)]
            x, hn, aff = _attn_merge(outs, x, at_w_o[j].astype(BF16), route)
        x = _ec_moe(i, x.reshape(bsz, seq, d), hn.reshape(bsz, seq, d), aff.reshape(bsz, seq, LANES),
                    moe_w1, moe_w3, moe_w2, tri)
    return x
```

```python
import functools

import jax
import jax.numpy as jnp
from jax import lax
from jax.experimental import pallas as pl
from jax.experimental.pallas import tpu as pltpu

F32 = jnp.float32
BF16 = jnp.bfloat16
I32 = jnp.int32

EPS = 1e-6
NEG_INF = -1e30

LANES = 128
MXU_DIM = 256
VMEM_LIMIT_BYTES = 56 * 1024 * 1024

GMLP_CHUNK = 128
GMLP_GROUPS = 8
DIL_PATTERNS = ((128, 1), (512, 4), (2048, 16))
N_GROUPS = len(DIL_PATTERNS)
HEADS = 8
HEAD_DIM = 64
ATTN_WIDTH = HEADS * HEAD_DIM
GROUP_WIDTH = 3 * ATTN_WIDTH
QUERY_BLOCK = 128
HALF_WINDOW = 64
N_EXPERTS = 16
CAPACITY_FACTOR = 2

GMLP_ROWS = 512
QKV_ROWS = 512
MERGE_ROWS = 512
ATTN_BLOCKS = 8
TOKEN_BLOCK = MXU_DIM
SELECT_SEQS = 2
COMBINE_BLOCKS = 2
PACK = 4
FIRST_WINDOW = MXU_DIM // PACK
GATHER_WINDOW = 128
SCATTER_WINDOW = MXU_DIM
LO_STRIDE = 32
EXPERT_SEQS = 2
EXPERT_CHUNK = 2 * MXU_DIM


def _params(*semantics):
    return pltpu.CompilerParams(dimension_semantics=semantics, vmem_limit_bytes=VMEM_LIMIT_BYTES)


def _const_spec(shape):
    zeros = (0,) * len(shape)
    return pl.BlockSpec(shape, lambda *_: zeros, pipeline_mode=pl.Buffered(1))


def _rms(x, gain):
    return x * lax.rsqrt(jnp.mean(x * x, axis=-1, keepdims=True) + EPS) * gain


def _dot(a, b):
    return jnp.dot(a, b, preferred_element_type=F32)


def _floor_to(x, multiple):
    shift = multiple.bit_length() - 1
    return lax.shift_left(lax.shift_right_logical(x, shift), shift)


def _ceil_div(x, divisor):
    return lax.shift_right_logical(x + (divisor - 1), divisor.bit_length() - 1)


def _route(x, g_ref, wr_ref, br_ref, hn_ref, aff_ref):
    hn = _rms(x, g_ref[...])
    hi = hn.astype(BF16)
    lo = (hn - hi.astype(F32)).astype(BF16)
    hn_ref[...] = hi
    both = _dot(hi, wr_ref[...])
    logits = both[:, :LANES] + both[:, LANES:] + _dot(lo, wr_ref[:, :LANES]) + br_ref[...]
    m = jnp.max(logits, axis=-1, keepdims=True)
    e = jnp.exp(logits - m)
    aff_ref[...] = e / jnp.sum(e, axis=-1, keepdims=True)


def _gmlp_body(x_ref, g_ref, win_ref, bin_ref, vg_ref, ws_ref, bst_ref, wout_ref, bout_ref,
               rg_ref, wr_ref, br_ref, o_ref, hn_ref, aff_ref, v_s, uv_s):
    x = x_ref[...]
    rows, width = v_s.shape
    gw = width // GMLP_GROUPS
    h = _rms(x, g_ref[...]).astype(BF16)
    zv = jax.nn.gelu(_dot(h, win_ref[:, width:]) + bin_ref[:, width:])
    r = lax.rsqrt(jnp.mean(zv * zv, axis=-1, keepdims=True) + EPS)
    v_s[...] = (zv * r * vg_ref[...]).astype(BF16)
    pair = 2 * gw
    for p in range(GMLP_GROUPS // 2):
        zu = jax.nn.gelu(_dot(h, win_ref[:, p * pair:(p + 1) * pair]) + bin_ref[:, p * pair:(p + 1) * pair])
        for gl in range(2):
            g = 2 * p + gl
            cols = slice(g * gw, (g + 1) * gw)
            bias = bst_ref[:, g:g + 1]
            for c in range(rows // GMLP_CHUNK):
                rs = slice(c * GMLP_CHUNK, (c + 1) * GMLP_CHUNK)
                sv = _dot(ws_ref[g], v_s[rs, cols]) + bias
                uv_s[rs, cols] = (zu[rs, gl * gw:(gl + 1) * gw] * sv).astype(BF16)
    out = x + _dot(uv_s[...], wout_ref[...]) + bout_ref[...]
    o_ref[...] = out
    _route(out, rg_ref, wr_ref, br_ref, hn_ref, aff_ref)


def _gmlp_layer(x2, gain, w_in, b_in, v_gain, w_s, b_st, w_out, b_out, route):
    n, d = x2.shape
    width = w_out.shape[0]
    rows = GMLP_ROWS
    return pl.pallas_call(
        _gmlp_body,
        out_shape=[jax.ShapeDtypeStruct((n, d), F32), jax.ShapeDtypeStruct((n, d), BF16),
                   jax.ShapeDtypeStruct((n, LANES), F32)],
        grid=(n // rows,),
        in_specs=[
            pl.BlockSpec((rows, d), lambda i: (i, 0)),
            _const_spec(gain.shape), _const_spec(w_in.shape), _const_spec(b_in.shape),
            _const_spec(v_gain.shape), _const_spec(w_s.shape), _const_spec(b_st.shape),
            _const_spec(w_out.shape), _const_spec(b_out.shape),
        ] + [_const_spec(a.shape) for a in route],
        out_specs=[pl.BlockSpec((rows, d), lambda i: (i, 0)), pl.BlockSpec((rows, d), lambda i: (i, 0)),
                   pl.BlockSpec((rows, LANES), lambda i: (i, 0))],
        scratch_shapes=[pltpu.VMEM((rows, width), BF16), pltpu.VMEM((rows, width), BF16)],
        compiler_params=_params("parallel"),
        name="gmlp_layer",
    )(x2, gain, w_in, b_in, v_gain, w_s, b_st, w_out, b_out, *route)


def _qkv_body(x_ref, g_ref, w_ref, gain_ref, bd_ref, *rest):
    out_refs, t_s = rest[:N_GROUPS], rest[N_GROUPS]
    rows = t_s.shape[1]
    h = _rms(x_ref[...], g_ref[...]).astype(BF16)
    bd = bd_ref[...]

    def put(col0, t):
        for c in range(t.shape[1] // LANES):
            t_s[col0 // LANES + c] = t[:, c * LANES:(c + 1) * LANES]

    for g in range(N_GROUPS):
        tg = _dot(h, w_ref[:, g * GROUP_WIDTH:(g + 1) * GROUP_WIDTH])
        for c in range(2 * ATTN_WIDTH // MXU_DIM):
            col0 = g * GROUP_WIDTH + c * MXU_DIM
            t = tg[:, c * MXU_DIM:(c + 1) * MXU_DIM]
            t2 = t * t
            hi = t2.astype(BF16)
            lo = (t2 - hi.astype(F32)).astype(BF16)
            ss = _dot(hi, bd) + _dot(lo, bd)
            put(col0, t * lax.rsqrt(ss * (1.0 / HEAD_DIM) + EPS) * gain_ref[:, col0:col0 + MXU_DIM])
        put(g * GROUP_WIDTH + 2 * ATTN_WIDTH, tg[:, 2 * ATTN_WIDTH:])
    for g, (_, dil) in enumerate(DIL_PATTERNS):
        for r in range(dil):
            rs = slice(None) if dil == 1 else pl.ds(r, rows // dil, stride=dil)
            for c in range(GROUP_WIDTH // LANES):
                out_refs[g][0, r, :, c * LANES:(c + 1) * LANES] = (
                    t_s[g * (GROUP_WIDTH // LANES) + c, rs, :].astype(BF16))


def _qkv_proj(x, gain, w_qkv, qk_gain, head_ones):
    bsz, seq, d = x.shape
    width = w_qkv.shape[1]
    rows = QKV_ROWS
    out_shape, out_specs = [], []
    for _, dil in DIL_PATTERNS:
        out_shape.append(jax.ShapeDtypeStruct((bsz, dil, seq // dil, GROUP_WIDTH), BF16))
        out_specs.append(pl.BlockSpec((1, dil, rows // dil, GROUP_WIDTH), lambda b, i: (b, 0, i, 0)))
    return pl.pallas_call(
        _qkv_body,
        out_shape=out_shape,
        grid=(bsz, seq // rows),
        in_specs=[
            pl.BlockSpec((None, rows, d), lambda b, i: (b, i, 0)),
            _const_spec(gain.shape), _const_spec(w_qkv.shape), _const_spec(qk_gain.shape),
            _const_spec(head_ones.shape),
        ],
        out_specs=out_specs,
        scratch_shapes=[pltpu.VMEM((width // LANES, rows, LANES), F32)],
        compiler_params=_params("parallel", "parallel"),
        name="qkv_proj",
    )(x, gain, w_qkv, qk_gain, head_ones)


def _attn_body(q_ref, kp_ref, kc_ref, kn_ref, vp_ref, vc_ref, vn_ref, bias_ref, o_ref, l_ref):
    hw = HALF_WINDOW
    qb = QUERY_BLOCK
    n_sub = q_ref.shape[2] // qb
    last_block = pl.num_programs(2) * n_sub - 1
    heads_per_tile = MXU_DIM // HEAD_DIM
    head_of_lane = lax.broadcasted_iota(I32, (qb, MXU_DIM), 1) // HEAD_DIM
    lane = lax.broadcasted_iota(I32, (qb, LANES), 1)
    for rr in range(q_ref.shape[1]):
        k_all = jnp.concatenate([kp_ref[0, rr, qb - hw:, :], kc_ref[0, rr], kn_ref[0, rr, :hw, :]], axis=0)
        v_all = jnp.concatenate([vp_ref[0, rr, qb - hw:, :], vc_ref[0, rr], vn_ref[0, rr, :hw, :]], axis=0)
        for sub in range(n_sub):
            block = pl.program_id(2) * n_sub + sub
            variant = jnp.where(block == 0, 0, jnp.where(block == last_block, 2, 1))
            q = q_ref[0, rr, sub * qb:(sub + 1) * qb, :]
            k = k_all[sub * qb:sub * qb + qb + 2 * hw]
            v = v_all[sub * qb:sub * qb + qb + 2 * hw]
            lse_tile = jnp.zeros((qb, LANES), F32)
            for half in range(ATTN_WIDTH // MXU_DIM):
                cs = slice(half * MXU_DIM, (half + 1) * MXU_DIM)
                qh, kh, vh = q[:, cs], k[:, cs], v[:, cs]
                qs = jnp.concatenate(
                    [jnp.where(head_of_lane == hh, qh, jnp.zeros_like(qh)) for hh in range(heads_per_tile)], axis=0)
                s = lax.dot_general(qs, kh, (((1,), (1,)), ((), ())), preferred_element_type=F32)
                s = s + bias_ref[variant, half]
                m = jnp.max(s, axis=-1, keepdims=True)
                p = jnp.exp(s - m)
                l = jnp.sum(p, axis=-1, keepdims=True)
                pv = _dot(p.astype(BF16), vh) / l
                lse = m + jnp.log(l)
                o_half = jnp.zeros((qb, MXU_DIM), F32)
                for hh in range(heads_per_tile):
                    rs = slice(hh * qb, (hh + 1) * qb)
                    o_half = jnp.where(head_of_lane == hh, pv[rs], o_half)
                    lse_tile = jnp.where(lane == half * heads_per_tile + hh, lse[rs], lse_tile)
                o_ref[0, rr, sub * qb:(sub + 1) * qb, cs] = o_half.astype(BF16)
            l_ref[0, rr, sub * qb:(sub + 1) * qb, :] = lse_tile


def _attn_group(g, qkv_g, bias):
    bsz, dil, sub_len, _ = qkv_g.shape
    nb = sub_len // QUERY_BLOCK
    per_step = min(ATTN_BLOCKS, nb)
    rows = per_step * QUERY_BLOCK
    res = min(ATTN_BLOCKS // per_step, dil)

    def main(which):
        return pl.BlockSpec((1, res, rows, ATTN_WIDTH), lambda b, r, i: (b, r, i, which))

    def halo(which, offset):
        def index(b, r, i):
            return (b, r, jnp.clip(i * per_step + offset, 0, nb - 1), which)
        return pl.BlockSpec((1, res, QUERY_BLOCK, ATTN_WIDTH), index)

    return pl.pallas_call(
        _attn_body,
        out_shape=[jax.ShapeDtypeStruct((bsz, dil, sub_len, ATTN_WIDTH), BF16),
                   jax.ShapeDtypeStruct((bsz, dil, sub_len, LANES), F32)],
        grid=(bsz, dil // res, nb // per_step),
        in_specs=[main(0), halo(1, -1), main(1), halo(1, per_step), halo(2, -1), main(2), halo(2, per_step),
                  _const_spec(bias.shape)],
        out_specs=[main(0), pl.BlockSpec((1, res, rows, LANES), lambda b, r, i: (b, r, i, 0))],
        compiler_params=_params("parallel", "parallel", "arbitrary"),
        name=f"dilated_attn_g{g}",
    )(*([qkv_g] * 7), bias)


def _attn_bias(dil):
    slopes = jnp.exp2(-8.0 * jnp.arange(1, HEADS + 1, dtype=F32) / HEADS)
    kb = QUERY_BLOCK + 2 * HALF_WINDOW
    rel = jnp.arange(kb)[None, :] - HALF_WINDOW - jnp.arange(QUERY_BLOCK)[:, None]
    band = jnp.abs(rel) <= HALF_WINDOW
    alibi = -slopes[:, None, None] * (dil * jnp.abs(rel)).astype(F32)[None]
    j = jnp.arange(kb)
    edge = jnp.stack([j >= HALF_WINDOW, j >= 0, j < kb - HALF_WINDOW])
    ok = band[None, None] & edge[:, None, None, :]
    tab = jnp.where(ok, alibi[None], NEG_INF)
    per_tile = MXU_DIM // HEAD_DIM
    return tab.reshape(3, HEADS // per_tile, per_tile * QUERY_BLOCK, kb)


def _merge_body(*refs):
    o_refs, l_refs = refs[:N_GROUPS], refs[N_GROUPS:2 * N_GROUPS]
    x_ref, wo_ref, spread_ref, rg_ref, wr_ref, br_ref, out_ref, hn_ref, aff_ref, o_s, l_s = refs[2 * N_GROUPS:]
    rows = x_ref.shape[0]
    tiles = ATTN_WIDTH // LANES
    for g, (_, dil) in enumerate(DIL_PATTERNS):
        for r in range(dil):
            rs = slice(None) if dil == 1 else pl.ds(r, rows // dil, stride=dil)
            l_s[g, rs, :] = l_refs[g][0, r]
            for c in range(tiles):
                o_s[g * tiles + c, rs, :] = o_refs[g][0, r, :, c * LANES:(c + 1) * LANES].astype(F32)
    lses = [l_s[g] for g in range(N_GROUPS)]
    mx = functools.reduce(jnp.maximum, lses)
    es = [jnp.exp(l - mx) for l in lses]
    den = functools.reduce(lambda a, b: a + b, es)
    merged = [jnp.zeros((rows, LANES), F32) for _ in range(tiles)]
    for g in range(N_GROUPS):
        w = es[g] / den
        hi = w.astype(BF16)
        lo = (w - hi.astype(F32)).astype(BF16)
        wide = _dot(hi, spread_ref[...]) + _dot(lo, spread_ref[...])
        for c in range(tiles):
            merged[c] = merged[c] + wide[:, c * LANES:(c + 1) * LANES] * o_s[g * tiles + c]
    merged = [m.astype(BF16) for m in merged]
    out = x_ref[...] + _dot(jnp.concatenate(merged, axis=1), wo_ref[...])
    out_ref[...] = out
    _route(out, rg_ref, wr_ref, br_ref, hn_ref, aff_ref)


def _attn_merge(outs, x, w_o, route):
    bsz, seq, d = x.shape
    rows = MERGE_ROWS
    args, in_specs = [], []
    for which, width in enumerate((ATTN_WIDTH, LANES)):
        for g, (_, dil) in enumerate(DIL_PATTERNS):
            args.append(outs[g][which])
            in_specs.append(pl.BlockSpec((1, dil, rows // dil, width), lambda b, i: (b, 0, i, 0)))
    lane_head = jnp.arange(ATTN_WIDTH) // HEAD_DIM
    spread = (jnp.arange(LANES)[:, None] == lane_head[None, :]).astype(BF16)
    args += [x, w_o, spread, *route]
    in_specs += [pl.BlockSpec((None, rows, d), lambda b, i: (b, i, 0)), _const_spec(w_o.shape),
                 _const_spec(spread.shape)]
    in_specs += [_const_spec(a.shape) for a in route]

    def row_block(width):
        return pl.BlockSpec((None, rows, width), lambda b, i: (b, i, 0))

    return pl.pallas_call(
        _merge_body,
        out_shape=[jax.ShapeDtypeStruct((bsz, seq, d), F32), jax.ShapeDtypeStruct((bsz, seq, d), BF16),
                   jax.ShapeDtypeStruct((bsz, seq, LANES), F32)],
        grid=(bsz, seq // rows),
        in_specs=in_specs,
        out_specs=[row_block(d), row_block(d), row_block(LANES)],
        scratch_shapes=[pltpu.VMEM((N_GROUPS * ATTN_WIDTH // LANES, rows, LANES), F32),
                        pltpu.VMEM((N_GROUPS, rows, LANES), F32)],
        compiler_params=_params("parallel", "parallel"),
        name="attn_merge",
    )(*args)


def _select_one(cap, s, aff_ref, tri_ref, post_ref, gatet_ref, pos_ref, lo_ref):
    seq = aff_ref.shape[1]
    aff_t = aff_ref[s].T[:N_EXPERTS]
    key = pltpu.bitcast(aff_t, I32)
    thr = jnp.zeros((N_EXPERTS, 1), I32)
    for bit in range(30, -1, -1):
        cand = thr | (1 << bit)
        cnt = jnp.sum(jnp.where(key >= cand, 1.0, 0.0), axis=1, keepdims=True)
        thr = jnp.where(cnt >= cap, cand, thr)
    gt = key > thr
    eq = key == thr
    need = cap - jnp.sum(jnp.where(gt, 1.0, 0.0), axis=1, keepdims=True)
    tri = tri_ref[...]
    nblk = seq // TOKEN_BLOCK

    def prefix(mask_f, want_offsets):
        off = jnp.zeros((N_EXPERTS, 1), F32)
        parts, offs = [], []
        for jb in range(nblk):
            blk = mask_f[:, jb * TOKEN_BLOCK:(jb + 1) * TOKEN_BLOCK]
            offs.append(off)
            parts.append(_dot(blk.astype(BF16), tri) + off)
            off = off + jnp.sum(blk, axis=1, keepdims=True)
        offs.append(off)
        return jnp.concatenate(parts, axis=1), (offs if want_offsets else None)

    eq_rank, _ = prefix(jnp.where(eq, 1.0, 0.0), False)
    sel = gt | (eq & (eq_rank < need))
    rank, offs = prefix(jnp.where(sel, 1.0, 0.0), True)
    post = jnp.where(sel, rank, -1.0)
    post_ref[s] = post.astype(I32)
    gatet_ref[s] = jnp.where(sel, aff_t, 0.0)
    padded = jnp.concatenate([post, jnp.full((LANES - N_EXPERTS, seq), -1.0, F32)], axis=0)
    pos_ref[s] = padded.T.astype(I32)
    lane = lax.broadcasted_iota(I32, (N_EXPERTS, LANES), 1)
    lo = jnp.zeros((N_EXPERTS, LANES), F32)
    for jb, off in enumerate(offs):
        lo = jnp.where(lane == jb, off, lo)
    lo_ref[s] = lo.astype(I32)


def _select_body(cap, aff_ref, *refs):
    for s in range(aff_ref.shape[0]):
        _select_one(cap, s, aff_ref, *refs)


def _select(aff, tri, cap):
    bsz, seq, _ = aff.shape
    n = SELECT_SEQS if bsz % SELECT_SEQS == 0 else 1
    return pl.pallas_call(
        functools.partial(_select_body, cap),
        out_shape=[jax.ShapeDtypeStruct((bsz, N_EXPERTS, seq), I32),
                   jax.ShapeDtypeStruct((bsz, N_EXPERTS, seq), F32),
                   jax.ShapeDtypeStruct((bsz, seq, LANES), I32),
                   jax.ShapeDtypeStruct((bsz, N_EXPERTS, LANES), I32)],
        grid=(bsz // n,),
        in_specs=[pl.BlockSpec((n, seq, LANES), lambda b: (b, 0, 0)), _const_spec(tri.shape)],
        out_specs=[pl.BlockSpec((n, N_EXPERTS, seq), lambda b: (b, 0, 0)),
                   pl.BlockSpec((n, N_EXPERTS, seq), lambda b: (b, 0, 0)),
                   pl.BlockSpec((n, seq, LANES), lambda b: (b, 0, 0)),
                   pl.BlockSpec((n, N_EXPERTS, LANES), lambda b: (b, 0, 0))],
        compiler_params=_params("parallel"),
        name="moe_select",
    )(aff, tri)


def _gather_body(lo_ref, hn_ref, post_ref, gatet_ref, xin_ref, gate_ref):
    b = pl.program_id(0)
    eg = pl.program_id(1)
    seq = hn_ref.shape[1]
    cap = xin_ref.shape[2]
    nblk = seq // TOKEN_BLOCK
    xin_ref[...] = jnp.zeros_like(xin_ref)
    gate_ref[...] = jnp.zeros_like(gate_ref)

    def table(i):
        return (b * N_EXPERTS + eg * PACK + i) * LO_STRIDE

    def window(i, ts, s0, rows, iota):
        s0c = pl.multiple_of(jnp.minimum(s0, cap - rows), 16)
        slot = iota + s0c
        hit = (post_ref[0, i, :, ts] == slot) & (slot >= s0)

        def add(contrib):
            xin_ref[0, i, pl.ds(s0c, rows), :] += contrib.astype(BF16)
            gsum = jnp.sum(jnp.where(hit, gatet_ref[0, i, :, ts], 0.0), axis=1, keepdims=True)
            gate_ref[0, i, pl.ds(s0c, rows), :] += jnp.broadcast_to(gsum, (rows, LANES))

        return jnp.where(hit, 1.0, 0.0).astype(BF16), add

    first_iota = lax.broadcasted_iota(I32, (FIRST_WINDOW, TOKEN_BLOCK), 0)
    for j in range(nblk):
        ts = slice(j * TOKEN_BLOCK, (j + 1) * TOKEN_BLOCK)
        rows, adds = zip(*[window(i, ts, _floor_to(lo_ref[table(i) + j], 16), FIRST_WINDOW, first_iota)
                           for i in range(PACK)])
        contrib = _dot(jnp.concatenate(rows, axis=0), hn_ref[0, ts, :])
        for i in range(PACK):
            adds[i](contrib[i * FIRST_WINDOW:(i + 1) * FIRST_WINDOW])

    def left_over(i, j):
        done = _floor_to(lo_ref[table(i) + j], 16) + FIRST_WINDOW
        return jnp.maximum(lo_ref[table(i) + j + 1] - done, 0), done

    most = functools.reduce(jnp.maximum, [left_over(i, j)[0] for i in range(PACK) for j in range(nblk)])

    @pl.when(most > 0)
    def _():
        more_iota = lax.broadcasted_iota(I32, (GATHER_WINDOW, TOKEN_BLOCK), 0)
        for j in range(nblk):
            ts = slice(j * TOKEN_BLOCK, (j + 1) * TOKEN_BLOCK)

            def expert(i, carry, j=j, ts=ts):
                left, done = left_over(i, j)

                def further(w, carry):
                    onehot, add = window(i, ts, done + w * GATHER_WINDOW, GATHER_WINDOW, more_iota)
                    add(_dot(onehot, hn_ref[0, ts, :]))
                    return carry

                return lax.fori_loop(0, _ceil_div(left, GATHER_WINDOW), further, carry)

            lax.fori_loop(0, PACK, expert, 0)


def _gather(lo_flat, hn, post, gatet, cap):
    bsz, seq, d = hn.shape
    post4 = post.reshape(bsz, N_EXPERTS, 1, seq)
    gatet4 = gatet.reshape(bsz, N_EXPERTS, 1, seq)
    grid_spec = pltpu.PrefetchScalarGridSpec(
        num_scalar_prefetch=1,
        grid=(bsz, N_EXPERTS // PACK),
        in_specs=[pl.BlockSpec((1, seq, d), lambda b, e, lo: (b, 0, 0)),
                  pl.BlockSpec((1, PACK, 1, seq), lambda b, e, lo: (b, e, 0, 0)),
                  pl.BlockSpec((1, PACK, 1, seq), lambda b, e, lo: (b, e, 0, 0))],
        out_specs=[pl.BlockSpec((1, PACK, cap, d), lambda b, e, lo: (b, e, 0, 0)),
                   pl.BlockSpec((1, PACK, cap, LANES), lambda b, e, lo: (b, e, 0, 0))],
    )
    return pl.pallas_call(
        _gather_body,
        out_shape=[jax.ShapeDtypeStruct((bsz, N_EXPERTS, cap, d), BF16),
                   jax.ShapeDtypeStruct((bsz, N_EXPERTS, cap, LANES), F32)],
        grid_spec=grid_spec,
        compiler_params=_params("parallel", "arbitrary"),
        name="moe_gather",
    )(lo_flat, hn, post4, gatet4)


def _expert_body(xin_ref, gate_ref, w1c_ref, w3c_ref, w2c_ref, y_ref, w1_s, w3_s, w2_s):
    r = pl.program_id(0)
    step = pl.program_id(1)
    fill = lax.rem(r, 2)

    def cast_slice():
        d_rows = w1c_ref.shape[2]
        f_rows = w2c_ref.shape[2]
        d0 = pl.multiple_of(step * d_rows, d_rows)
        f0 = pl.multiple_of(step * f_rows, f_rows)
        w1_s[fill, pl.ds(d0, d_rows), :] = w1c_ref[0, 0].astype(BF16)
        w3_s[fill, pl.ds(d0, d_rows), :] = w3c_ref[0, 0].astype(BF16)
        w2_s[fill, pl.ds(f0, f_rows), :] = w2c_ref[0, 0].astype(BF16)

    @pl.when(r == 0)
    def _():
        cast_slice()
        y_ref[...] = jnp.zeros_like(y_ref)

    @pl.when(r > 0)
    def _():
        cast_slice()
        use = 1 - fill
        chunk = EXPERT_CHUNK
        for s in range(xin_ref.shape[0]):
            xin = xin_ref[s, 0]
            y = jnp.zeros(y_ref.shape[2:], F32)
            for c in range(w1_s.shape[2] // chunk):
                fs = slice(c * chunk, (c + 1) * chunk)
                hid = jax.nn.silu(_dot(xin, w1_s[use, :, fs])) * _dot(xin, w3_s[use, :, fs])
                y = y + _dot(hid.astype(BF16), w2_s[use, fs, :])
            y_ref[s, 0] = (y * gate_ref[s, 0, :, 0:1]).astype(BF16)


def _experts(layer, xin, gate, w1, w3, w2):
    bsz, ne, cap, d = xin.shape
    f = w1.shape[3]
    seqs = EXPERT_SEQS if bsz % EXPERT_SEQS == 0 else 1
    steps = bsz // seqs
    d_rows, f_rows = d // steps, f // steps
    assert d_rows * steps == d and f_rows * steps == f

    def act(width):
        return pl.BlockSpec((seqs, 1, cap, width), lambda r, b: (b, jnp.maximum(r - 1, 0), 0, 0))

    out_spec = pl.BlockSpec((seqs, 1, cap, d), lambda r, b: (b, r, 0, 0))

    def weight_slice(rows, cols):
        return pl.BlockSpec((1, 1, rows, cols), lambda r, b: (layer, jnp.minimum(r, ne - 1), b, 0))

    return pl.pallas_call(
        _expert_body,
        out_shape=jax.ShapeDtypeStruct((bsz, ne + 1, cap, d), BF16),
        grid=(ne + 1, steps),
        in_specs=[act(d), act(LANES), weight_slice(d_rows, f), weight_slice(d_rows, f), weight_slice(f_rows, d)],
        out_specs=out_spec,
        scratch_shapes=[pltpu.VMEM((2, d, f), BF16), pltpu.VMEM((2, d, f), BF16), pltpu.VMEM((2, f, d), BF16)],
        compiler_params=_params("arbitrary", "arbitrary"),
        name="moe_experts",
    )(xin, gate, w1, w3, w2)


def _combine_body(lo_ref, x_ref, y_ref, pos_ref, o_ref):
    b = pl.program_id(0)
    cap = y_ref.shape[2]
    lane = lax.broadcasted_iota(I32, (TOKEN_BLOCK, PACK * FIRST_WINDOW), 1)
    lane_expert = lane // FIRST_WINDOW
    lane_slot = lane % FIRST_WINDOW
    pending = []
    for sub in range(COMBINE_BLOCKS):
        j = pl.program_id(1) * COMBINE_BLOCKS + sub
        ts = slice(sub * TOKEN_BLOCK, (sub + 1) * TOKEN_BLOCK)
        pos = pos_ref[0, ts, :]

        def bounds(e, j=j):
            base = (b * N_EXPERTS + e) * LO_STRIDE
            return lo_ref[base + j], lo_ref[base + j + 1]

        acc = x_ref[0, ts, :]
        for q in range(N_EXPERTS // PACK):
            target = jnp.full(lane.shape, -1, I32)
            y_rows = []
            for i in range(PACK):
                e = q * PACK + i
                start = _floor_to(bounds(e)[0], 16)
                s0c = pl.multiple_of(jnp.minimum(start, cap - FIRST_WINDOW), 16)
                pcol = pos[:, e:e + 1]
                rel = jnp.where(pcol >= start, pcol - s0c, -1)
                target = jnp.where(lane_expert == i, rel, target)
                y_rows.append(y_ref[0, e + 1, pl.ds(s0c, FIRST_WINDOW), :])
            onehot = jnp.where(target == lane_slot, 1.0, 0.0).astype(BF16)
            acc = acc + _dot(onehot, jnp.concatenate(y_rows, axis=0))
        o_ref[0, ts, :] = acc

        def left_over(e, bounds=bounds):
            lo, hi = bounds(e)
            done = _floor_to(lo, 16) + FIRST_WINDOW
            return jnp.maximum(hi - done, 0), done

        pending.append((ts, pos, left_over))

    for ts, pos, left_over in pending:
        most = functools.reduce(jnp.maximum, [left_over(e)[0] for e in range(N_EXPERTS)])

        @pl.when(most > 0)
        def _(ts=ts, pos=pos, left_over=left_over):
            more_lane = lax.broadcasted_iota(I32, (TOKEN_BLOCK, SCATTER_WINDOW), 1)
            for e in range(N_EXPERTS):
                left, done = left_over(e)
                pcol = pos[:, e:e + 1]

                def window(w, carry, e=e, done=done, pcol=pcol):
                    s0 = done + w * SCATTER_WINDOW
                    s0c = pl.multiple_of(jnp.minimum(s0, cap - SCATTER_WINDOW), 16)
                    hit = ((pcol - s0c) == more_lane) & (pcol >= s0)
                    onehot = jnp.where(hit, 1.0, 0.0).astype(BF16)
                    o_ref[0, ts, :] += _dot(onehot, y_ref[0, e + 1, pl.ds(s0c, SCATTER_WINDOW), :])
                    return carry

                lax.fori_loop(0, _ceil_div(left, SCATTER_WINDOW), window, 0)


def _combine(lo_flat, x, y, pos):
    bsz, seq, d = x.shape
    _, slots, cap, _ = y.shape
    rows = COMBINE_BLOCKS * TOKEN_BLOCK
    grid_spec = pltpu.PrefetchScalarGridSpec(
        num_scalar_prefetch=1,
        grid=(bsz, seq // rows),
        in_specs=[pl.BlockSpec((1, rows, d), lambda b, j, lo: (b, j, 0)),
                  pl.BlockSpec((1, slots, cap, d), lambda b, j, lo: (b, 0, 0, 0)),
                  pl.BlockSpec((1, rows, LANES), lambda b, j, lo: (b, j, 0))],
        out_specs=pl.BlockSpec((1, rows, d), lambda b, j, lo: (b, j, 0)),
    )
    return pl.pallas_call(
        _combine_body,
        out_shape=jax.ShapeDtypeStruct((bsz, seq, d), F32),
        grid_spec=grid_spec,
        compiler_params=_params("parallel", "arbitrary"),
        name="moe_combine",
    )(lo_flat, x, y, pos)


def _ec_moe(layer, x, hn, aff, w1, w3, w2, tri):
    bsz, seq, d = x.shape
    cap = CAPACITY_FACTOR * seq // N_EXPERTS
    post, gatet, pos, lo = _select(aff, tri, cap)
    lo_flat = lo[:, :, :LO_STRIDE].reshape(-1)
    xin, gate = _gather(lo_flat, hn, post, gatet, cap)
    y = _experts(layer, xin, gate, w1, w3, w2)
    return _combine(lo_flat, x, y, pos)


def _group_major(a):
    lead = a.shape[:-1]
    a = a.reshape(lead + (3, N_GROUPS, ATTN_WIDTH))
    return jnp.swapaxes(a, -3, -2).reshape(lead + (3 * N_GROUPS * ATTN_WIDTH,))


def kernel(x, mix_norm, ffn_norm, gm_w_in, gm_b_in, gm_v_norm, gm_w_s, gm_b_s, gm_w_out, gm_b_out,
           at_w_qkv, at_q_norm, at_k_norm, at_w_o, moe_w_router, moe_b_router, moe_w1, moe_w3, moe_w2):
    bsz, seq, d = x.shape
    depth = mix_norm.shape[0]
    n_mixers = 2

    lane_head = jnp.arange(MXU_DIM) // HEAD_DIM
    head_ones = (lane_head[:, None] == lane_head[None, :]).astype(BF16)
    tri = (jnp.arange(TOKEN_BLOCK)[:, None] < jnp.arange(TOKEN_BLOCK)[None, :]).astype(BF16)
    biases = [_attn_bias(dil) for _, dil in DIL_PATTERNS]
    pad_e = LANES - N_EXPERTS

    for i in range(depth):
        j = i // n_mixers
        gain = mix_norm[i][None, :]
        w_r = jnp.pad(moe_w_router[i], ((0, 0), (0, pad_e)))
        w_hi = w_r.astype(BF16)
        w_split = jnp.concatenate([w_hi, (w_r - w_hi.astype(F32)).astype(BF16)], axis=1)
        b_r = jnp.pad(moe_b_router[i], (0, pad_e), constant_values=NEG_INF)[None, :]
        route = (ffn_norm[i][None, :], w_split, b_r)
        if i % n_mixers == 0:
            x, hn, aff = _gmlp_layer(
                x.reshape(bsz * seq, d), gain, gm_w_in[j].astype(BF16), gm_b_in[j][None, :],
                gm_v_norm[j][None, :], gm_w_s[j].astype(BF16), gm_b_s[j].T, gm_w_out[j].astype(BF16),
                gm_b_out[j][None, :], route)
        else:
            def per_head(g):
                return jnp.tile(g[:, None, :], (1, HEADS, 1)).reshape(N_GROUPS, ATTN_WIDTH)
            qk_gain = jnp.stack([per_head(at_q_norm[j]) * (HEAD_DIM ** -0.5), per_head(at_k_norm[j]),
                                 jnp.ones((N_GROUPS, ATTN_WIDTH), F32)]).reshape(1, -1)
            qkvs = _qkv_proj(x, gain, _group_major(at_w_qkv[j]).astype(BF16), _group_major(qk_gain), head_ones)
            outs = [_attn_group(g, qkvs[g], biases[g]) for g in range(N_GROUPS)]
            x, hn, aff = _attn_merge(outs, x, at_w_o[j].astype(BF16), route)
        x = _ec_moe(i, x.reshape(bsz, seq, d), hn.reshape(bsz, seq, d), aff.reshape(bsz, seq, LANES),
                    moe_w1, moe_w3, moe_w2, tri)
    return x
```

```python
import functools

import jax
import jax.numpy as jnp
from jax import lax
from jax.experimental import pallas as pl
from jax.experimental.pallas import tpu as pltpu

F32 = jnp.float32
BF16 = jnp.bfloat16
I32 = jnp.int32

EPS = 1e-6
NEG_INF = -1e30

LANES = 128
MXU_DIM = 256
BF16_ROWS = 16
VMEM_LIMIT_BYTES = 56 * 1024 * 1024

GMLP_CHUNK = 128
GMLP_GROUPS = 8
DIL_PATTERNS = ((128, 1), (512, 4), (2048, 16))
N_GROUPS = len(DIL_PATTERNS)
HEADS = 8
HEAD_DIM = 64
ATTN_WIDTH = HEADS * HEAD_DIM
GROUP_WIDTH = 3 * ATTN_WIDTH
QUERY_BLOCK = 128
HALF_WINDOW = 64
N_EXPERTS = 16
CAPACITY_FACTOR = 2

GMLP_ROWS = 512
QKV_ROWS = 512
MERGE_ROWS = 512
ATTN_BLOCKS = 16
TOKEN_BLOCK = MXU_DIM
SELECT_SEQS = 2
COMBINE_BLOCKS = 2
PACK = 4
FIRST_WINDOW = MXU_DIM // PACK
GATHER_WINDOW = 128
SCATTER_WINDOW = MXU_DIM
LO_STRIDE = 32
EXPERT_SEQS = 2
EXPERT_CHUNK = 2 * MXU_DIM


def _params(*semantics):
    return pltpu.CompilerParams(dimension_semantics=semantics, vmem_limit_bytes=VMEM_LIMIT_BYTES)


def _const_spec(shape):
    zeros = (0,) * len(shape)
    return pl.BlockSpec(shape, lambda *_: zeros, pipeline_mode=pl.Buffered(1))


def _rms(x, gain):
    return x * lax.rsqrt(jnp.mean(x * x, axis=-1, keepdims=True) + EPS) * gain


def _dot(a, b):
    return jnp.dot(a, b, preferred_element_type=F32)


def _floor_to(x, multiple):
    shift = multiple.bit_length() - 1
    return lax.shift_left(lax.shift_right_logical(x, shift), shift)


def _ceil_div(x, divisor):
    return lax.shift_right_logical(x + (divisor - 1), divisor.bit_length() - 1)


def _route(x, g_ref, wr_ref, br_ref, hn_ref, aff_ref):
    hn = _rms(x, g_ref[...])
    hi = hn.astype(BF16)
    lo = (hn - hi.astype(F32)).astype(BF16)
    hn_ref[...] = hi
    both = _dot(hi, wr_ref[...])
    logits = both[:, :LANES] + both[:, LANES:] + _dot(lo, wr_ref[:, :LANES]) + br_ref[...]
    m = jnp.max(logits, axis=-1, keepdims=True)
    e = jnp.exp(logits - m)
    aff_ref[...] = e / jnp.sum(e, axis=-1, keepdims=True)


def _gmlp_body(x_ref, g_ref, win_ref, bin_ref, vg_ref, ws_ref, bst_ref, wout_ref, bout_ref,
               rg_ref, wr_ref, br_ref, o_ref, hn_ref, aff_ref, v_s, uv_s):
    x = x_ref[...]
    rows, width = v_s.shape
    gw = width // GMLP_GROUPS
    h = _rms(x, g_ref[...]).astype(BF16)
    zv = jax.nn.gelu(_dot(h, win_ref[:, width:]) + bin_ref[:, width:])
    r = lax.rsqrt(jnp.mean(zv * zv, axis=-1, keepdims=True) + EPS)
    v_s[...] = (zv * r * vg_ref[...]).astype(BF16)
    pair = 2 * gw
    for p in range(GMLP_GROUPS // 2):
        zu = jax.nn.gelu(_dot(h, win_ref[:, p * pair:(p + 1) * pair]) + bin_ref[:, p * pair:(p + 1) * pair])
        for gl in range(2):
            g = 2 * p + gl
            cols = slice(g * gw, (g + 1) * gw)
            bias = bst_ref[:, g:g + 1]
            for c in range(rows // GMLP_CHUNK):
                rs = slice(c * GMLP_CHUNK, (c + 1) * GMLP_CHUNK)
                sv = _dot(ws_ref[g], v_s[rs, cols]) + bias
                uv_s[rs, cols] = (zu[rs, gl * gw:(gl + 1) * gw] * sv).astype(BF16)
    out = x + _dot(uv_s[...], wout_ref[...]) + bout_ref[...]
    o_ref[...] = out
    _route(out, rg_ref, wr_ref, br_ref, hn_ref, aff_ref)


def _gmlp_layer(x2, gain, w_in, b_in, v_gain, w_s, b_st, w_out, b_out, route):
    n, d = x2.shape
    width = w_out.shape[0]
    rows = GMLP_ROWS
    return pl.pallas_call(
        _gmlp_body,
        out_shape=[jax.ShapeDtypeStruct((n, d), F32), jax.ShapeDtypeStruct((n, d), BF16),
                   jax.ShapeDtypeStruct((n, LANES), F32)],
        grid=(n // rows,),
        in_specs=[
            pl.BlockSpec((rows, d), lambda i: (i, 0)),
            _const_spec(gain.shape), _const_spec(w_in.shape), _const_spec(b_in.shape),
            _const_spec(v_gain.shape), _const_spec(w_s.shape), _const_spec(b_st.shape),
            _const_spec(w_out.shape), _const_spec(b_out.shape),
        ] + [_const_spec(a.shape) for a in route],
        out_specs=[pl.BlockSpec((rows, d), lambda i: (i, 0)), pl.BlockSpec((rows, d), lambda i: (i, 0)),
                   pl.BlockSpec((rows, LANES), lambda i: (i, 0))],
        scratch_shapes=[pltpu.VMEM((rows, width), BF16), pltpu.VMEM((rows, width), BF16)],
        compiler_params=_params("parallel"),
        name="gmlp_layer",
    )(x2, gain, w_in, b_in, v_gain, w_s, b_st, w_out, b_out, *route)


def _qkv_body(x_ref, g_ref, w_ref, gain_ref, bd_ref, *rest):
    out_refs, t_s = rest[:N_GROUPS], rest[N_GROUPS]
    rows = t_s.shape[1]
    h = _rms(x_ref[...], g_ref[...]).astype(BF16)
    bd = bd_ref[...]

    def put(col0, t):
        for c in range(t.shape[1] // LANES):
            t_s[col0 // LANES + c] = t[:, c * LANES:(c + 1) * LANES]

    for g in range(N_GROUPS):
        tg = _dot(h, w_ref[:, g * GROUP_WIDTH:(g + 1) * GROUP_WIDTH])
        for c in range(2 * ATTN_WIDTH // MXU_DIM):
            col0 = g * GROUP_WIDTH + c * MXU_DIM
            t = tg[:, c * MXU_DIM:(c + 1) * MXU_DIM]
            t2 = t * t
            hi = t2.astype(BF16)
            lo = (t2 - hi.astype(F32)).astype(BF16)
            ss = _dot(hi, bd) + _dot(lo, bd)
            put(col0, t * lax.rsqrt(ss * (1.0 / HEAD_DIM) + EPS) * gain_ref[:, col0:col0 + MXU_DIM])
        put(g * GROUP_WIDTH + 2 * ATTN_WIDTH, tg[:, 2 * ATTN_WIDTH:])
    for g, (_, dil) in enumerate(DIL_PATTERNS):
        for r in range(dil):
            rs = slice(None) if dil == 1 else pl.ds(r, rows // dil, stride=dil)
            for c in range(GROUP_WIDTH // LANES):
                out_refs[g][0, r, :, c * LANES:(c + 1) * LANES] = (
                    t_s[g * (GROUP_WIDTH // LANES) + c, rs, :].astype(BF16))


def _qkv_proj(x, gain, w_qkv, qk_gain, head_ones):
    bsz, seq, d = x.shape
    width = w_qkv.shape[1]
    rows = QKV_ROWS
    out_shape, out_specs = [], []
    for _, dil in DIL_PATTERNS:
        out_shape.append(jax.ShapeDtypeStruct((bsz, dil, seq // dil, GROUP_WIDTH), BF16))
        out_specs.append(pl.BlockSpec((1, dil, rows // dil, GROUP_WIDTH), lambda b, i: (b, 0, i, 0)))
    return pl.pallas_call(
        _qkv_body,
        out_shape=out_shape,
        grid=(bsz, seq // rows),
        in_specs=[
            pl.BlockSpec((None, rows, d), lambda b, i: (b, i, 0)),
            _const_spec(gain.shape), _const_spec(w_qkv.shape), _const_spec(qk_gain.shape),
            _const_spec(head_ones.shape),
        ],
        out_specs=out_specs,
        scratch_shapes=[pltpu.VMEM((width // LANES, rows, LANES), F32)],
        compiler_params=_params("parallel", "parallel"),
        name="qkv_proj",
    )(x, gain, w_qkv, qk_gain, head_ones)


def _attn_body(q_ref, kp_ref, kc_ref, kn_ref, vp_ref, vc_ref, vn_ref, bias_ref, o_ref, l_ref):
    hw = HALF_WINDOW
    qb = QUERY_BLOCK
    n_sub = q_ref.shape[2] // qb
    last_block = pl.num_programs(2) * n_sub - 1
    heads_per_tile = MXU_DIM // HEAD_DIM
    head_of_lane = lax.broadcasted_iota(I32, (qb, MXU_DIM), 1) // HEAD_DIM
    lane = lax.broadcasted_iota(I32, (qb, LANES), 1)
    for rr in range(q_ref.shape[1]):
        k_all = jnp.concatenate([kp_ref[0, rr, qb - hw:, :], kc_ref[0, rr], kn_ref[0, rr, :hw, :]], axis=0)
        v_all = jnp.concatenate([vp_ref[0, rr, qb - hw:, :], vc_ref[0, rr], vn_ref[0, rr, :hw, :]], axis=0)
        for sub in range(n_sub):
            block = pl.program_id(2) * n_sub + sub
            variant = jnp.where(block == 0, 0, jnp.where(block == last_block, 2, 1))
            q = q_ref[0, rr, sub * qb:(sub + 1) * qb, :]
            k = k_all[sub * qb:sub * qb + qb + 2 * hw]
            v = v_all[sub * qb:sub * qb + qb + 2 * hw]
            lse_tile = jnp.zeros((qb, LANES), F32)
            for half in range(ATTN_WIDTH // MXU_DIM):
                cs = slice(half * MXU_DIM, (half + 1) * MXU_DIM)
                qh, kh, vh = q[:, cs], k[:, cs], v[:, cs]
                qs = jnp.concatenate(
                    [jnp.where(head_of_lane == hh, qh, jnp.zeros_like(qh)) for hh in range(heads_per_tile)], axis=0)
                s = lax.dot_general(qs, kh, (((1,), (1,)), ((), ())), preferred_element_type=F32)
                s = s + bias_ref[variant, half]
                m = jnp.max(s, axis=-1, keepdims=True)
                p = jnp.exp(s - m)
                l = jnp.sum(p, axis=-1, keepdims=True)
                pv = _dot(p.astype(BF16), vh) / l
                lse = m + jnp.log(l)
                o_half = jnp.zeros((qb, MXU_DIM), F32)
                for hh in range(heads_per_tile):
                    rs = slice(hh * qb, (hh + 1) * qb)
                    o_half = jnp.where(head_of_lane == hh, pv[rs], o_half)
                    lse_tile = jnp.where(lane == half * heads_per_tile + hh, lse[rs], lse_tile)
                o_ref[0, rr, sub * qb:(sub + 1) * qb, cs] = o_half.astype(BF16)
            l_ref[0, rr, sub * qb:(sub + 1) * qb, :] = lse_tile


def _attn_group(g, qkv_g, bias):
    bsz, dil, sub_len, _ = qkv_g.shape
    nb = sub_len // QUERY_BLOCK
    per_step = min(ATTN_BLOCKS, nb)
    rows = per_step * QUERY_BLOCK
    res = min(ATTN_BLOCKS // per_step, dil)

    def main(which):
        return pl.BlockSpec((1, res, rows, ATTN_WIDTH), lambda b, r, i: (b, r, i, which))

    def halo(which, offset):
        def index(b, r, i):
            return (b, r, jnp.clip(i * per_step + offset, 0, nb - 1), which)
        return pl.BlockSpec((1, res, QUERY_BLOCK, ATTN_WIDTH), index)

    return pl.pallas_call(
        _attn_body,
        out_shape=[jax.ShapeDtypeStruct((bsz, dil, sub_len, ATTN_WIDTH), BF16),
                   jax.ShapeDtypeStruct((bsz, dil, sub_len, LANES), F32)],
        grid=(bsz, dil // res, nb // per_step),
        in_specs=[main(0), halo(1, -1), main(1), halo(1, per_step), halo(2, -1), main(2), halo(2, per_step),
                  _const_spec(bias.shape)],
        out_specs=[main(0), pl.BlockSpec((1, res, rows, LANES), lambda b, r, i: (b, r, i, 0))],
        compiler_params=_params("parallel", "parallel", "arbitrary"),
        name=f"dilated_attn_g{g}",
    )(*([qkv_g] * 7), bias)


def _attn_bias(dil):
    slopes = jnp.exp2(-8.0 * jnp.arange(1, HEADS + 1, dtype=F32) / HEADS)
    kb = QUERY_BLOCK + 2 * HALF_WINDOW
    rel = jnp.arange(kb)[None, :] - HALF_WINDOW - jnp.arange(QUERY_BLOCK)[:, None]
    band = jnp.abs(rel) <= HALF_WINDOW
    alibi = -slopes[:, None, None] * (dil * jnp.abs(rel)).astype(F32)[None]
    j = jnp.arange(kb)
    edge = jnp.stack([j >= HALF_WINDOW, j >= 0, j < kb - HALF_WINDOW])
    ok = band[None, None] & edge[:, None, None, :]
    tab = jnp.where(ok, alibi[None], NEG_INF)
    per_tile = MXU_DIM // HEAD_DIM
    return tab.reshape(3, HEADS // per_tile, per_tile * QUERY_BLOCK, kb)


def _merge_body(*refs):
    o_refs, l_refs = refs[:N_GROUPS], refs[N_GROUPS:2 * N_GROUPS]
    x_ref, wo_ref, spread_ref, rg_ref, wr_ref, br_ref, out_ref, hn_ref, aff_ref, o_s, l_s = refs[2 * N_GROUPS:]
    rows = x_ref.shape[0]
    tiles = ATTN_WIDTH // LANES
    for g, (_, dil) in enumerate(DIL_PATTERNS):
        for r in range(dil):
            rs = slice(None) if dil == 1 else pl.ds(r, rows // dil, stride=dil)
            l_s[g, rs, :] = l_refs[g][0, r]
            for c in range(tiles):
                o_s[g * tiles + c, rs, :] = o_refs[g][0, r, :, c * LANES:(c + 1) * LANES].astype(F32)
    lses = [l_s[g] for g in range(N_GROUPS)]
    mx = functools.reduce(jnp.maximum, lses)
    es = [jnp.exp(l - mx) for l in lses]
    den = functools.reduce(lambda a, b: a + b, es)
    merged = [jnp.zeros((rows, LANES), F32) for _ in range(tiles)]
    for g in range(N_GROUPS):
        w = es[g] / den
        hi = w.astype(BF16)
        lo = (w - hi.astype(F32)).astype(BF16)
        wide = _dot(hi, spread_ref[...]) + _dot(lo, spread_ref[...])
        for c in range(tiles):
            merged[c] = merged[c] + wide[:, c * LANES:(c + 1) * LANES] * o_s[g * tiles + c]
    merged = [m.astype(BF16) for m in merged]
    out = x_ref[...] + _dot(jnp.concatenate(merged, axis=1), wo_ref[...])
    out_ref[...] = out
    _route(out, rg_ref, wr_ref, br_ref, hn_ref, aff_ref)


def _attn_merge(outs, x, w_o, route):
    bsz, seq, d = x.shape
    rows = MERGE_ROWS
    args, in_specs = [], []
    for which, width in enumerate((ATTN_WIDTH, LANES)):
        for g, (_, dil) in enumerate(DIL_PATTERNS):
            args.append(outs[g][which])
            in_specs.append(pl.BlockSpec((1, dil, rows // dil, width), lambda b, i: (b, 0, i, 0)))
    lane_head = jnp.arange(ATTN_WIDTH) // HEAD_DIM
    spread = (jnp.arange(LANES)[:, None] == lane_head[None, :]).astype(BF16)
    args += [x, w_o, spread, *route]
    in_specs += [pl.BlockSpec((None, rows, d), lambda b, i: (b, i, 0)), _const_spec(w_o.shape),
                 _const_spec(spread.shape)]
    in_specs += [_const_spec(a.shape) for a in route]

    def row_block(width):
        return pl.BlockSpec((None, rows, width), lambda b, i: (b, i, 0))

    return pl.pallas_call(
        _merge_body,
        out_shape=[jax.ShapeDtypeStruct((bsz, seq, d), F32), jax.ShapeDtypeStruct((bsz, seq, d), BF16),
                   jax.ShapeDtypeStruct((bsz, seq, LANES), F32)],
        grid=(bsz, seq // rows),
        in_specs=in_specs,
        out_specs=[row_block(d), row_block(d), row_block(LANES)],
        scratch_shapes=[pltpu.VMEM((N_GROUPS * ATTN_WIDTH // LANES, rows, LANES), F32),
                        pltpu.VMEM((N_GROUPS, rows, LANES), F32)],
        compiler_params=_params("parallel", "parallel"),
        name="attn_merge",
    )(*args)


def _select_one(cap, s, aff_ref, tri_ref, post_ref, gatet_ref, pos_ref, lo_ref):
    seq = aff_ref.shape[1]
    aff_t = aff_ref[s].T[:N_EXPERTS]
    key = pltpu.bitcast(aff_t, I32)
    thr = jnp.zeros((N_EXPERTS, 1), I32)
    for bit in range(30, -1, -1):
        cand = thr | (1 << bit)
        cnt = jnp.sum(jnp.where(key >= cand, 1.0, 0.0), axis=1, keepdims=True)
        thr = jnp.where(cnt >= cap, cand, thr)
    gt = key > thr
    eq = key == thr
    need = cap - jnp.sum(jnp.where(gt, 1.0, 0.0), axis=1, keepdims=True)
    tri = tri_ref[...]
    nblk = seq // TOKEN_BLOCK

    def prefix(mask_f, want_offsets):
        off = jnp.zeros((N_EXPERTS, 1), F32)
        parts, offs = [], []
        for jb in range(nblk):
            blk = mask_f[:, jb * TOKEN_BLOCK:(jb + 1) * TOKEN_BLOCK]
            offs.append(off)
            parts.append(_dot(blk.astype(BF16), tri) + off)
            off = off + jnp.sum(blk, axis=1, keepdims=True)
        offs.append(off)
        return jnp.concatenate(parts, axis=1), (offs if want_offsets else None)

    eq_rank, _ = prefix(jnp.where(eq, 1.0, 0.0), False)
    sel = gt | (eq & (eq_rank < need))
    rank, offs = prefix(jnp.where(sel, 1.0, 0.0), True)
    post = jnp.where(sel, rank, -1.0)
    post_ref[s] = post.astype(I32)
    gatet_ref[s] = jnp.where(sel, aff_t, 0.0)
    padded = jnp.concatenate([post, jnp.full((LANES - N_EXPERTS, seq), -1.0, F32)], axis=0)
    pos_ref[s] = padded.T.astype(I32)
    lane = lax.broadcasted_iota(I32, (N_EXPERTS, LANES), 1)
    lo = jnp.zeros((N_EXPERTS, LANES), F32)
    for jb, off in enumerate(offs):
        lo = jnp.where(lane == jb, off, lo)
    lo_ref[s] = lo.astype(I32)


def _select_body(cap, aff_ref, *refs):
    for s in range(aff_ref.shape[0]):
        _select_one(cap, s, aff_ref, *refs)


def _select(aff, tri, cap):
    bsz, seq, _ = aff.shape
    n = SELECT_SEQS if bsz % SELECT_SEQS == 0 else 1
    return pl.pallas_call(
        functools.partial(_select_body, cap),
        out_shape=[jax.ShapeDtypeStruct((bsz, N_EXPERTS, seq), I32),
                   jax.ShapeDtypeStruct((bsz, N_EXPERTS, seq), F32),
                   jax.ShapeDtypeStruct((bsz, seq, LANES), I32),
                   jax.ShapeDtypeStruct((bsz, N_EXPERTS, LANES), I32)],
        grid=(bsz // n,),
        in_specs=[pl.BlockSpec((n, seq, LANES), lambda b: (b, 0, 0)), _const_spec(tri.shape)],
        out_specs=[pl.BlockSpec((n, N_EXPERTS, seq), lambda b: (b, 0, 0)),
                   pl.BlockSpec((n, N_EXPERTS, seq), lambda b: (b, 0, 0)),
                   pl.BlockSpec((n, seq, LANES), lambda b: (b, 0, 0)),
                   pl.BlockSpec((n, N_EXPERTS, LANES), lambda b: (b, 0, 0))],
        compiler_params=_params("parallel"),
        name="moe_select",
    )(aff, tri)


def _gather_body(lo_ref, hn_ref, post_ref, gatet_ref, xin_ref, gate_ref):
    b = pl.program_id(0)
    eg = pl.program_id(1)
    seq = hn_ref.shape[1]
    cap = xin_ref.shape[2]
    nblk = seq // TOKEN_BLOCK
    xin_ref[...] = jnp.zeros_like(xin_ref)
    gate_ref[...] = jnp.zeros_like(gate_ref)

    def table(i):
        return (b * N_EXPERTS + eg * PACK + i) * LO_STRIDE

    def window(i, ts, s0, rows, iota):
        s0c = pl.multiple_of(jnp.minimum(s0, cap - rows), BF16_ROWS)
        slot = iota + s0c
        hit = (post_ref[0, i, :, ts] == slot) & (slot >= s0)

        def add(contrib):
            xin_ref[0, i, pl.ds(s0c, rows), :] += contrib.astype(BF16)
            gsum = jnp.sum(jnp.where(hit, gatet_ref[0, i, :, ts], 0.0), axis=1, keepdims=True)
            gate_ref[0, i, pl.ds(s0c, rows), :] += jnp.broadcast_to(gsum, (rows, LANES))

        return jnp.where(hit, 1.0, 0.0).astype(BF16), add

    first_iota = lax.broadcasted_iota(I32, (FIRST_WINDOW, TOKEN_BLOCK), 0)
    for j in range(nblk):
        ts = slice(j * TOKEN_BLOCK, (j + 1) * TOKEN_BLOCK)
        rows, adds = zip(*[window(i, ts, _floor_to(lo_ref[table(i) + j], BF16_ROWS), FIRST_WINDOW, first_iota)
                           for i in range(PACK)])
        contrib = _dot(jnp.concatenate(rows, axis=0), hn_ref[0, ts, :])
        for i in range(PACK):
            adds[i](contrib[i * FIRST_WINDOW:(i + 1) * FIRST_WINDOW])

    def left_over(i, j):
        done = _floor_to(lo_ref[table(i) + j], BF16_ROWS) + FIRST_WINDOW
        return jnp.maximum(lo_ref[table(i) + j + 1] - done, 0), done

    most = functools.reduce(jnp.maximum, [left_over(i, j)[0] for i in range(PACK) for j in range(nblk)])

    @pl.when(most > 0)
    def _():
        more_iota = lax.broadcasted_iota(I32, (GATHER_WINDOW, TOKEN_BLOCK), 0)
        for j in range(nblk):
            ts = slice(j * TOKEN_BLOCK, (j + 1) * TOKEN_BLOCK)

            def expert(i, carry, j=j, ts=ts):
                left, done = left_over(i, j)

                def further(w, carry):
                    onehot, add = window(i, ts, done + w * GATHER_WINDOW, GATHER_WINDOW, more_iota)
                    add(_dot(onehot, hn_ref[0, ts, :]))
                    return carry

                return lax.fori_loop(0, _ceil_div(left, GATHER_WINDOW), further, carry)

            lax.fori_loop(0, PACK, expert, 0)


def _gather(lo_flat, hn, post, gatet, cap):
    bsz, seq, d = hn.shape
    post4 = post.reshape(bsz, N_EXPERTS, 1, seq)
    gatet4 = gatet.reshape(bsz, N_EXPERTS, 1, seq)
    grid_spec = pltpu.PrefetchScalarGridSpec(
        num_scalar_prefetch=1,
        grid=(bsz, N_EXPERTS // PACK),
        in_specs=[pl.BlockSpec((1, seq, d), lambda b, e, lo: (b, 0, 0)),
                  pl.BlockSpec((1, PACK, 1, seq), lambda b, e, lo: (b, e, 0, 0)),
                  pl.BlockSpec((1, PACK, 1, seq), lambda b, e, lo: (b, e, 0, 0))],
        out_specs=[pl.BlockSpec((1, PACK, cap, d), lambda b, e, lo: (b, e, 0, 0)),
                   pl.BlockSpec((1, PACK, cap, LANES), lambda b, e, lo: (b, e, 0, 0))],
    )
    return pl.pallas_call(
        _gather_body,
        out_shape=[jax.ShapeDtypeStruct((bsz, N_EXPERTS, cap, d), BF16),
                   jax.ShapeDtypeStruct((bsz, N_EXPERTS, cap, LANES), F32)],
        grid_spec=grid_spec,
        compiler_params=_params("parallel", "arbitrary"),
        name="moe_gather",
    )(lo_flat, hn, post4, gatet4)


def _expert_body(xin_ref, gate_ref, w1c_ref, w3c_ref, w2c_ref, y_ref, w1_s, w3_s, w2_s):
    r = pl.program_id(0)
    step = pl.program_id(1)
    fill = lax.rem(r, 2)

    def cast_slice():
        d_rows = w1c_ref.shape[2]
        f_rows = w2c_ref.shape[2]
        d0 = pl.multiple_of(step * d_rows, d_rows)
        f0 = pl.multiple_of(step * f_rows, f_rows)
        w1_s[fill, pl.ds(d0, d_rows), :] = w1c_ref[0, 0].astype(BF16)
        w3_s[fill, pl.ds(d0, d_rows), :] = w3c_ref[0, 0].astype(BF16)
        w2_s[fill, pl.ds(f0, f_rows), :] = w2c_ref[0, 0].astype(BF16)

    @pl.when(r == 0)
    def _():
        cast_slice()
        y_ref[...] = jnp.zeros_like(y_ref)

    @pl.when(r > 0)
    def _():
        cast_slice()
        use = 1 - fill
        chunk = EXPERT_CHUNK
        for s in range(xin_ref.shape[0]):
            xin = xin_ref[s, 0]
            y = jnp.zeros(y_ref.shape[2:], F32)
            for c in range(w1_s.shape[2] // chunk):
                fs = slice(c * chunk, (c + 1) * chunk)
                hid = jax.nn.silu(_dot(xin, w1_s[use, :, fs])) * _dot(xin, w3_s[use, :, fs])
                y = y + _dot(hid.astype(BF16), w2_s[use, fs, :])
            y_ref[s, 0] = (y * gate_ref[s, 0, :, 0:1]).astype(BF16)


def _experts(layer, xin, gate, w1, w3, w2):
    bsz, ne, cap, d = xin.shape
    f = w1.shape[3]
    seqs = EXPERT_SEQS if bsz % EXPERT_SEQS == 0 else 1
    steps = bsz // seqs
    d_rows, f_rows = d // steps, f // steps
    assert d_rows * steps == d and f_rows * steps == f

    def act(width):
        return pl.BlockSpec((seqs, 1, cap, width), lambda r, b: (b, jnp.maximum(r - 1, 0), 0, 0))

    out_spec = pl.BlockSpec((seqs, 1, cap, d), lambda r, b: (b, r, 0, 0))

    def weight_slice(rows, cols):
        return pl.BlockSpec((1, 1, rows, cols), lambda r, b: (layer, jnp.minimum(r, ne - 1), b, 0))

    return pl.pallas_call(
        _expert_body,
        out_shape=jax.ShapeDtypeStruct((bsz, ne + 1, cap, d), BF16),
        grid=(ne + 1, steps),
        in_specs=[act(d), act(LANES), weight_slice(d_rows, f), weight_slice(d_rows, f), weight_slice(f_rows, d)],
        out_specs=out_spec,
        scratch_shapes=[pltpu.VMEM((2, d, f), BF16), pltpu.VMEM((2, d, f), BF16), pltpu.VMEM((2, f, d), BF16)],
        compiler_params=_params("arbitrary", "arbitrary"),
        name="moe_experts",
    )(xin, gate, w1, w3, w2)


def _combine_body(lo_ref, x_ref, y_ref, pos_ref, o_ref):
    b = pl.program_id(0)
    cap = y_ref.shape[2]
    lane = lax.broadcasted_iota(I32, (TOKEN_BLOCK, PACK * FIRST_WINDOW), 1)
    lane_expert = lane // FIRST_WINDOW
    lane_slot = lane % FIRST_WINDOW
    pending = []
    for sub in range(COMBINE_BLOCKS):
        j = pl.program_id(1) * COMBINE_BLOCKS + sub
        ts = slice(sub * TOKEN_BLOCK, (sub + 1) * TOKEN_BLOCK)
        pos = pos_ref[0, ts, :]

        def bounds(e, j=j):
            base = (b * N_EXPERTS + e) * LO_STRIDE
            return lo_ref[base + j], lo_ref[base + j + 1]

        acc = x_ref[0, ts, :]
        for q in range(N_EXPERTS // PACK):
            target = jnp.full(lane.shape, -1, I32)
            y_rows = []
            for i in range(PACK):
                e = q * PACK + i
                start = _floor_to(bounds(e)[0], BF16_ROWS)
                s0c = pl.multiple_of(jnp.minimum(start, cap - FIRST_WINDOW), BF16_ROWS)
                pcol = pos[:, e:e + 1]
                rel = jnp.where(pcol >= start, pcol - s0c, -1)
                target = jnp.where(lane_expert == i, rel, target)
                y_rows.append(y_ref[0, e + 1, pl.ds(s0c, FIRST_WINDOW), :])
            onehot = jnp.where(target == lane_slot, 1.0, 0.0).astype(BF16)
            acc = acc + _dot(onehot, jnp.concatenate(y_rows, axis=0))
        o_ref[0, ts, :] = acc

        def left_over(e, bounds=bounds):
            lo, hi = bounds(e)
            done = _floor_to(lo, BF16_ROWS) + FIRST_WINDOW
            return jnp.maximum(hi - done, 0), done

        pending.append((ts, pos, left_over))

    for ts, pos, left_over in pending:
        most = functools.reduce(jnp.maximum, [left_over(e)[0] for e in range(N_EXPERTS)])

        @pl.when(most > 0)
        def _(ts=ts, pos=pos, left_over=left_over):
            more_lane = lax.broadcasted_iota(I32, (TOKEN_BLOCK, SCATTER_WINDOW), 1)
            for e in range(N_EXPERTS):
                left, done = left_over(e)
                pcol = pos[:, e:e + 1]

                def window(w, carry, e=e, done=done, pcol=pcol):
                    s0 = done + w * SCATTER_WINDOW
                    s0c = pl.multiple_of(jnp.minimum(s0, cap - SCATTER_WINDOW), BF16_ROWS)
                    hit = ((pcol - s0c) == more_lane) & (pcol >= s0)
                    onehot = jnp.where(hit, 1.0, 0.0).astype(BF16)
                    o_ref[0, ts, :] += _dot(onehot, y_ref[0, e + 1, pl.ds(s0c, SCATTER_WINDOW), :])
                    return carry

                lax.fori_loop(0, _ceil_div(left, SCATTER_WINDOW), window, 0)


def _combine(lo_flat, x, y, pos):
    bsz, seq, d = x.shape
    _, slots, cap, _ = y.shape
    rows = COMBINE_BLOCKS * TOKEN_BLOCK
    grid_spec = pltpu.PrefetchScalarGridSpec(
        num_scalar_prefetch=1,
        grid=(bsz, seq // rows),
        in_specs=[pl.BlockSpec((1, rows, d), lambda b, j, lo: (b, j, 0)),
                  pl.BlockSpec((1, slots, cap, d), lambda b, j, lo: (b, 0, 0, 0)),
                  pl.BlockSpec((1, rows, LANES), lambda b, j, lo: (b, j, 0))],
        out_specs=pl.BlockSpec((1, rows, d), lambda b, j, lo: (b, j, 0)),
    )
    return pl.pallas_call(
        _combine_body,
        out_shape=jax.ShapeDtypeStruct((bsz, seq, d), F32),
        grid_spec=grid_spec,
        compiler_params=_params("parallel", "arbitrary"),
        name="moe_combine",
    )(lo_flat, x, y, pos)


def _ec_moe(layer, x, hn, aff, w1, w3, w2, tri):
    bsz, seq, d = x.shape
    cap = CAPACITY_FACTOR * seq // N_EXPERTS
    post, gatet, pos, lo = _select(aff, tri, cap)
    lo_flat = lo[:, :, :LO_STRIDE].reshape(-1)
    xin, gate = _gather(lo_flat, hn, post, gatet, cap)
    y = _experts(layer, xin, gate, w1, w3, w2)
    return _combine(lo_flat, x, y, pos)


def _group_major(a):
    lead = a.shape[:-1]
    a = a.reshape(lead + (3, N_GROUPS, ATTN_WIDTH))
    return jnp.swapaxes(a, -3, -2).reshape(lead + (3 * N_GROUPS * ATTN_WIDTH,))


def kernel(x, mix_norm, ffn_norm, gm_w_in, gm_b_in, gm_v_norm, gm_w_s, gm_b_s, gm_w_out, gm_b_out,
           at_w_qkv, at_q_norm, at_k_norm, at_w_o, moe_w_router, moe_b_router, moe_w1, moe_w3, moe_w2):
    bsz, seq, d = x.shape
    depth = mix_norm.shape[0]
    n_mixers = 2

    lane_head = jnp.arange(MXU_DIM) // HEAD_DIM
    head_ones = (lane_head[:, None] == lane_head[None, :]).astype(BF16)
    tri = (jnp.arange(TOKEN_BLOCK)[:, None] < jnp.arange(TOKEN_BLOCK)[None, :]).astype(BF16)
    biases = [_attn_bias(dil) for _, dil in DIL_PATTERNS]
    pad_e = LANES - N_EXPERTS

    for i in range(depth):
        j = i // n_mixers
        gain = mix_norm[i][None, :]
        w_r = jnp.pad(moe_w_router[i], ((0, 0), (0, pad_e)))
        w_hi = w_r.astype(BF16)
        w_split = jnp.concatenate([w_hi, (w_r - w_hi.astype(F32)).astype(BF16)], axis=1)
        b_r = jnp.pad(moe_b_router[i], (0, pad_e), constant_values=NEG_INF)[None, :]
        route = (ffn_norm[i][None, :], w_split, b_r)
        if i % n_mixers == 0:
            x, hn, aff = _gmlp_layer(
                x.reshape(bsz * seq, d), gain, gm_w_in[j].astype(BF16), gm_b_in[j][None, :],
                gm_v_norm[j][None, :], gm_w_s[j].astype(BF16), gm_b_s[j].T, gm_w_out[j].astype(BF16),
                gm_b_out[j][None, :], route)
        else:
            def per_head(g):
                return jnp.tile(g[:, None, :], (1, HEADS, 1)).reshape(N_GROUPS, ATTN_WIDTH)
            qk_gain = jnp.stack([per_head(at_q_norm[j]) * (HEAD_DIM ** -0.5), per_head(at_k_norm[j]),
                                 jnp.ones((N_GROUPS, ATTN_WIDTH), F32)]).reshape(1, -1)
            qkvs = _qkv_proj(x, gain, _group_major(at_w_qkv[j]).astype(BF16), _group_major(qk_gain), head_ones)
            outs = [_attn_group(g, qkvs[g], biases[g]) for g in range(N_GROUPS)]
            x, hn, aff = _attn_merge(outs, x, at_w_o[j].astype(BF16), route)
        x = _ec_moe(i, x.reshape(bsz, seq, d), hn.reshape(bsz, seq, d), aff.reshape(bsz, seq, LANES),
                    moe_w1, moe_w3, moe_w2, tri)
    return x
```

```python
import functools

import jax
import jax.numpy as jnp
from jax import lax
from jax.experimental import pallas as pl
from jax.experimental.pallas import tpu as pltpu

F32 = jnp.float32
BF16 = jnp.bfloat16
I32 = jnp.int32

EPS = 1e-6
NEG_INF = -1e30

LANES = 128
MXU_DIM = 256
BF16_ROWS = 16
VMEM_LIMIT_BYTES = 56 * 1024 * 1024

GMLP_CHUNK = 128
GMLP_GROUPS = 8
DIL_PATTERNS = ((128, 1), (512, 4), (2048, 16))
N_GROUPS = len(DIL_PATTERNS)
HEADS = 8
HEAD_DIM = 64
ATTN_WIDTH = HEADS * HEAD_DIM
GROUP_WIDTH = 3 * ATTN_WIDTH
QUERY_BLOCK = 128
HALF_WINDOW = 64
N_EXPERTS = 16
CAPACITY_FACTOR = 2

GMLP_ROWS = 512
QKV_ROWS = 512
MERGE_ROWS = 512
ATTN_BLOCKS = 16
TOKEN_BLOCK = MXU_DIM
SELECT_SEQS = 2
COMBINE_BLOCKS = 4
PACK = 4
GATHER_PACK = 8
FIRST_WINDOW = MXU_DIM // PACK
GATHER_WINDOW = 128
SCATTER_WINDOW = MXU_DIM
LO_STRIDE = 32
EXPERT_SEQS = 2
EXPERT_CHUNK = 2 * MXU_DIM


def _params(*semantics):
    return pltpu.CompilerParams(dimension_semantics=semantics, vmem_limit_bytes=VMEM_LIMIT_BYTES)


def _const_spec(shape):
    zeros = (0,) * len(shape)
    return pl.BlockSpec(shape, lambda *_: zeros, pipeline_mode=pl.Buffered(1))


def _rms(x, gain):
    return x * lax.rsqrt(jnp.mean(x * x, axis=-1, keepdims=True) + EPS) * gain


def _dot(a, b):
    return jnp.dot(a, b, preferred_element_type=F32)


def _floor_to(x, multiple):
    shift = multiple.bit_length() - 1
    return lax.shift_left(lax.shift_right_logical(x, shift), shift)


def _ceil_div(x, divisor):
    return lax.shift_right_logical(x + (divisor - 1), divisor.bit_length() - 1)


def _route(x, g_ref, wr_ref, br_ref, hn_ref, aff_ref):
    hn = _rms(x, g_ref[...])
    hi = hn.astype(BF16)
    lo = (hn - hi.astype(F32)).astype(BF16)
    hn_ref[...] = hi
    both = _dot(hi, wr_ref[...])
    logits = both[:, :LANES] + both[:, LANES:] + _dot(lo, wr_ref[:, :LANES]) + br_ref[...]
    m = jnp.max(logits, axis=-1, keepdims=True)
    e = jnp.exp(logits - m)
    aff_ref[...] = e / jnp.sum(e, axis=-1, keepdims=True)


def _gmlp_body(x_ref, g_ref, win_ref, bin_ref, vg_ref, ws_ref, bst_ref, wout_ref, bout_ref,
               rg_ref, wr_ref, br_ref, o_ref, hn_ref, aff_ref, v_s, uv_s):
    x = x_ref[...]
    rows, width = v_s.shape
    gw = width // GMLP_GROUPS
    h = _rms(x, g_ref[...]).astype(BF16)
    zv = jax.nn.gelu(_dot(h, win_ref[:, width:]) + bin_ref[:, width:])
    r = lax.rsqrt(jnp.mean(zv * zv, axis=-1, keepdims=True) + EPS)
    v_s[...] = (zv * r * vg_ref[...]).astype(BF16)
    pair = 2 * gw
    for p in range(GMLP_GROUPS // 2):
        zu = jax.nn.gelu(_dot(h, win_ref[:, p * pair:(p + 1) * pair]) + bin_ref[:, p * pair:(p + 1) * pair])
        for gl in range(2):
            g = 2 * p + gl
            cols = slice(g * gw, (g + 1) * gw)
            bias = bst_ref[:, g:g + 1]
            for c in range(rows // GMLP_CHUNK):
                rs = slice(c * GMLP_CHUNK, (c + 1) * GMLP_CHUNK)
                sv = _dot(ws_ref[g], v_s[rs, cols]) + bias
                uv_s[rs, cols] = (zu[rs, gl * gw:(gl + 1) * gw] * sv).astype(BF16)
    out = x + _dot(uv_s[...], wout_ref[...]) + bout_ref[...]
    o_ref[...] = out
    _route(out, rg_ref, wr_ref, br_ref, hn_ref, aff_ref)


def _gmlp_layer(x2, gain, w_in, b_in, v_gain, w_s, b_st, w_out, b_out, route):
    n, d = x2.shape
    width = w_out.shape[0]
    rows = GMLP_ROWS
    return pl.pallas_call(
        _gmlp_body,
        out_shape=[jax.ShapeDtypeStruct((n, d), F32), jax.ShapeDtypeStruct((n, d), BF16),
                   jax.ShapeDtypeStruct((n, LANES), F32)],
        grid=(n // rows,),
        in_specs=[
            pl.BlockSpec((rows, d), lambda i: (i, 0)),
            _const_spec(gain.shape), _const_spec(w_in.shape), _const_spec(b_in.shape),
            _const_spec(v_gain.shape), _const_spec(w_s.shape), _const_spec(b_st.shape),
            _const_spec(w_out.shape), _const_spec(b_out.shape),
        ] + [_const_spec(a.shape) for a in route],
        out_specs=[pl.BlockSpec((rows, d), lambda i: (i, 0)), pl.BlockSpec((rows, d), lambda i: (i, 0)),
                   pl.BlockSpec((rows, LANES), lambda i: (i, 0))],
        scratch_shapes=[pltpu.VMEM((rows, width), BF16), pltpu.VMEM((rows, width), BF16)],
        compiler_params=_params("parallel"),
        name="gmlp_layer",
    )(x2, gain, w_in, b_in, v_gain, w_s, b_st, w_out, b_out, *route)


def _qkv_body(x_ref, g_ref, w_ref, gain_ref, bd_ref, *rest):
    out_refs, t_s = rest[:N_GROUPS], rest[N_GROUPS]
    rows = t_s.shape[1]
    h = _rms(x_ref[...], g_ref[...]).astype(BF16)
    bd = bd_ref[...]

    def put(col0, t):
        for c in range(t.shape[1] // LANES):
            t_s[col0 // LANES + c] = t[:, c * LANES:(c + 1) * LANES]

    for g in range(N_GROUPS):
        tg = _dot(h, w_ref[:, g * GROUP_WIDTH:(g + 1) * GROUP_WIDTH])
        for c in range(2 * ATTN_WIDTH // MXU_DIM):
            col0 = g * GROUP_WIDTH + c * MXU_DIM
            t = tg[:, c * MXU_DIM:(c + 1) * MXU_DIM]
            t2 = t * t
            hi = t2.astype(BF16)
            lo = (t2 - hi.astype(F32)).astype(BF16)
            ss = _dot(hi, bd) + _dot(lo, bd)
            put(col0, t * lax.rsqrt(ss * (1.0 / HEAD_DIM) + EPS) * gain_ref[:, col0:col0 + MXU_DIM])
        put(g * GROUP_WIDTH + 2 * ATTN_WIDTH, tg[:, 2 * ATTN_WIDTH:])
    for g, (_, dil) in enumerate(DIL_PATTERNS):
        for r in range(dil):
            rs = slice(None) if dil == 1 else pl.ds(r, rows // dil, stride=dil)
            for c in range(GROUP_WIDTH // LANES):
                out_refs[g][0, r, :, c * LANES:(c + 1) * LANES] = (
                    t_s[g * (GROUP_WIDTH // LANES) + c, rs, :].astype(BF16))


def _qkv_proj(x, gain, w_qkv, qk_gain, head_ones):
    bsz, seq, d = x.shape
    width = w_qkv.shape[1]
    rows = QKV_ROWS
    out_shape, out_specs = [], []
    for _, dil in DIL_PATTERNS:
        out_shape.append(jax.ShapeDtypeStruct((bsz, dil, seq // dil, GROUP_WIDTH), BF16))
        out_specs.append(pl.BlockSpec((1, dil, rows // dil, GROUP_WIDTH), lambda b, i: (b, 0, i, 0)))
    return pl.pallas_call(
        _qkv_body,
        out_shape=out_shape,
        grid=(bsz, seq // rows),
        in_specs=[
            pl.BlockSpec((None, rows, d), lambda b, i: (b, i, 0)),
            _const_spec(gain.shape), _const_spec(w_qkv.shape), _const_spec(qk_gain.shape),
            _const_spec(head_ones.shape),
        ],
        out_specs=out_specs,
        scratch_shapes=[pltpu.VMEM((width // LANES, rows, LANES), F32)],
        compiler_params=_params("parallel", "parallel"),
        name="qkv_proj",
    )(x, gain, w_qkv, qk_gain, head_ones)


def _attn_body(q_ref, kp_ref, kc_ref, kn_ref, vp_ref, vc_ref, vn_ref, bias_ref, o_ref, l_ref):
    hw = HALF_WINDOW
    qb = QUERY_BLOCK
    n_sub = q_ref.shape[2] // qb
    last_block = pl.num_programs(2) * n_sub - 1
    heads_per_tile = MXU_DIM // HEAD_DIM
    head_of_lane = lax.broadcasted_iota(I32, (qb, MXU_DIM), 1) // HEAD_DIM
    lane = lax.broadcasted_iota(I32, (qb, LANES), 1)
    for rr in range(q_ref.shape[1]):
        k_all = jnp.concatenate([kp_ref[0, rr, qb - hw:, :], kc_ref[0, rr], kn_ref[0, rr, :hw, :]], axis=0)
        v_all = jnp.concatenate([vp_ref[0, rr, qb - hw:, :], vc_ref[0, rr], vn_ref[0, rr, :hw, :]], axis=0)
        for sub in range(n_sub):
            block = pl.program_id(2) * n_sub + sub
            variant = jnp.where(block == 0, 0, jnp.where(block == last_block, 2, 1))
            q = q_ref[0, rr, sub * qb:(sub + 1) * qb, :]
            k = k_all[sub * qb:sub * qb + qb + 2 * hw]
            v = v_all[sub * qb:sub * qb + qb + 2 * hw]
            lse_tile = jnp.zeros((qb, LANES), F32)
            for half in range(ATTN_WIDTH // MXU_DIM):
                cs = slice(half * MXU_DIM, (half + 1) * MXU_DIM)
                qh, kh, vh = q[:, cs], k[:, cs], v[:, cs]
                qs = jnp.concatenate(
                    [jnp.where(head_of_lane == hh, qh, jnp.zeros_like(qh)) for hh in range(heads_per_tile)], axis=0)
                s = lax.dot_general(qs, kh, (((1,), (1,)), ((), ())), preferred_element_type=F32)
                s = s + bias_ref[variant, half]
                m = jnp.max(s, axis=-1, keepdims=True)
                p = jnp.exp(s - m)
                l = jnp.sum(p, axis=-1, keepdims=True)
                pv = _dot(p.astype(BF16), vh) / l
                lse = m + jnp.log(l)
                o_half = jnp.zeros((qb, MXU_DIM), F32)
                for hh in range(heads_per_tile):
                    rs = slice(hh * qb, (hh + 1) * qb)
                    o_half = jnp.where(head_of_lane == hh, pv[rs], o_half)
                    lse_tile = jnp.where(lane == half * heads_per_tile + hh, lse[rs], lse_tile)
                o_ref[0, rr, sub * qb:(sub + 1) * qb, cs] = o_half.astype(BF16)
            l_ref[0, rr, sub * qb:(sub + 1) * qb, :] = lse_tile


def _attn_group(g, qkv_g, bias):
    bsz, dil, sub_len, _ = qkv_g.shape
    nb = sub_len // QUERY_BLOCK
    per_step = min(ATTN_BLOCKS, nb)
    rows = per_step * QUERY_BLOCK
    res = min(ATTN_BLOCKS // per_step, dil)

    def main(which):
        return pl.BlockSpec((1, res, rows, ATTN_WIDTH), lambda b, r, i: (b, r, i, which))

    def halo(which, offset):
        def index(b, r, i):
            return (b, r, jnp.clip(i * per_step + offset, 0, nb - 1), which)
        return pl.BlockSpec((1, res, QUERY_BLOCK, ATTN_WIDTH), index)

    return pl.pallas_call(
        _attn_body,
        out_shape=[jax.ShapeDtypeStruct((bsz, dil, sub_len, ATTN_WIDTH), BF16),
                   jax.ShapeDtypeStruct((bsz, dil, sub_len, LANES), F32)],
        grid=(bsz, dil // res, nb // per_step),
        in_specs=[main(0), halo(1, -1), main(1), halo(1, per_step), halo(2, -1), main(2), halo(2, per_step),
                  _const_spec(bias.shape)],
        out_specs=[main(0), pl.BlockSpec((1, res, rows, LANES), lambda b, r, i: (b, r, i, 0))],
        compiler_params=_params("parallel", "parallel", "arbitrary"),
        name=f"dilated_attn_g{g}",
    )(*([qkv_g] * 7), bias)


def _attn_bias(dil):
    slopes = jnp.exp2(-8.0 * jnp.arange(1, HEADS + 1, dtype=F32) / HEADS)
    kb = QUERY_BLOCK + 2 * HALF_WINDOW
    rel = jnp.arange(kb)[None, :] - HALF_WINDOW - jnp.arange(QUERY_BLOCK)[:, None]
    band = jnp.abs(rel) <= HALF_WINDOW
    alibi = -slopes[:, None, None] * (dil * jnp.abs(rel)).astype(F32)[None]
    j = jnp.arange(kb)
    edge = jnp.stack([j >= HALF_WINDOW, j >= 0, j < kb - HALF_WINDOW])
    ok = band[None, None] & edge[:, None, None, :]
    tab = jnp.where(ok, alibi[None], NEG_INF)
    per_tile = MXU_DIM // HEAD_DIM
    return tab.reshape(3, HEADS // per_tile, per_tile * QUERY_BLOCK, kb)


def _merge_body(*refs):
    o_refs, l_refs = refs[:N_GROUPS], refs[N_GROUPS:2 * N_GROUPS]
    x_ref, wo_ref, spread_ref, rg_ref, wr_ref, br_ref, out_ref, hn_ref, aff_ref, o_s, l_s = refs[2 * N_GROUPS:]
    rows = x_ref.shape[0]
    tiles = ATTN_WIDTH // LANES
    for g, (_, dil) in enumerate(DIL_PATTERNS):
        for r in range(dil):
            rs = slice(None) if dil == 1 else pl.ds(r, rows // dil, stride=dil)
            l_s[g, rs, :] = l_refs[g][0, r]
            for c in range(tiles):
                o_s[g * tiles + c, rs, :] = o_refs[g][0, r, :, c * LANES:(c + 1) * LANES].astype(F32)
    lses = [l_s[g] for g in range(N_GROUPS)]
    mx = functools.reduce(jnp.maximum, lses)
    es = [jnp.exp(l - mx) for l in lses]
    den = functools.reduce(lambda a, b: a + b, es)
    merged = [jnp.zeros((rows, LANES), F32) for _ in range(tiles)]
    for g in range(N_GROUPS):
        w = es[g] / den
        hi = w.astype(BF16)
        lo = (w - hi.astype(F32)).astype(BF16)
        wide = _dot(hi, spread_ref[...]) + _dot(lo, spread_ref[...])
        for c in range(tiles):
            merged[c] = merged[c] + wide[:, c * LANES:(c + 1) * LANES] * o_s[g * tiles + c]
    merged = [m.astype(BF16) for m in merged]
    out = x_ref[...] + _dot(jnp.concatenate(merged, axis=1), wo_ref[...])
    out_ref[...] = out
    _route(out, rg_ref, wr_ref, br_ref, hn_ref, aff_ref)


def _attn_merge(outs, x, w_o, route):
    bsz, seq, d = x.shape
    rows = MERGE_ROWS
    args, in_specs = [], []
    for which, width in enumerate((ATTN_WIDTH, LANES)):
        for g, (_, dil) in enumerate(DIL_PATTERNS):
            args.append(outs[g][which])
            in_specs.append(pl.BlockSpec((1, dil, rows // dil, width), lambda b, i: (b, 0, i, 0)))
    lane_head = jnp.arange(ATTN_WIDTH) // HEAD_DIM
    spread = (jnp.arange(LANES)[:, None] == lane_head[None, :]).astype(BF16)
    args += [x, w_o, spread, *route]
    in_specs += [pl.BlockSpec((None, rows, d), lambda b, i: (b, i, 0)), _const_spec(w_o.shape),
                 _const_spec(spread.shape)]
    in_specs += [_const_spec(a.shape) for a in route]

    def row_block(width):
        return pl.BlockSpec((None, rows, width), lambda b, i: (b, i, 0))

    return pl.pallas_call(
        _merge_body,
        out_shape=[jax.ShapeDtypeStruct((bsz, seq, d), F32), jax.ShapeDtypeStruct((bsz, seq, d), BF16),
                   jax.ShapeDtypeStruct((bsz, seq, LANES), F32)],
        grid=(bsz, seq // rows),
        in_specs=in_specs,
        out_specs=[row_block(d), row_block(d), row_block(LANES)],
        scratch_shapes=[pltpu.VMEM((N_GROUPS * ATTN_WIDTH // LANES, rows, LANES), F32),
                        pltpu.VMEM((N_GROUPS, rows, LANES), F32)],
        compiler_params=_params("parallel", "parallel"),
        name="attn_merge",
    )(*args)


def _select_one(cap, s, aff_ref, tri_ref, post_ref, gatet_ref, pos_ref, lo_ref):
    seq = aff_ref.shape[1]
    aff_t = aff_ref[s].T[:N_EXPERTS]
    key = pltpu.bitcast(aff_t, I32)
    thr = jnp.zeros((N_EXPERTS, 1), I32)
    for bit in range(30, -1, -1):
        cand = thr | (1 << bit)
        cnt = jnp.sum(jnp.where(key >= cand, 1.0, 0.0), axis=1, keepdims=True)
        thr = jnp.where(cnt >= cap, cand, thr)
    gt = key > thr
    eq = key == thr
    need = cap - jnp.sum(jnp.where(gt, 1.0, 0.0), axis=1, keepdims=True)
    tri = tri_ref[...]
    nblk = seq // TOKEN_BLOCK

    def prefix(mask_f, want_offsets):
        off = jnp.zeros((N_EXPERTS, 1), F32)
        parts, offs = [], []
        for jb in range(nblk):
            blk = mask_f[:, jb * TOKEN_BLOCK:(jb + 1) * TOKEN_BLOCK]
            offs.append(off)
            parts.append(_dot(blk.astype(BF16), tri) + off)
            off = off + jnp.sum(blk, axis=1, keepdims=True)
        offs.append(off)
        return jnp.concatenate(parts, axis=1), (offs if want_offsets else None)

    eq_rank, _ = prefix(jnp.where(eq, 1.0, 0.0), False)
    sel = gt | (eq & (eq_rank < need))
    rank, offs = prefix(jnp.where(sel, 1.0, 0.0), True)
    post = jnp.where(sel, rank, -1.0)
    post_ref[s] = post.astype(I32)
    gatet_ref[s] = jnp.where(sel, aff_t, 0.0)
    padded = jnp.concatenate([post, jnp.full((LANES - N_EXPERTS, seq), -1.0, F32)], axis=0)
    pos_ref[s] = padded.T.astype(I32)
    lane = lax.broadcasted_iota(I32, (N_EXPERTS, LANES), 1)
    lo = jnp.zeros((N_EXPERTS, LANES), F32)
    for jb, off in enumerate(offs):
        lo = jnp.where(lane == jb, off, lo)
    lo_ref[s] = lo.astype(I32)


def _select_body(cap, aff_ref, *refs):
    for s in range(aff_ref.shape[0]):
        _select_one(cap, s, aff_ref, *refs)


def _select(aff, tri, cap):
    bsz, seq, _ = aff.shape
    n = SELECT_SEQS if bsz % SELECT_SEQS == 0 else 1
    return pl.pallas_call(
        functools.partial(_select_body, cap),
        out_shape=[jax.ShapeDtypeStruct((bsz, N_EXPERTS, seq), I32),
                   jax.ShapeDtypeStruct((bsz, N_EXPERTS, seq), F32),
                   jax.ShapeDtypeStruct((bsz, seq, LANES), I32),
                   jax.ShapeDtypeStruct((bsz, N_EXPERTS, LANES), I32)],
        grid=(bsz // n,),
        in_specs=[pl.BlockSpec((n, seq, LANES), lambda b: (b, 0, 0)), _const_spec(tri.shape)],
        out_specs=[pl.BlockSpec((n, N_EXPERTS, seq), lambda b: (b, 0, 0)),
                   pl.BlockSpec((n, N_EXPERTS, seq), lambda b: (b, 0, 0)),
                   pl.BlockSpec((n, seq, LANES), lambda b: (b, 0, 0)),
                   pl.BlockSpec((n, N_EXPERTS, LANES), lambda b: (b, 0, 0))],
        compiler_params=_params("parallel"),
        name="moe_select",
    )(aff, tri)


def _gather_body(lo_ref, hn_ref, post_ref, gatet_ref, xin_ref, gate_ref):
    b = pl.program_id(0)
    eg = pl.program_id(1)
    seq = hn_ref.shape[1]
    cap = xin_ref.shape[2]
    nblk = seq // TOKEN_BLOCK
    xin_ref[...] = jnp.zeros_like(xin_ref)
    gate_ref[...] = jnp.zeros_like(gate_ref)

    def table(i):
        return (b * N_EXPERTS + eg * GATHER_PACK + i) * LO_STRIDE

    def window(i, ts, s0, rows, iota):
        s0c = pl.multiple_of(jnp.minimum(s0, cap - rows), BF16_ROWS)
        slot = iota + s0c
        hit = (post_ref[0, i, :, ts] == slot) & (slot >= s0)

        def add(contrib):
            xin_ref[0, i, pl.ds(s0c, rows), :] += contrib.astype(BF16)
            gsum = jnp.sum(jnp.where(hit, gatet_ref[0, i, :, ts], 0.0), axis=1, keepdims=True)
            gate_ref[0, i, pl.ds(s0c, rows), :] += jnp.broadcast_to(gsum, (rows, LANES))

        return jnp.where(hit, 1.0, 0.0).astype(BF16), add

    first_iota = lax.broadcasted_iota(I32, (FIRST_WINDOW, TOKEN_BLOCK), 0)
    for j in range(nblk):
        ts = slice(j * TOKEN_BLOCK, (j + 1) * TOKEN_BLOCK)
        rows, adds = zip(*[window(i, ts, _floor_to(lo_ref[table(i) + j], BF16_ROWS), FIRST_WINDOW, first_iota)
                           for i in range(GATHER_PACK)])
        contrib = _dot(jnp.concatenate(rows, axis=0), hn_ref[0, ts, :])
        for i in range(GATHER_PACK):
            adds[i](contrib[i * FIRST_WINDOW:(i + 1) * FIRST_WINDOW])

    def left_over(i, j):
        done = _floor_to(lo_ref[table(i) + j], BF16_ROWS) + FIRST_WINDOW
        return jnp.maximum(lo_ref[table(i) + j + 1] - done, 0), done

    most = functools.reduce(jnp.maximum, [left_over(i, j)[0] for i in range(GATHER_PACK) for j in range(nblk)])

    @pl.when(most > 0)
    def _():
        more_iota = lax.broadcasted_iota(I32, (GATHER_WINDOW, TOKEN_BLOCK), 0)
        for j in range(nblk):
            ts = slice(j * TOKEN_BLOCK, (j + 1) * TOKEN_BLOCK)

            def expert(i, carry, j=j, ts=ts):
                left, done = left_over(i, j)

                def further(w, carry):
                    onehot, add = window(i, ts, done + w * GATHER_WINDOW, GATHER_WINDOW, more_iota)
                    add(_dot(onehot, hn_ref[0, ts, :]))
                    return carry

                return lax.fori_loop(0, _ceil_div(left, GATHER_WINDOW), further, carry)

            lax.fori_loop(0, GATHER_PACK, expert, 0)


def _gather(lo_flat, hn, post, gatet, cap):
    bsz, seq, d = hn.shape
    post4 = post.reshape(bsz, N_EXPERTS, 1, seq)
    gatet4 = gatet.reshape(bsz, N_EXPERTS, 1, seq)
    grid_spec = pltpu.PrefetchScalarGridSpec(
        num_scalar_prefetch=1,
        grid=(bsz, N_EXPERTS // GATHER_PACK),
        in_specs=[pl.BlockSpec((1, seq, d), lambda b, e, lo: (b, 0, 0)),
                  pl.BlockSpec((1, GATHER_PACK, 1, seq), lambda b, e, lo: (b, e, 0, 0)),
                  pl.BlockSpec((1, GATHER_PACK, 1, seq), lambda b, e, lo: (b, e, 0, 0))],
        out_specs=[pl.BlockSpec((1, GATHER_PACK, cap, d), lambda b, e, lo: (b, e, 0, 0)),
                   pl.BlockSpec((1, GATHER_PACK, cap, LANES), lambda b, e, lo: (b, e, 0, 0))],
    )
    return pl.pallas_call(
        _gather_body,
        out_shape=[jax.ShapeDtypeStruct((bsz, N_EXPERTS, cap, d), BF16),
                   jax.ShapeDtypeStruct((bsz, N_EXPERTS, cap, LANES), F32)],
        grid_spec=grid_spec,
        compiler_params=_params("parallel", "arbitrary"),
        name="moe_gather",
    )(lo_flat, hn, post4, gatet4)


def _expert_body(xin_ref, gate_ref, w1c_ref, w3c_ref, w2c_ref, y_ref, w1_s, w3_s, w2_s):
    r = pl.program_id(0)
    step = pl.program_id(1)
    fill = lax.rem(r, 2)

    def cast_slice():
        d_rows = w1c_ref.shape[2]
        f_rows = w2c_ref.shape[2]
        d0 = pl.multiple_of(step * d_rows, d_rows)
        f0 = pl.multiple_of(step * f_rows, f_rows)
        w1_s[fill, pl.ds(d0, d_rows), :] = w1c_ref[0, 0].astype(BF16)
        w3_s[fill, pl.ds(d0, d_rows), :] = w3c_ref[0, 0].astype(BF16)
        w2_s[fill, pl.ds(f0, f_rows), :] = w2c_ref[0, 0].astype(BF16)

    @pl.when(r == 0)
    def _():
        cast_slice()
        y_ref[...] = jnp.zeros_like(y_ref)

    @pl.when(r > 0)
    def _():
        cast_slice()
        use = 1 - fill
        chunk = EXPERT_CHUNK
        for s in range(xin_ref.shape[0]):
            xin = xin_ref[s, 0]
            y = jnp.zeros(y_ref.shape[2:], F32)
            for c in range(w1_s.shape[2] // chunk):
                fs = slice(c * chunk, (c + 1) * chunk)
                hid = jax.nn.silu(_dot(xin, w1_s[use, :, fs])) * _dot(xin, w3_s[use, :, fs])
                y = y + _dot(hid.astype(BF16), w2_s[use, fs, :])
            y_ref[s, 0] = (y * gate_ref[s, 0, :, 0:1]).astype(BF16)


def _experts(layer, xin, gate, w1, w3, w2):
    bsz, ne, cap, d = xin.shape
    f = w1.shape[3]
    seqs = EXPERT_SEQS if bsz % EXPERT_SEQS == 0 else 1
    steps = bsz // seqs
    d_rows, f_rows = d // steps, f // steps
    assert d_rows * steps == d and f_rows * steps == f

    def act(width):
        return pl.BlockSpec((seqs, 1, cap, width), lambda r, b: (b, jnp.maximum(r - 1, 0), 0, 0))

    out_spec = pl.BlockSpec((seqs, 1, cap, d), lambda r, b: (b, r, 0, 0))

    def weight_slice(rows, cols):
        return pl.BlockSpec((1, 1, rows, cols), lambda r, b: (layer, jnp.minimum(r, ne - 1), b, 0))

    return pl.pallas_call(
        _expert_body,
        out_shape=jax.ShapeDtypeStruct((bsz, ne + 1, cap, d), BF16),
        grid=(ne + 1, steps),
        in_specs=[act(d), act(LANES), weight_slice(d_rows, f), weight_slice(d_rows, f), weight_slice(f_rows, d)],
        out_specs=out_spec,
        scratch_shapes=[pltpu.VMEM((2, d, f), BF16), pltpu.VMEM((2, d, f), BF16), pltpu.VMEM((2, f, d), BF16)],
        compiler_params=_params("arbitrary", "arbitrary"),
        name="moe_experts",
    )(xin, gate, w1, w3, w2)


def _combine_body(lo_ref, x_ref, y_ref, pos_ref, o_ref):
    b = pl.program_id(0)
    cap = y_ref.shape[2]
    lane = lax.broadcasted_iota(I32, (TOKEN_BLOCK, PACK * FIRST_WINDOW), 1)
    lane_expert = lane // FIRST_WINDOW
    lane_slot = lane % FIRST_WINDOW
    pending = []
    for sub in range(COMBINE_BLOCKS):
        j = pl.program_id(1) * COMBINE_BLOCKS + sub
        ts = slice(sub * TOKEN_BLOCK, (sub + 1) * TOKEN_BLOCK)
        pos = pos_ref[0, ts, :]

        def bounds(e, j=j):
            base = (b * N_EXPERTS + e) * LO_STRIDE
            return lo_ref[base + j], lo_ref[base + j + 1]

        acc = x_ref[0, ts, :]
        for q in range(N_EXPERTS // PACK):
            target = jnp.full(lane.shape, -1, I32)
            y_rows = []
            for i in range(PACK):
                e = q * PACK + i
                start = _floor_to(bounds(e)[0], BF16_ROWS)
                s0c = pl.multiple_of(jnp.minimum(start, cap - FIRST_WINDOW), BF16_ROWS)
                pcol = pos[:, e:e + 1]
                rel = jnp.where(pcol >= start, pcol - s0c, -1)
                target = jnp.where(lane_expert == i, rel, target)
                y_rows.append(y_ref[0, e + 1, pl.ds(s0c, FIRST_WINDOW), :])
            onehot = jnp.where(target == lane_slot, 1.0, 0.0).astype(BF16)
            acc = acc + _dot(onehot, jnp.concatenate(y_rows, axis=0))
        o_ref[0, ts, :] = acc

        def left_over(e, bounds=bounds):
            lo, hi = bounds(e)
            done = _floor_to(lo, BF16_ROWS) + FIRST_WINDOW
            return jnp.maximum(hi - done, 0), done

        pending.append((ts, pos, left_over))

    for ts, pos, left_over in pending:
        most = functools.reduce(jnp.maximum, [left_over(e)[0] for e in range(N_EXPERTS)])

        @pl.when(most > 0)
        def _(ts=ts, pos=pos, left_over=left_over):
            more_lane = lax.broadcasted_iota(I32, (TOKEN_BLOCK, SCATTER_WINDOW), 1)
            for e in range(N_EXPERTS):
                left, done = left_over(e)
                pcol = pos[:, e:e + 1]

                def window(w, carry, e=e, done=done, pcol=pcol):
                    s0 = done + w * SCATTER_WINDOW
                    s0c = pl.multiple_of(jnp.minimum(s0, cap - SCATTER_WINDOW), BF16_ROWS)
                    hit = ((pcol - s0c) == more_lane) & (pcol >= s0)
                    onehot = jnp.where(hit, 1.0, 0.0).astype(BF16)
                    o_ref[0, ts, :] += _dot(onehot, y_ref[0, e + 1, pl.ds(s0c, SCATTER_WINDOW), :])
                    return carry

                lax.fori_loop(0, _ceil_div(left, SCATTER_WINDOW), window, 0)


def _combine(lo_flat, x, y, pos):
    bsz, seq, d = x.shape
    _, slots, cap, _ = y.shape
    rows = COMBINE_BLOCKS * TOKEN_BLOCK
    grid_spec = pltpu.PrefetchScalarGridSpec(
        num_scalar_prefetch=1,
        grid=(bsz, seq // rows),
        in_specs=[pl.BlockSpec((1, rows, d), lambda b, j, lo: (b, j, 0)),
                  pl.BlockSpec((1, slots, cap, d), lambda b, j, lo: (b, 0, 0, 0)),
                  pl.BlockSpec((1, rows, LANES), lambda b, j, lo: (b, j, 0))],
        out_specs=pl.BlockSpec((1, rows, d), lambda b, j, lo: (b, j, 0)),
    )
    return pl.pallas_call(
        _combine_body,
        out_shape=jax.ShapeDtypeStruct((bsz, seq, d), F32),
        grid_spec=grid_spec,
        compiler_params=_params("parallel", "arbitrary"),
        name="moe_combine",
    )(lo_flat, x, y, pos)


def _ec_moe(layer, x, hn, aff, w1, w3, w2, tri):
    bsz, seq, d = x.shape
    cap = CAPACITY_FACTOR * seq // N_EXPERTS
    post, gatet, pos, lo = _select(aff, tri, cap)
    lo_flat = lo[:, :, :LO_STRIDE].reshape(-1)
    xin, gate = _gather(lo_flat, hn, post, gatet, cap)
    y = _experts(layer, xin, gate, w1, w3, w2)
    return _combine(lo_flat, x, y, pos)


def _group_major(a):
    lead = a.shape[:-1]
    a = a.reshape(lead + (3, N_GROUPS, ATTN_WIDTH))
    return jnp.swapaxes(a, -3, -2).reshape(lead + (3 * N_GROUPS * ATTN_WIDTH,))


def kernel(x, mix_norm, ffn_norm, gm_w_in, gm_b_in, gm_v_norm, gm_w_s, gm_b_s, gm_w_out, gm_b_out,
           at_w_qkv, at_q_norm, at_k_norm, at_w_o, moe_w_router, moe_b_router, moe_w1, moe_w3, moe_w2):
    bsz, seq, d = x.shape
    depth = mix_norm.shape[0]
    n_mixers = 2

    lane_head = jnp.arange(MXU_DIM) // HEAD_DIM
    head_ones = (lane_head[:, None] == lane_head[None, :]).astype(BF16)
    tri = (jnp.arange(TOKEN_BLOCK)[:, None] < jnp.arange(TOKEN_BLOCK)[None, :]).astype(BF16)
    biases = [_attn_bias(dil) for _, dil in DIL_PATTERNS]
    pad_e = LANES - N_EXPERTS

    for i in range(depth):
        j = i // n_mixers
        gain = mix_norm[i][None, :]
        w_r = jnp.pad(moe_w_router[i], ((0, 0), (0, pad_e)))
        w_hi = w_r.astype(BF16)
        w_split = jnp.concatenate([w_hi, (w_r - w_hi.astype(F32)).astype(BF16)], axis=1)
        b_r = jnp.pad(moe_b_router[i], (0, pad_e), constant_values=NEG_INF)[None, :]
        route = (ffn_norm[i][None, :], w_split, b_r)
        if i % n_mixers == 0:
            x, hn, aff = _gmlp_layer(
                x.reshape(bsz * seq, d), gain, gm_w_in[j].astype(BF16), gm_b_in[j][None, :],
                gm_v_norm[j][None, :], gm_w_s[j].astype(BF16), gm_b_s[j].T, gm_w_out[j].astype(BF16),
                gm_b_out[j][None, :], route)
        else:
            def per_head(g):
                return jnp.tile(g[:, None, :], (1, HEADS, 1)).reshape(N_GROUPS, ATTN_WIDTH)
            qk_gain = jnp.stack([per_head(at_q_norm[j]) * (HEAD_DIM ** -0.5), per_head(at_k_norm[j]),
                                 jnp.ones((N_GROUPS, ATTN_WIDTH), F32)]).reshape(1, -1)
            qkvs = _qkv_proj(x, gain, _group_major(at_w_qkv[j]).astype(BF16), _group_major(qk_gain), head_ones)
            outs = [_attn_group(g, qkvs[g], biases[g]) for g in range(N_GROUPS)]
            x, hn, aff = _attn_merge(outs, x, at_w_o[j].astype(BF16), route)
        x = _ec_moe(i, x.reshape(bsz, seq, d), hn.reshape(bsz, seq, d), aff.reshape(bsz, seq, LANES),
                    moe_w1, moe_w3, moe_w2, tri)
    return x
```

```python
import functools

import jax
import jax.numpy as jnp
from jax import lax
from jax.experimental import pallas as pl
from jax.experimental.pallas import tpu as pltpu

F32 = jnp.float32
BF16 = jnp.bfloat16
I32 = jnp.int32

EPS = 1e-6
NEG_INF = -1e30

LANES = 128
MXU_DIM = 256
BF16_ROWS = 16
VMEM_LIMIT_BYTES = 56 * 1024 * 1024

GMLP_CHUNK = 128
GMLP_GROUPS = 8
DIL_PATTERNS = ((128, 1), (512, 4), (2048, 16))
N_GROUPS = len(DIL_PATTERNS)
HEADS = 8
HEAD_DIM = 64
ATTN_WIDTH = HEADS * HEAD_DIM
GROUP_WIDTH = 3 * ATTN_WIDTH
QUERY_BLOCK = 128
HALF_WINDOW = 64
N_EXPERTS = 16
CAPACITY_FACTOR = 2

GMLP_ROWS = 512
QKV_ROWS = 512
MERGE_ROWS = 512
ATTN_BLOCKS = 16
TOKEN_BLOCK = MXU_DIM
SELECT_SEQS = 2
COMBINE_BLOCKS = 4
PACK = 4
GATHER_PACK = 8
FIRST_WINDOW = MXU_DIM // PACK
GATHER_WINDOW = 128
SCATTER_WINDOW = MXU_DIM
LO_STRIDE = 32
EXPERT_SEQS = 2
EXPERT_CHUNK = 2 * MXU_DIM


def _params(*semantics):
    return pltpu.CompilerParams(dimension_semantics=semantics, vmem_limit_bytes=VMEM_LIMIT_BYTES)


def _const_spec(shape):
    zeros = (0,) * len(shape)
    return pl.BlockSpec(shape, lambda *_: zeros, pipeline_mode=pl.Buffered(1))


def _rms(x, gain):
    return x * lax.rsqrt(jnp.mean(x * x, axis=-1, keepdims=True) + EPS) * gain


def _dot(a, b):
    return jnp.dot(a, b, preferred_element_type=F32)


def _floor_to(x, multiple):
    shift = multiple.bit_length() - 1
    return lax.shift_left(lax.shift_right_logical(x, shift), shift)


def _ceil_div(x, divisor):
    return lax.shift_right_logical(x + (divisor - 1), divisor.bit_length() - 1)


def _route(x, g_ref, wr_ref, br_ref, hn_ref, aff_ref):
    hn = _rms(x, g_ref[...])
    hi = hn.astype(BF16)
    lo = (hn - hi.astype(F32)).astype(BF16)
    hn_ref[...] = hi
    both = _dot(hi, wr_ref[...])
    logits = both[:, :LANES] + both[:, LANES:] + _dot(lo, wr_ref[:, :LANES]) + br_ref[...]
    m = jnp.max(logits, axis=-1, keepdims=True)
    e = jnp.exp(logits - m)
    aff_ref[...] = e / jnp.sum(e, axis=-1, keepdims=True)


def _gmlp_body(x_ref, g_ref, win_ref, bin_ref, vg_ref, ws_ref, bst_ref, wout_ref, bout_ref,
               rg_ref, wr_ref, br_ref, o_ref, hn_ref, aff_ref, v_s, uv_s):
    x = x_ref[...]
    rows, width = v_s.shape
    gw = width // GMLP_GROUPS
    h = _rms(x, g_ref[...]).astype(BF16)
    zv = jax.nn.gelu(_dot(h, win_ref[:, width:]) + bin_ref[:, width:])
    r = lax.rsqrt(jnp.mean(zv * zv, axis=-1, keepdims=True) + EPS)
    v_s[...] = (zv * r * vg_ref[...]).astype(BF16)
    pair = 2 * gw
    for p in range(GMLP_GROUPS // 2):
        zu = jax.nn.gelu(_dot(h, win_ref[:, p * pair:(p + 1) * pair]) + bin_ref[:, p * pair:(p + 1) * pair])
        for gl in range(2):
            g = 2 * p + gl
            cols = slice(g * gw, (g + 1) * gw)
            bias = bst_ref[:, g:g + 1]
            for c in range(rows // GMLP_CHUNK):
                rs = slice(c * GMLP_CHUNK, (c + 1) * GMLP_CHUNK)
                sv = _dot(ws_ref[g], v_s[rs, cols]) + bias
                uv_s[rs, cols] = (zu[rs, gl * gw:(gl + 1) * gw] * sv).astype(BF16)
    out = x + _dot(uv_s[...], wout_ref[...]) + bout_ref[...]
    o_ref[...] = out
    _route(out, rg_ref, wr_ref, br_ref, hn_ref, aff_ref)


def _gmlp_layer(x2, gain, w_in, b_in, v_gain, w_s, b_st, w_out, b_out, route):
    n, d = x2.shape
    width = w_out.shape[0]
    rows = GMLP_ROWS
    return pl.pallas_call(
        _gmlp_body,
        out_shape=[jax.ShapeDtypeStruct((n, d), F32), jax.ShapeDtypeStruct((n, d), BF16),
                   jax.ShapeDtypeStruct((n, LANES), F32)],
        grid=(n // rows,),
        in_specs=[
            pl.BlockSpec((rows, d), lambda i: (i, 0)),
            _const_spec(gain.shape), _const_spec(w_in.shape), _const_spec(b_in.shape),
            _const_spec(v_gain.shape), _const_spec(w_s.shape), _const_spec(b_st.shape),
            _const_spec(w_out.shape), _const_spec(b_out.shape),
        ] + [_const_spec(a.shape) for a in route],
        out_specs=[pl.BlockSpec((rows, d), lambda i: (i, 0)), pl.BlockSpec((rows, d), lambda i: (i, 0)),
                   pl.BlockSpec((rows, LANES), lambda i: (i, 0))],
        scratch_shapes=[pltpu.VMEM((rows, width), BF16), pltpu.VMEM((rows, width), BF16)],
        compiler_params=_params("parallel"),
        name="gmlp_layer",
    )(x2, gain, w_in, b_in, v_gain, w_s, b_st, w_out, b_out, *route)


def _qkv_body(x_ref, g_ref, w_ref, gain_ref, bd_ref, *rest):
    out_refs, t_s = rest[:N_GROUPS], rest[N_GROUPS]
    rows = t_s.shape[1]
    h = _rms(x_ref[...], g_ref[...]).astype(BF16)
    bd = bd_ref[...]

    def put(col0, t):
        for c in range(t.shape[1] // LANES):
            t_s[col0 // LANES + c] = t[:, c * LANES:(c + 1) * LANES]

    for g in range(N_GROUPS):
        tg = _dot(h, w_ref[:, g * GROUP_WIDTH:(g + 1) * GROUP_WIDTH])
        for c in range(2 * ATTN_WIDTH // MXU_DIM):
            col0 = g * GROUP_WIDTH + c * MXU_DIM
            t = tg[:, c * MXU_DIM:(c + 1) * MXU_DIM]
            ss = _dot((t * t).astype(BF16), bd)
            put(col0, t * lax.rsqrt(ss * (1.0 / HEAD_DIM) + EPS) * gain_ref[:, col0:col0 + MXU_DIM])
        put(g * GROUP_WIDTH + 2 * ATTN_WIDTH, tg[:, 2 * ATTN_WIDTH:])
    for g, (_, dil) in enumerate(DIL_PATTERNS):
        for r in range(dil):
            rs = slice(None) if dil == 1 else pl.ds(r, rows // dil, stride=dil)
            for c in range(GROUP_WIDTH // LANES):
                out_refs[g][0, r, :, c * LANES:(c + 1) * LANES] = (
                    t_s[g * (GROUP_WIDTH // LANES) + c, rs, :].astype(BF16))


def _qkv_proj(x, gain, w_qkv, qk_gain, head_ones):
    bsz, seq, d = x.shape
    width = w_qkv.shape[1]
    rows = QKV_ROWS
    out_shape, out_specs = [], []
    for _, dil in DIL_PATTERNS:
        out_shape.append(jax.ShapeDtypeStruct((bsz, dil, seq // dil, GROUP_WIDTH), BF16))
        out_specs.append(pl.BlockSpec((1, dil, rows // dil, GROUP_WIDTH), lambda b, i: (b, 0, i, 0)))
    return pl.pallas_call(
        _qkv_body,
        out_shape=out_shape,
        grid=(bsz, seq // rows),
        in_specs=[
            pl.BlockSpec((None, rows, d), lambda b, i: (b, i, 0)),
            _const_spec(gain.shape), _const_spec(w_qkv.shape), _const_spec(qk_gain.shape),
            _const_spec(head_ones.shape),
        ],
        out_specs=out_specs,
        scratch_shapes=[pltpu.VMEM((width // LANES, rows, LANES), F32)],
        compiler_params=_params("parallel", "parallel"),
        name="qkv_proj",
    )(x, gain, w_qkv, qk_gain, head_ones)


def _attn_body(q_ref, kp_ref, kc_ref, kn_ref, vp_ref, vc_ref, vn_ref, bias_ref, o_ref, l_ref):
    hw = HALF_WINDOW
    qb = QUERY_BLOCK
    n_sub = q_ref.shape[2] // qb
    last_block = pl.num_programs(2) * n_sub - 1
    heads_per_tile = MXU_DIM // HEAD_DIM
    head_of_lane = lax.broadcasted_iota(I32, (qb, MXU_DIM), 1) // HEAD_DIM
    lane = lax.broadcasted_iota(I32, (qb, LANES), 1)
    for rr in range(q_ref.shape[1]):
        k_all = jnp.concatenate([kp_ref[0, rr, qb - hw:, :], kc_ref[0, rr], kn_ref[0, rr, :hw, :]], axis=0)
        v_all = jnp.concatenate([vp_ref[0, rr, qb - hw:, :], vc_ref[0, rr], vn_ref[0, rr, :hw, :]], axis=0)
        for sub in range(n_sub):
            block = pl.program_id(2) * n_sub + sub
            variant = jnp.where(block == 0, 0, jnp.where(block == last_block, 2, 1))
            q = q_ref[0, rr, sub * qb:(sub + 1) * qb, :]
            k = k_all[sub * qb:sub * qb + qb + 2 * hw]
            v = v_all[sub * qb:sub * qb + qb + 2 * hw]
            lse_tile = jnp.zeros((qb, LANES), F32)
            for half in range(ATTN_WIDTH // MXU_DIM):
                cs = slice(half * MXU_DIM, (half + 1) * MXU_DIM)
                qh, kh, vh = q[:, cs], k[:, cs], v[:, cs]
                qs = jnp.concatenate(
                    [jnp.where(head_of_lane == hh, qh, jnp.zeros_like(qh)) for hh in range(heads_per_tile)], axis=0)
                s = lax.dot_general(qs, kh, (((1,), (1,)), ((), ())), preferred_element_type=F32)
                s = s + bias_ref[variant, half]
                m = jnp.max(s, axis=-1, keepdims=True)
                p = jnp.exp(s - m)
                l = jnp.sum(p, axis=-1, keepdims=True)
                pv = _dot(p.astype(BF16), vh) / l
                lse = m + jnp.log(l)
                o_half = jnp.zeros((qb, MXU_DIM), F32)
                for hh in range(heads_per_tile):
                    rs = slice(hh * qb, (hh + 1) * qb)
                    o_half = jnp.where(head_of_lane == hh, pv[rs], o_half)
                    lse_tile = jnp.where(lane == half * heads_per_tile + hh, lse[rs], lse_tile)
                o_ref[0, rr, sub * qb:(sub + 1) * qb, cs] = o_half.astype(BF16)
            l_ref[0, rr, sub * qb:(sub + 1) * qb, :] = lse_tile


def _attn_group(g, qkv_g, bias):
    bsz, dil, sub_len, _ = qkv_g.shape
    nb = sub_len // QUERY_BLOCK
    per_step = min(ATTN_BLOCKS, nb)
    rows = per_step * QUERY_BLOCK
    res = min(ATTN_BLOCKS // per_step, dil)

    def main(which):
        return pl.BlockSpec((1, res, rows, ATTN_WIDTH), lambda b, r, i: (b, r, i, which))

    def halo(which, offset):
        def index(b, r, i):
            return (b, r, jnp.clip(i * per_step + offset, 0, nb - 1), which)
        return pl.BlockSpec((1, res, QUERY_BLOCK, ATTN_WIDTH), index)

    return pl.pallas_call(
        _attn_body,
        out_shape=[jax.ShapeDtypeStruct((bsz, dil, sub_len, ATTN_WIDTH), BF16),
                   jax.ShapeDtypeStruct((bsz, dil, sub_len, LANES), F32)],
        grid=(bsz, dil // res, nb // per_step),
        in_specs=[main(0), halo(1, -1), main(1), halo(1, per_step), halo(2, -1), main(2), halo(2, per_step),
                  _const_spec(bias.shape)],
        out_specs=[main(0), pl.BlockSpec((1, res, rows, LANES), lambda b, r, i: (b, r, i, 0))],
        compiler_params=_params("parallel", "parallel", "arbitrary"),
        name=f"dilated_attn_g{g}",
    )(*([qkv_g] * 7), bias)


def _attn_bias(dil):
    slopes = jnp.exp2(-8.0 * jnp.arange(1, HEADS + 1, dtype=F32) / HEADS)
    kb = QUERY_BLOCK + 2 * HALF_WINDOW
    rel = jnp.arange(kb)[None, :] - HALF_WINDOW - jnp.arange(QUERY_BLOCK)[:, None]
    band = jnp.abs(rel) <= HALF_WINDOW
    alibi = -slopes[:, None, None] * (dil * jnp.abs(rel)).astype(F32)[None]
    j = jnp.arange(kb)
    edge = jnp.stack([j >= HALF_WINDOW, j >= 0, j < kb - HALF_WINDOW])
    ok = band[None, None] & edge[:, None, None, :]
    tab = jnp.where(ok, alibi[None], NEG_INF)
    per_tile = MXU_DIM // HEAD_DIM
    return tab.reshape(3, HEADS // per_tile, per_tile * QUERY_BLOCK, kb)


def _merge_body(*refs):
    o_refs, l_refs = refs[:N_GROUPS], refs[N_GROUPS:2 * N_GROUPS]
    x_ref, wo_ref, spread_ref, rg_ref, wr_ref, br_ref, out_ref, hn_ref, aff_ref, o_s, l_s = refs[2 * N_GROUPS:]
    rows = x_ref.shape[0]
    tiles = ATTN_WIDTH // LANES
    for g, (_, dil) in enumerate(DIL_PATTERNS):
        for r in range(dil):
            rs = slice(None) if dil == 1 else pl.ds(r, rows // dil, stride=dil)
            l_s[g, rs, :] = l_refs[g][0, r]
            for c in range(tiles):
                o_s[g * tiles + c, rs, :] = o_refs[g][0, r, :, c * LANES:(c + 1) * LANES].astype(F32)
    lses = [l_s[g] for g in range(N_GROUPS)]
    mx = functools.reduce(jnp.maximum, lses)
    es = [jnp.exp(l - mx) for l in lses]
    den = functools.reduce(lambda a, b: a + b, es)
    merged = [jnp.zeros((rows, LANES), F32) for _ in range(tiles)]
    for g in range(N_GROUPS):
        w = es[g] / den
        hi = w.astype(BF16)
        lo = (w - hi.astype(F32)).astype(BF16)
        wide = _dot(hi, spread_ref[...]) + _dot(lo, spread_ref[...])
        for c in range(tiles):
            merged[c] = merged[c] + wide[:, c * LANES:(c + 1) * LANES] * o_s[g * tiles + c]
    merged = [m.astype(BF16) for m in merged]
    out = x_ref[...] + _dot(jnp.concatenate(merged, axis=1), wo_ref[...])
    out_ref[...] = out
    _route(out, rg_ref, wr_ref, br_ref, hn_ref, aff_ref)


def _attn_merge(outs, x, w_o, route):
    bsz, seq, d = x.shape
    rows = MERGE_ROWS
    args, in_specs = [], []
    for which, width in enumerate((ATTN_WIDTH, LANES)):
        for g, (_, dil) in enumerate(DIL_PATTERNS):
            args.append(outs[g][which])
            in_specs.append(pl.BlockSpec((1, dil, rows // dil, width), lambda b, i: (b, 0, i, 0)))
    lane_head = jnp.arange(ATTN_WIDTH) // HEAD_DIM
    spread = (jnp.arange(LANES)[:, None] == lane_head[None, :]).astype(BF16)
    args += [x, w_o, spread, *route]
    in_specs += [pl.BlockSpec((None, rows, d), lambda b, i: (b, i, 0)), _const_spec(w_o.shape),
                 _const_spec(spread.shape)]
    in_specs += [_const_spec(a.shape) for a in route]

    def row_block(width):
        return pl.BlockSpec((None, rows, width), lambda b, i: (b, i, 0))

    return pl.pallas_call(
        _merge_body,
        out_shape=[jax.ShapeDtypeStruct((bsz, seq, d), F32), jax.ShapeDtypeStruct((bsz, seq, d), BF16),
                   jax.ShapeDtypeStruct((bsz, seq, LANES), F32)],
        grid=(bsz, seq // rows),
        in_specs=in_specs,
        out_specs=[row_block(d), row_block(d), row_block(LANES)],
        scratch_shapes=[pltpu.VMEM((N_GROUPS * ATTN_WIDTH // LANES, rows, LANES), F32),
                        pltpu.VMEM((N_GROUPS, rows, LANES), F32)],
        compiler_params=_params("parallel", "parallel"),
        name="attn_merge",
    )(*args)


def _select_one(cap, s, aff_ref, tri_ref, post_ref, gatet_ref, pos_ref, lo_ref):
    seq = aff_ref.shape[1]
    aff_t = aff_ref[s].T[:N_EXPERTS]
    key = pltpu.bitcast(aff_t, I32)
    thr = jnp.zeros((N_EXPERTS, 1), I32)
    for bit in range(30, -1, -1):
        cand = thr | (1 << bit)
        cnt = jnp.sum(jnp.where(key >= cand, 1.0, 0.0), axis=1, keepdims=True)
        thr = jnp.where(cnt >= cap, cand, thr)
    gt = key > thr
    eq = key == thr
    need = cap - jnp.sum(jnp.where(gt, 1.0, 0.0), axis=1, keepdims=True)
    tri = tri_ref[...]
    nblk = seq // TOKEN_BLOCK

    def prefix(mask_f, want_offsets):
        off = jnp.zeros((N_EXPERTS, 1), F32)
        parts, offs = [], []
        for jb in range(nblk):
            blk = mask_f[:, jb * TOKEN_BLOCK:(jb + 1) * TOKEN_BLOCK]
            offs.append(off)
            parts.append(_dot(blk.astype(BF16), tri) + off)
            off = off + jnp.sum(blk, axis=1, keepdims=True)
        offs.append(off)
        return jnp.concatenate(parts, axis=1), (offs if want_offsets else None)

    eq_rank, _ = prefix(jnp.where(eq, 1.0, 0.0), False)
    sel = gt | (eq & (eq_rank < need))
    rank, offs = prefix(jnp.where(sel, 1.0, 0.0), True)
    post = jnp.where(sel, rank, -1.0)
    post_ref[s] = post.astype(I32)
    gatet_ref[s] = jnp.where(sel, aff_t, 0.0)
    padded = jnp.concatenate([post, jnp.full((LANES - N_EXPERTS, seq), -1.0, F32)], axis=0)
    pos_ref[s] = padded.T.astype(I32)
    lane = lax.broadcasted_iota(I32, (N_EXPERTS, LANES), 1)
    lo = jnp.zeros((N_EXPERTS, LANES), F32)
    for jb, off in enumerate(offs):
        lo = jnp.where(lane == jb, off, lo)
    lo_ref[s] = lo.astype(I32)


def _select_body(cap, aff_ref, *refs):
    for s in range(aff_ref.shape[0]):
        _select_one(cap, s, aff_ref, *refs)


def _select(aff, tri, cap):
    bsz, seq, _ = aff.shape
    n = SELECT_SEQS if bsz % SELECT_SEQS == 0 else 1
    return pl.pallas_call(
        functools.partial(_select_body, cap),
        out_shape=[jax.ShapeDtypeStruct((bsz, N_EXPERTS, seq), I32),
                   jax.ShapeDtypeStruct((bsz, N_EXPERTS, seq), F32),
                   jax.ShapeDtypeStruct((bsz, seq, LANES), I32),
                   jax.ShapeDtypeStruct((bsz, N_EXPERTS, LANES), I32)],
        grid=(bsz // n,),
        in_specs=[pl.BlockSpec((n, seq, LANES), lambda b: (b, 0, 0)), _const_spec(tri.shape)],
        out_specs=[pl.BlockSpec((n, N_EXPERTS, seq), lambda b: (b, 0, 0)),
                   pl.BlockSpec((n, N_EXPERTS, seq), lambda b: (b, 0, 0)),
                   pl.BlockSpec((n, seq, LANES), lambda b: (b, 0, 0)),
                   pl.BlockSpec((n, N_EXPERTS, LANES), lambda b: (b, 0, 0))],
        compiler_params=_params("parallel"),
        name="moe_select",
    )(aff, tri)


def _gather_body(lo_ref, hn_ref, post_ref, gatet_ref, xin_ref, gate_ref):
    b = pl.program_id(0)
    eg = pl.program_id(1)
    seq = hn_ref.shape[1]
    cap = xin_ref.shape[2]
    nblk = seq // TOKEN_BLOCK
    xin_ref[...] = jnp.zeros_like(xin_ref)
    gate_ref[...] = jnp.zeros_like(gate_ref)

    def table(i):
        return (b * N_EXPERTS + eg * GATHER_PACK + i) * LO_STRIDE

    def window(i, ts, s0, rows, iota):
        s0c = pl.multiple_of(jnp.minimum(s0, cap - rows), BF16_ROWS)
        slot = iota + s0c
        hit = (post_ref[0, i, :, ts] == slot) & (slot >= s0)

        def add(contrib):
            xin_ref[0, i, pl.ds(s0c, rows), :] += contrib.astype(BF16)
            gsum = jnp.sum(jnp.where(hit, gatet_ref[0, i, :, ts], 0.0), axis=1, keepdims=True)
            gate_ref[0, i, pl.ds(s0c, rows), :] += jnp.broadcast_to(gsum, (rows, LANES))

        return jnp.where(hit, 1.0, 0.0).astype(BF16), add

    first_iota = lax.broadcasted_iota(I32, (FIRST_WINDOW, TOKEN_BLOCK), 0)
    for j in range(nblk):
        ts = slice(j * TOKEN_BLOCK, (j + 1) * TOKEN_BLOCK)
        rows, adds = zip(*[window(i, ts, _floor_to(lo_ref[table(i) + j], BF16_ROWS), FIRST_WINDOW, first_iota)
                           for i in range(GATHER_PACK)])
        contrib = _dot(jnp.concatenate(rows, axis=0), hn_ref[0, ts, :])
        for i in range(GATHER_PACK):
            adds[i](contrib[i * FIRST_WINDOW:(i + 1) * FIRST_WINDOW])

    def left_over(i, j):
        done = _floor_to(lo_ref[table(i) + j], BF16_ROWS) + FIRST_WINDOW
        return jnp.maximum(lo_ref[table(i) + j + 1] - done, 0), done

    most = functools.reduce(jnp.maximum, [left_over(i, j)[0] for i in range(GATHER_PACK) for j in range(nblk)])

    @pl.when(most > 0)
    def _():
        more_iota = lax.broadcasted_iota(I32, (GATHER_WINDOW, TOKEN_BLOCK), 0)
        for j in range(nblk):
            ts = slice(j * TOKEN_BLOCK, (j + 1) * TOKEN_BLOCK)

            def expert(i, carry, j=j, ts=ts):
                left, done = left_over(i, j)

                def further(w, carry):
                    onehot, add = window(i, ts, done + w * GATHER_WINDOW, GATHER_WINDOW, more_iota)
                    add(_dot(onehot, hn_ref[0, ts, :]))
                    return carry

                return lax.fori_loop(0, _ceil_div(left, GATHER_WINDOW), further, carry)

            lax.fori_loop(0, GATHER_PACK, expert, 0)


def _gather(lo_flat, hn, post, gatet, cap):
    bsz, seq, d = hn.shape
    post4 = post.reshape(bsz, N_EXPERTS, 1, seq)
    gatet4 = gatet.reshape(bsz, N_EXPERTS, 1, seq)
    grid_spec = pltpu.PrefetchScalarGridSpec(
        num_scalar_prefetch=1,
        grid=(bsz, N_EXPERTS // GATHER_PACK),
        in_specs=[pl.BlockSpec((1, seq, d), lambda b, e, lo: (b, 0, 0)),
                  pl.BlockSpec((1, GATHER_PACK, 1, seq), lambda b, e, lo: (b, e, 0, 0)),
                  pl.BlockSpec((1, GATHER_PACK, 1, seq), lambda b, e, lo: (b, e, 0, 0))],
        out_specs=[pl.BlockSpec((1, GATHER_PACK, cap, d), lambda b, e, lo: (b, e, 0, 0)),
                   pl.BlockSpec((1, GATHER_PACK, cap, LANES), lambda b, e, lo: (b, e, 0, 0))],
    )
    return pl.pallas_call(
        _gather_body,
        out_shape=[jax.ShapeDtypeStruct((bsz, N_EXPERTS, cap, d), BF16),
                   jax.ShapeDtypeStruct((bsz, N_EXPERTS, cap, LANES), F32)],
        grid_spec=grid_spec,
        compiler_params=_params("parallel", "arbitrary"),
        name="moe_gather",
    )(lo_flat, hn, post4, gatet4)


def _expert_body(xin_ref, gate_ref, w1c_ref, w3c_ref, w2c_ref, y_ref, w1_s, w3_s, w2_s):
    r = pl.program_id(0)
    step = pl.program_id(1)
    fill = lax.rem(r, 2)

    def cast_slice():
        d_rows = w1c_ref.shape[2]
        f_rows = w2c_ref.shape[2]
        d0 = pl.multiple_of(step * d_rows, d_rows)
        f0 = pl.multiple_of(step * f_rows, f_rows)
        w1_s[fill, pl.ds(d0, d_rows), :] = w1c_ref[0, 0].astype(BF16)
        w3_s[fill, pl.ds(d0, d_rows), :] = w3c_ref[0, 0].astype(BF16)
        w2_s[fill, pl.ds(f0, f_rows), :] = w2c_ref[0, 0].astype(BF16)

    @pl.when(r == 0)
    def _():
        cast_slice()
        y_ref[...] = jnp.zeros_like(y_ref)

    @pl.when(r > 0)
    def _():
        cast_slice()
        use = 1 - fill
        chunk = EXPERT_CHUNK
        for s in range(xin_ref.shape[0]):
            xin = xin_ref[s, 0]
            y = jnp.zeros(y_ref.shape[2:], F32)
            for c in range(w1_s.shape[2] // chunk):
                fs = slice(c * chunk, (c + 1) * chunk)
                hid = jax.nn.silu(_dot(xin, w1_s[use, :, fs])) * _dot(xin, w3_s[use, :, fs])
                y = y + _dot(hid.astype(BF16), w2_s[use, fs, :])
            y_ref[s, 0] = (y * gate_ref[s, 0, :, 0:1]).astype(BF16)


def _experts(layer, xin, gate, w1, w3, w2):
    bsz, ne, cap, d = xin.shape
    f = w1.shape[3]
    seqs = EXPERT_SEQS if bsz % EXPERT_SEQS == 0 else 1
    steps = bsz // seqs
    d_rows, f_rows = d // steps, f // steps
    assert d_rows * steps == d and f_rows * steps == f

    def act(width):
        return pl.BlockSpec((seqs, 1, cap, width), lambda r, b: (b, jnp.maximum(r - 1, 0), 0, 0))

    out_spec = pl.BlockSpec((seqs, 1, cap, d), lambda r, b: (b, r, 0, 0))

    def weight_slice(rows, cols):
        return pl.BlockSpec((1, 1, rows, cols), lambda r, b: (layer, jnp.minimum(r, ne - 1), b, 0))

    return pl.pallas_call(
        _expert_body,
        out_shape=jax.ShapeDtypeStruct((bsz, ne + 1, cap, d), BF16),
        grid=(ne + 1, steps),
        in_specs=[act(d), act(LANES), weight_slice(d_rows, f), weight_slice(d_rows, f), weight_slice(f_rows, d)],
        out_specs=out_spec,
        scratch_shapes=[pltpu.VMEM((2, d, f), BF16), pltpu.VMEM((2, d, f), BF16), pltpu.VMEM((2, f, d), BF16)],
        compiler_params=_params("arbitrary", "arbitrary"),
        name="moe_experts",
    )(xin, gate, w1, w3, w2)


def _combine_body(lo_ref, x_ref, y_ref, pos_ref, o_ref):
    b = pl.program_id(0)
    cap = y_ref.shape[2]
    lane = lax.broadcasted_iota(I32, (TOKEN_BLOCK, PACK * FIRST_WINDOW), 1)
    lane_expert = lane // FIRST_WINDOW
    lane_slot = lane % FIRST_WINDOW
    pending = []
    for sub in range(COMBINE_BLOCKS):
        j = pl.program_id(1) * COMBINE_BLOCKS + sub
        ts = slice(sub * TOKEN_BLOCK, (sub + 1) * TOKEN_BLOCK)
        pos = pos_ref[0, ts, :]

        def bounds(e, j=j):
            base = (b * N_EXPERTS + e) * LO_STRIDE
            return lo_ref[base + j], lo_ref[base + j + 1]

        acc = x_ref[0, ts, :]
        for q in range(N_EXPERTS // PACK):
            target = jnp.full(lane.shape, -1, I32)
            y_rows = []
            for i in range(PACK):
                e = q * PACK + i
                start = _floor_to(bounds(e)[0], BF16_ROWS)
                s0c = pl.multiple_of(jnp.minimum(start, cap - FIRST_WINDOW), BF16_ROWS)
                pcol = pos[:, e:e + 1]
                rel = jnp.where(pcol >= start, pcol - s0c, -1)
                target = jnp.where(lane_expert == i, rel, target)
                y_rows.append(y_ref[0, e + 1, pl.ds(s0c, FIRST_WINDOW), :])
            onehot = jnp.where(target == lane_slot, 1.0, 0.0).astype(BF16)
            acc = acc + _dot(onehot, jnp.concatenate(y_rows, axis=0))
        o_ref[0, ts, :] = acc

        def left_over(e, bounds=bounds):
            lo, hi = bounds(e)
            done = _floor_to(lo, BF16_ROWS) + FIRST_WINDOW
            return jnp.maximum(hi - done, 0), done

        pending.append((ts, pos, left_over))

    for ts, pos, left_over in pending:
        most = functools.reduce(jnp.maximum, [left_over(e)[0] for e in range(N_EXPERTS)])

        @pl.when(most > 0)
        def _(ts=ts, pos=pos, left_over=left_over):
            more_lane = lax.broadcasted_iota(I32, (TOKEN_BLOCK, SCATTER_WINDOW), 1)
            for e in range(N_EXPERTS):
                left, done = left_over(e)
                pcol = pos[:, e:e + 1]

                def window(w, carry, e=e, done=done, pcol=pcol):
                    s0 = done + w * SCATTER_WINDOW
                    s0c = pl.multiple_of(jnp.minimum(s0, cap - SCATTER_WINDOW), BF16_ROWS)
                    hit = ((pcol - s0c) == more_lane) & (pcol >= s0)
                    onehot = jnp.where(hit, 1.0, 0.0).astype(BF16)
                    o_ref[0, ts, :] += _dot(onehot, y_ref[0, e + 1, pl.ds(s0c, SCATTER_WINDOW), :])
                    return carry

                lax.fori_loop(0, _ceil_div(left, SCATTER_WINDOW), window, 0)


def _combine(lo_flat, x, y, pos):
    bsz, seq, d = x.shape
    _, slots, cap, _ = y.shape
    rows = COMBINE_BLOCKS * TOKEN_BLOCK
    grid_spec = pltpu.PrefetchScalarGridSpec(
        num_scalar_prefetch=1,
        grid=(bsz, seq // rows),
        in_specs=[pl.BlockSpec((1, rows, d), lambda b, j, lo: (b, j, 0)),
                  pl.BlockSpec((1, slots, cap, d), lambda b, j, lo: (b, 0, 0, 0)),
                  pl.BlockSpec((1, rows, LANES), lambda b, j, lo: (b, j, 0))],
        out_specs=pl.BlockSpec((1, rows, d), lambda b, j, lo: (b, j, 0)),
    )
    return pl.pallas_call(
        _combine_body,
        out_shape=jax.ShapeDtypeStruct((bsz, seq, d), F32),
        grid_spec=grid_spec,
        compiler_params=_params("parallel", "arbitrary"),
        name="moe_combine",
    )(lo_flat, x, y, pos)


def _ec_moe(layer, x, hn, aff, w1, w3, w2, tri):
    bsz, seq, d = x.shape
    cap = CAPACITY_FACTOR * seq // N_EXPERTS
    post, gatet, pos, lo = _select(aff, tri, cap)
    lo_flat = lo[:, :, :LO_STRIDE].reshape(-1)
    xin, gate = _gather(lo_flat, hn, post, gatet, cap)
    y = _experts(layer, xin, gate, w1, w3, w2)
    return _combine(lo_flat, x, y, pos)


def _group_major(a):
    lead = a.shape[:-1]
    a = a.reshape(lead + (3, N_GROUPS, ATTN_WIDTH))
    return jnp.swapaxes(a, -3, -2).reshape(lead + (3 * N_GROUPS * ATTN_WIDTH,))


def kernel(x, mix_norm, ffn_norm, gm_w_in, gm_b_in, gm_v_norm, gm_w_s, gm_b_s, gm_w_out, gm_b_out,
           at_w_qkv, at_q_norm, at_k_norm, at_w_o, moe_w_router, moe_b_router, moe_w1, moe_w3, moe_w2):
    bsz, seq, d = x.shape
    depth = mix_norm.shape[0]
    n_mixers = 2

    lane_head = jnp.arange(MXU_DIM) // HEAD_DIM
    head_ones = (lane_head[:, None] == lane_head[None, :]).astype(BF16)
    tri = (jnp.arange(TOKEN_BLOCK)[:, None] < jnp.arange(TOKEN_BLOCK)[None, :]).astype(BF16)
    biases = [_attn_bias(dil) for _, dil in DIL_PATTERNS]
    pad_e = LANES - N_EXPERTS

    for i in range(depth):
        j = i // n_mixers
        gain = mix_norm[i][None, :]
        w_r = jnp.pad(moe_w_router[i], ((0, 0), (0, pad_e)))
        w_hi = w_r.astype(BF16)
        w_split = jnp.concatenate([w_hi, (w_r - w_hi.astype(F32)).astype(BF16)], axis=1)
        b_r = jnp.pad(moe_b_router[i], (0, pad_e), constant_values=NEG_INF)[None, :]
        route = (ffn_norm[i][None, :], w_split, b_r)
        if i % n_mixers == 0:
            x, hn, aff = _gmlp_layer(
                x.reshape(bsz * seq, d), gain, gm_w_in[j].astype(BF16), gm_b_in[j][None, :],
                gm_v_norm[j][None, :], gm_w_s[j].astype(BF16), gm_b_s[j].T, gm_w_out[j].astype(BF16),
                gm_b_out[j][None, :], route)
        else:
            def per_head(g):
                return jnp.tile(g[:, None, :], (1, HEADS, 1)).reshape(N_GROUPS, ATTN_WIDTH)
            qk_gain = jnp.stack([per_head(at_q_norm[j]) * (HEAD_DIM ** -0.5), per_head(at_k_norm[j]),
                                 jnp.ones((N_GROUPS, ATTN_WIDTH), F32)]).reshape(1, -1)
            qkvs = _qkv_proj(x, gain, _group_major(at_w_qkv[j]).astype(BF16), _group_major(qk_gain), head_ones)
            outs = [_attn_group(g, qkvs[g], biases[g]) for g in range(N_GROUPS)]
            x, hn, aff = _attn_merge(outs, x, at_w_o[j].astype(BF16), route)
        x = _ec_moe(i, x.reshape(bsz, seq, d), hn.reshape(bsz, seq, d), aff.reshape(bsz, seq, LANES),
                    moe_w1, moe_w3, moe_w2, tri)
    return x
```
